```python
import math
import jax
import jax.numpy as jnp
from jax import lax
import numpy as np

D_MODEL = 1024
BATCH = 8
SEQ = 2048
DEPTH = 4
DEC_BATCH = 128
DEC_SEQ = 8
PAST_LEN = 2048
PAGE_SIZE = 128

N_MIXERS = 4
D_PLE = 256
N_BUCKETS = 32
MAX_DISTANCE = 128
QBLOCK = 128
EPS = 1e-6
NEG = -1e30

NSA_HEADS = 16
NSA_HEAD_DIM = D_MODEL // NSA_HEADS
NSA_KV_HEADS = 4
NSA_GROUP = NSA_HEADS // NSA_KV_HEADS
NSA_BLOCK = 32
NSA_TOPK = 16
NSA_WINDOW = 512
NSA_KV_WIDTH = NSA_KV_HEADS * NSA_HEAD_DIM
NSA_IN = 2 * D_MODEL + 6 * NSA_KV_WIDTH + 3 * NSA_HEADS

ML_HEADS = 4
ML_DV = D_MODEL // ML_HEADS
ML_DK = ML_DV // 2
ML_QK = ML_HEADS * ML_DK
ML_CHUNK = 64
ML_IN = 2 * ML_QK + 3 * D_MODEL + 2 * ML_HEADS

CONV_WIDTH = 31

DF_HEADS = 8
DF_DK = D_MODEL // (2 * DF_HEADS)
DF_DV = 2 * DF_DK

N_BIAS_HEADS = NSA_HEADS

N_NSA = (DEPTH + 3) // N_MIXERS
N_MLSTM = (DEPTH + 2) // N_MIXERS
N_CONV = (DEPTH + 1) // N_MIXERS
N_DIFF = DEPTH // N_MIXERS

kernel_name = "nsa_mlstm_conformer_diffattn_hybrid_step"


def rmsnorm(x, g):
    xf = x.astype(jnp.float32)
    y = xf * lax.rsqrt(jnp.mean(xf * xf, -1, keepdims=True) + EPS)
    return (y * g.astype(jnp.float32)).astype(x.dtype)


def layernorm(x, g, b):
    xf = x.astype(jnp.float32)
    xc = xf - jnp.mean(xf, -1, keepdims=True)
    var = jnp.mean(xc * xc, -1, keepdims=True)
    return (xc * lax.rsqrt(var + EPS) * g.astype(jnp.float32) + b.astype(jnp.float32)).astype(x.dtype)


def masked_softmax(s, mask):
    s = jnp.where(mask, s.astype(jnp.float32), NEG)
    s = s - jnp.max(s, -1, keepdims=True)
    e = jnp.where(mask, jnp.exp(s), 0.0)
    return e / jnp.maximum(jnp.sum(e, -1, keepdims=True), 1e-30)


def t5_bucket(dist):
    d = jnp.maximum(dist, 0)
    exact = N_BUCKETS // 2
    df = jnp.maximum(d, 1).astype(jnp.float32)
    large = exact + (jnp.log(df / exact) / math.log(MAX_DISTANCE / exact) * (N_BUCKETS - exact)).astype(jnp.int32)
    return jnp.where(d < exact, d, jnp.minimum(large, N_BUCKETS - 1))


def gather_pages(pool, page_table):
    g = pool[page_table]
    return g.reshape((g.shape[0], g.shape[1] * g.shape[2]) + g.shape[3:])


def to_qblocks(a, qb):
    n, L = a.shape[0], a.shape[1]
    return jnp.moveaxis(a.reshape((n, L // qb, qb) + a.shape[2:]), 1, 0)


def nsa_mixer(xn, w_in, w_out, rel_bias, past_kv, win_buf, past_len):
    n, L, _ = xn.shape
    dt = xn.dtype
    KH, G, Dh, CB = NSA_KV_HEADS, NSA_GROUP, NSA_HEAD_DIM, NSA_BLOCK
    u = xn @ w_in
    q, kv, g, z = jnp.split(u, [D_MODEL, D_MODEL + 6 * NSA_KV_WIDTH, D_MODEL + 6 * NSA_KV_WIDTH + 3 * NSA_HEADS], axis=-1)
    q = q.reshape(n, L, KH, G, Dh)
    kv = kv.reshape(n, L, 6, KH, Dh)
    g = jax.nn.sigmoid(g.astype(jnp.float32)).reshape(n, L, 3, NSA_HEADS)
    new_rows = kv[:, :, :4]
    full = new_rows if past_kv is None else jnp.concatenate([past_kv.astype(dt), new_rows], 1)
    T = past_len + L
    q_pos = past_len + jnp.arange(L)
    ctx = jnp.concatenate([win_buf.astype(dt), kv[:, :, 4:]], 1)
    w_pos = past_len - win_buf.shape[1] + jnp.arange(ctx.shape[1])
    scale = Dh ** -0.5

    nb = -(-T // CB)
    padded = jnp.pad(full, ((0, 0), (0, nb * CB - T), (0, 0), (0, 0), (0, 0)))
    blocks = padded.reshape(n, nb, CB, 4, KH, Dh)
    means = jnp.mean(blocks[:, :, :, :2], axis=2)
    block_end = jnp.arange(nb) * CB + CB - 1
    s_c = jnp.einsum('nqgrd,nbgd->nqgrb', q, means[:, :, 0]) * scale
    bias_c = rel_bias[t5_bucket(q_pos[:, None] - block_end[None, :])].reshape(L, nb, KH, G).transpose(0, 2, 3, 1)
    mask_c = (block_end[None, :] <= q_pos[:, None])[:, None, None, :]
    p_c = masked_softmax(s_c + bias_c, mask_c)
    o_c = jnp.einsum('nqgrb,nbgd->nqgrd', p_c.astype(dt), means[:, :, 1])

    imp = jnp.sum(p_c, axis=3)
    cur = (q_pos // CB)[:, None, None]
    bidx = jnp.arange(nb)
    sel = jnp.where(bidx == cur, float(NSA_GROUP + 1), jnp.where(bidx < cur, imp, -1.0))
    k_sel = min(NSA_TOPK, nb)
    _, idx = lax.top_k(sel, k_sel)
    ks_blk = blocks[:, :, :, 2].transpose(0, 3, 1, 2, 4)
    vs_blk = blocks[:, :, :, 3].transpose(0, 3, 1, 2, 4)

    qb = math.gcd(L, QBLOCK)
    nq = L // qb
    lw = ctx.shape[1] - L + qb
    bi = jnp.arange(n)[:, None, None, None]
    gi = jnp.arange(KH)[None, None, :, None]
    gi5 = jnp.arange(KH)[None, None, :, None, None]
    tab = rel_bias.reshape(N_BUCKETS, KH, G)
    nsel = k_sel * CB

    def block_fn(args):
        qblk, iblk, pos, start = args
        ks = ks_blk[bi, gi, iblk]
        vs = vs_blk[bi, gi, iblk]
        spos = iblk[..., None] * CB + jnp.arange(CB)
        dist = pos[None, :, None, None, None] - spos
        bias_s = jnp.moveaxis(tab[t5_bucket(dist), gi5], -1, 3).reshape(n, qb, KH, G, nsel)
        s_s = jnp.einsum('nqgrd,nqgkcd->nqgrkc', qblk, ks).reshape(n, qb, KH, G, nsel) * scale
        mask_s = (dist >= 0).reshape(n, qb, KH, nsel)[:, :, :, None, :]
        p_s = masked_softmax(s_s + bias_s, mask_s)
        o_s = jnp.einsum('nqgrj,nqgjd->nqgrd', p_s.astype(dt), vs.reshape(n, qb, KH, nsel, Dh))
        wkv = lax.dynamic_slice_in_dim(ctx, start, lw, axis=1)
        wp = lax.dynamic_slice_in_dim(w_pos, start, lw)
        dw = pos[:, None] - wp[None, :]
        bias_w = rel_bias[t5_bucket(dw)].reshape(qb, lw, KH, G).transpose(0, 2, 3, 1)
        mask_w = ((dw >= 0) & (dw <= NSA_WINDOW) & (wp[None, :] >= 0))[:, None, None, :]
        s_w = jnp.einsum('nqgrd,nsgd->nqgrs', qblk, wkv[:, :, 0]) * scale
        p_w = masked_softmax(s_w + bias_w, mask_w)
        o_w = jnp.einsum('nqgrs,nsgd->nqgrd', p_w.astype(dt), wkv[:, :, 1])
        return o_s, o_w

    o_s, o_w = lax.map(block_fn, (to_qblocks(q, qb), to_qblocks(idx, qb), q_pos.reshape(nq, qb), jnp.arange(nq) * qb))
    o_s = jnp.moveaxis(o_s, 0, 1).reshape(n, L, NSA_HEADS, Dh)
    o_w = jnp.moveaxis(o_w, 0, 1).reshape(n, L, NSA_HEADS, Dh)
    o = (g[:, :, 0, :, None] * o_c.reshape(n, L, NSA_HEADS, Dh)
         + g[:, :, 1, :, None] * o_s + g[:, :, 2, :, None] * o_w).astype(dt)
    y = (o.reshape(n, L, D_MODEL) * jax.nn.silu(z)) @ w_out
    keep = min(NSA_WINDOW, past_len + L)
    return y, new_rows, ctx[:, -keep:]


def mlstm_cell(q, k, v, li, lf, c0, n0, m0):
    n, L, H, _ = q.shape
    c = math.gcd(L, ML_CHUNK)
    nc = L // c

    def chunks(a):
        return jnp.swapaxes(jnp.moveaxis(a.reshape((n, nc, c) + a.shape[2:]), 1, 0), 2, 3)

    tri = jnp.tril(jnp.ones((c, c), dtype=bool))

    def step(carry, xs):
        C, nv, m = carry
        qc, kc, vc, ic, fc = xs
        F = jnp.cumsum(fc, -1)
        Dm = jnp.where(tri, F[..., :, None] - F[..., None, :] + ic[..., None, :], NEG)
        inter = F + m[..., None]
        mt = jnp.maximum(jnp.max(Dm, -1), inter)
        W = jnp.where(tri, jnp.exp(Dm - mt[..., None]), 0.0)
        S = jnp.einsum('nhtk,nhsk->nhts', qc, kc) * W
        a = jnp.exp(inter - mt)
        num = jnp.einsum('nhts,nhsv->nhtv', S, vc) + a[..., None] * jnp.einsum('nhvk,nhtk->nhtv', C, qc)
        den = jnp.sum(S, -1) + a * jnp.einsum('nhk,nhtk->nht', nv, qc)
        h = num / jnp.maximum(jnp.abs(den), jnp.exp(-mt))[..., None]
        m_new = mt[..., -1]
        wend = jnp.exp(F[..., -1:] - F + ic - m_new[..., None])
        decay = jnp.exp(inter[..., -1] - m_new)
        C_new = decay[..., None, None] * C + jnp.einsum('nhs,nhsv,nhsk->nhvk', wend, vc, kc)
        n_new = decay[..., None] * nv + jnp.einsum('nhs,nhsk->nhk', wend, kc)
        return (C_new, n_new, m_new), h

    (C, nv, m), hs = lax.scan(step, (c0, n0, m0), (chunks(q), chunks(k), chunks(v), chunks(li), chunks(lf)))
    h = jnp.transpose(hs, (1, 0, 3, 2, 4)).reshape(n, L, H, ML_DV)
    return h, C, nv, m


def mlstm_mixer(xn, w_in, b_if, head_g, w_out, c0, n0, m0):
    n, L, _ = xn.shape
    f32 = jnp.float32
    cuts = [ML_QK, 2 * ML_QK, 2 * ML_QK + D_MODEL, 2 * ML_QK + D_MODEL + ML_HEADS,
            2 * ML_QK + D_MODEL + 2 * ML_HEADS, 2 * ML_QK + 2 * D_MODEL + 2 * ML_HEADS]
    q, k, v, gi, gf, o, z = jnp.split(xn @ w_in, cuts, axis=-1)
    q = q.reshape(n, L, ML_HEADS, ML_DK).astype(f32)
    k = k.reshape(n, L, ML_HEADS, ML_DK).astype(f32) * (ML_DK ** -0.5)
    v = v.reshape(n, L, ML_HEADS, ML_DV).astype(f32)
    li = gi.astype(f32) + b_if[0].astype(f32)
    lf = jax.nn.log_sigmoid(gf.astype(f32) + b_if[1].astype(f32))
    h, C, nv, m = mlstm_cell(q, k, v, li, lf, c0.astype(f32), n0.astype(f32), m0.astype(f32))
    h = jax.nn.sigmoid(o.astype(f32)).reshape(n, L, ML_HEADS, ML_DV) * rmsnorm(h, head_g)
    y = (h.reshape(n, L, D_MODEL).astype(xn.dtype) * jax.nn.silu(z)) @ w_out
    return y, C, nv, m


def conv_mixer(xn, w_in, dw_w, dw_b, ln_g, ln_b, w_out, buf):
    a, b, z = jnp.split(xn @ w_in, 3, axis=-1)
    u = a * jax.nn.sigmoid(b)
    ctx = jnp.concatenate([buf.astype(u.dtype), u], 1)
    c = lax.conv_general_dilated(ctx, dw_w[:, None, :].astype(u.dtype), (1,), 'VALID',
                                 dimension_numbers=('NWC', 'WIO', 'NWC'),
                                 feature_group_count=D_MODEL) + dw_b
    c = jax.nn.silu(layernorm(c, ln_g, ln_b))
    y = (c * jax.nn.silu(z)) @ w_out
    return y, ctx[:, -(CONV_WIDTH - 1):]


def diff_mixer(xn, w_in, lam_p, head_g, w_out, rel_bias, past_kv, past_len, lam_init):
    n, L, _ = xn.shape
    dt = xn.dtype
    q, k, v, z = jnp.split(xn @ w_in, 4, axis=-1)
    q = q.reshape(n, L, DF_HEADS, 2, DF_DK)
    kv_new = jnp.stack([k.reshape(n, L, DF_HEADS, DF_DV), v.reshape(n, L, DF_HEADS, DF_DV)], 2)
    full = kv_new if past_kv is None else jnp.concatenate([past_kv.astype(dt), kv_new], 1)
    T = past_len + L
    keys = full[:, :, 0].reshape(n, T, DF_HEADS, 2, DF_DK)
    vals = full[:, :, 1]
    k_pos = jnp.arange(T)
    q_pos = past_len + jnp.arange(L)
    lp = lam_p.astype(jnp.float32)
    lam = jnp.exp(jnp.sum(lp[0] * lp[1])) - jnp.exp(jnp.sum(lp[2] * lp[3])) + lam_init
    scale = DF_DK ** -0.5
    qb = math.gcd(L, QBLOCK)
    nq = L // qb

    def block_fn(args):
        qblk, pos = args
        d = pos[:, None] - k_pos[None, :]
        bias = rel_bias[t5_bucket(d)].reshape(qb, T, DF_HEADS, 2).transpose(2, 3, 0, 1)
        s = jnp.einsum('nqhcd,nkhcd->nhcqk', qblk, keys) * scale
        p = masked_softmax(s + bias, d >= 0)
        a = (p[:, :, 0] - lam * p[:, :, 1]).astype(dt)
        return jnp.einsum('nhqk,nkhv->nqhv', a, vals)

    o = lax.map(block_fn, (to_qblocks(q, qb), q_pos.reshape(nq, qb)))
    o = jnp.moveaxis(o, 0, 1).reshape(n, L, DF_HEADS, DF_DV)
    o = rmsnorm(o, head_g) * (1.0 - lam_init)
    y = (o.reshape(n, L, D_MODEL) * jax.nn.silu(z)) @ w_out
    return y, kv_new


def trunk(x, p, past_len, page_table, nsa_pool, nsa_win, ml_C, ml_n, ml_m, conv_buf, diff_pool, weights):
    (norm_g, final_norm_g, rel_bias, ple_w, ple_gate_w, ple_norm_g,
     nsa_w_in, nsa_w_out, mlstm_w_in, mlstm_b_if, mlstm_head_g, mlstm_w_out,
     conv_w_in, conv_dw_w, conv_dw_b, conv_ln_g, conv_ln_b, conv_w_out,
     diff_w_in, diff_lam, diff_head_g, diff_w_out) = weights
    n, L, _ = x.shape
    sample = page_table is not None
    out_kv, out_win, out_C, out_n, out_m, out_conv, out_diff = [], [], [], [], [], [], []
    for i in range(DEPTH):
        kind, j = i % N_MIXERS, i // N_MIXERS
        h = rmsnorm(x, norm_g[i])
        if kind == 0:
            past = gather_pages(nsa_pool[j], page_table) if sample else None
            buf = nsa_win[j] if sample else jnp.zeros((n, NSA_WINDOW, 2, NSA_KV_HEADS, NSA_HEAD_DIM), x.dtype)
            y, rows, win = nsa_mixer(h, nsa_w_in[j], nsa_w_out[j], rel_bias, past, buf, past_len)
            out_kv.append(rows)
            out_win.append(win)
        elif kind == 1:
            if sample:
                c0, n0, m0 = ml_C[j], ml_n[j], ml_m[j]
            else:
                c0 = jnp.zeros((n, ML_HEADS, ML_DV, ML_DK), jnp.float32)
                n0 = jnp.zeros((n, ML_HEADS, ML_DK), jnp.float32)
                m0 = jnp.zeros((n, ML_HEADS), jnp.float32)
            y, C, nv, m = mlstm_mixer(h, mlstm_w_in[j], mlstm_b_if[j], mlstm_head_g[j], mlstm_w_out[j], c0, n0, m0)
            out_C.append(C)
            out_n.append(nv)
            out_m.append(m)
        elif kind == 2:
            buf = conv_buf[j] if sample else jnp.zeros((n, CONV_WIDTH - 1, D_MODEL), x.dtype)
            y, nbuf = conv_mixer(h, conv_w_in[j], conv_dw_w[j], conv_dw_b[j], conv_ln_g[j], conv_ln_b[j], conv_w_out[j], buf)
            out_conv.append(nbuf)
        else:
            past = gather_pages(diff_pool[j], page_table) if sample else None
            lam_init = 0.8 - 0.6 * math.exp(-0.3 * i)
            y, rows = diff_mixer(h, diff_w_in[j], diff_lam[j], diff_head_g[j], diff_w_out[j], rel_bias, past, past_len, lam_init)
            out_diff.append(rows)
        x = x + y
        gate = jax.nn.sigmoid(rmsnorm(x, ple_norm_g[i]) @ ple_gate_w[i])
        x = x + gate * (p[i] @ ple_w[i])
    return (rmsnorm(x, final_norm_g), jnp.stack(out_kv), jnp.stack(out_win), jnp.stack(out_C),
            jnp.stack(out_n), jnp.stack(out_m), jnp.stack(out_conv), jnp.stack(out_diff))


def setup_inputs(seed: int = 0) -> dict:
    key = jax.random.key(seed)
    ks = iter(jax.random.split(key, 48))
    f32 = jnp.float32

    def nrm(shape, scale):
        return jax.random.normal(next(ks), shape, f32) * scale

    n_pages = PAST_LEN // PAGE_SIZE
    n_used = DEC_BATCH * n_pages
    n_pool = n_used + n_used // 4
    wb = min(NSA_WINDOW, PAST_LEN)
    D = D_MODEL
    inputs = {}
    inputs['x_prompt'] = nrm((BATCH, SEQ, D), 1.0)
    inputs['x_sample'] = nrm((DEC_BATCH, DEC_SEQ, D), 1.0)
    inputs['cache_nsa_kv'] = nrm((N_NSA, n_pool, PAGE_SIZE, 4, NSA_KV_HEADS, NSA_HEAD_DIM), 1.0)
    inputs['state_nsa_win'] = nrm((N_NSA, DEC_BATCH, wb, 2, NSA_KV_HEADS, NSA_HEAD_DIM), 1.0)
    inputs['state_mlstm_C'] = nrm((N_MLSTM, DEC_BATCH, ML_HEADS, ML_DV, ML_DK), 0.5)
    inputs['state_mlstm_n'] = nrm((N_MLSTM, DEC_BATCH, ML_HEADS, ML_DK), 0.5)
    inputs['state_mlstm_m'] = nrm((N_MLSTM, DEC_BATCH, ML_HEADS), 1.0)
    inputs['state_conv'] = nrm((N_CONV, DEC_BATCH, CONV_WIDTH - 1, D), 0.5)
    inputs['cache_diff_kv'] = nrm((N_DIFF, n_pool, PAGE_SIZE, 2, DF_HEADS, DF_DV), 1.0)
    inputs['page_table'] = jax.random.permutation(next(ks), n_pool)[:n_used].reshape(DEC_BATCH, n_pages).astype(jnp.int32)
    inputs['p_prompt'] = nrm((DEPTH, BATCH, SEQ, D_PLE), 1.0)
    inputs['p_sample'] = nrm((DEPTH, DEC_BATCH, DEC_SEQ, D_PLE), 1.0)
    inputs['norm_g'] = 1.0 + nrm((DEPTH, D), 0.02)
    inputs['final_norm_g'] = 1.0 + nrm((D,), 0.02)
    inputs['rel_bias'] = nrm((N_BUCKETS, N_BIAS_HEADS), 0.5)
    inputs['ple_w'] = nrm((DEPTH, D_PLE, D), D_PLE ** -0.5)
    inputs['ple_gate_w'] = nrm((DEPTH, D, D), D ** -0.5)
    inputs['ple_norm_g'] = 1.0 + nrm((DEPTH, D), 0.02)
    inputs['nsa_w_in'] = nrm((N_NSA, D, NSA_IN), D ** -0.5)
    inputs['nsa_w_out'] = nrm((N_NSA, D, D), D ** -0.5)
    inputs['mlstm_w_in'] = nrm((N_MLSTM, D, ML_IN), D ** -0.5)
    inputs['mlstm_b_if'] = nrm((N_MLSTM, 2, ML_HEADS), 0.1) + jnp.array([0.0, 3.0], f32)[None, :, None]
    inputs['mlstm_head_g'] = 1.0 + nrm((N_MLSTM, ML_HEADS, ML_DV), 0.02)
    inputs['mlstm_w_out'] = nrm((N_MLSTM, D, D), D ** -0.5)
    inputs['conv_w_in'] = nrm((N_CONV, D, 3 * D), D ** -0.5)
    inputs['conv_dw_w'] = nrm((N_CONV, CONV_WIDTH, D), CONV_WIDTH ** -0.5)
    inputs['conv_dw_b'] = nrm((N_CONV, D), 0.02)
    inputs['conv_ln_g'] = 1.0 + nrm((N_CONV, D), 0.02)
    inputs['conv_ln_b'] = nrm((N_CONV, D), 0.02)
    inputs['conv_w_out'] = nrm((N_CONV, D, D), D ** -0.5)
    inputs['diff_w_in'] = nrm((N_DIFF, D, 4 * D), D ** -0.5)
    inputs['diff_lam'] = nrm((N_DIFF, 4, DF_DK), 0.1)
    inputs['diff_head_g'] = 1.0 + nrm((N_DIFF, DF_HEADS, DF_DV), 0.02)
    inputs['diff_w_out'] = nrm((N_DIFF, D, D), D ** -0.5)
    return inputs


def reference(x_prompt, x_sample, cache_nsa_kv, state_nsa_win, state_mlstm_C, state_mlstm_n, state_mlstm_m,
              state_conv, cache_diff_kv, page_table, p_prompt, p_sample, norm_g, final_norm_g, rel_bias,
              ple_w, ple_gate_w, ple_norm_g, nsa_w_in, nsa_w_out, mlstm_w_in, mlstm_b_if, mlstm_head_g,
              mlstm_w_out, conv_w_in, conv_dw_w, conv_dw_b, conv_ln_g, conv_ln_b, conv_w_out, diff_w_in,
              diff_lam, diff_head_g, diff_w_out):
    weights = (norm_g, final_norm_g, rel_bias, ple_w, ple_gate_w, ple_norm_g,
               nsa_w_in, nsa_w_out, mlstm_w_in, mlstm_b_if, mlstm_head_g, mlstm_w_out,
               conv_w_in, conv_dw_w, conv_dw_b, conv_ln_g, conv_ln_b, conv_w_out,
               diff_w_in, diff_lam, diff_head_g, diff_w_out)
    past_len = page_table.shape[1] * cache_nsa_kv.shape[2]
    y_prompt, kv_p, win_p, C_p, n_p, m_p, conv_p, diff_p = trunk(
        x_prompt, p_prompt, 0, None, None, None, None, None, None, None, None, weights)
    y_sample, kv_s, win_s, C_s, n_s, m_s, conv_s, diff_s = trunk(
        x_sample, p_sample, past_len, page_table, cache_nsa_kv, state_nsa_win, state_mlstm_C,
        state_mlstm_n, state_mlstm_m, state_conv, cache_diff_kv, weights)
    return (y_prompt, y_sample, kv_p, kv_s, win_p, win_s, C_p, C_s, n_p, n_s, m_p, m_s, conv_p, conv_s, diff_p, diff_s)
```

```python
import functools
import math

import jax
import jax.numpy as jnp
import numpy as np
from jax import lax
from jax.experimental import pallas as pl
from jax.experimental.pallas import tpu as pltpu

F32 = jnp.float32
BF16 = jnp.bfloat16

D_MODEL = 1024
D_PLE = 256
N_BUCKETS = 32
MAX_DISTANCE = 128
EPS = 1e-6

NSA_HEADS = 16
NSA_KV_HEADS = 4
NSA_GROUP = 4
NSA_DH = 64
NSA_BLOCK = 32
NSA_TOPK = 16
NSA_WINDOW = 512
NSA_KVW = NSA_KV_HEADS * NSA_DH

ML_HEADS = 4
ML_DV = 256
ML_DK = 128
ML_QK = ML_HEADS * ML_DK

CONV_WIDTH = 31
CONV_HALO = 32

DF_HEADS = 8
DF_DK = 64
DF_DV = 128

LANES = 128
MASKED = -2e30
M_FLOOR = -1e30
VMEM_LIMIT = 48 * 1024 * 1024
ROW_TILE = 256
ATT_TILE = 256
COL_CHUNK = 512


def _cparams(*sem):
    return pltpu.CompilerParams(dimension_semantics=sem, vmem_limit_bytes=VMEM_LIMIT)


def _dot(a, b):
    return jnp.dot(a, b, preferred_element_type=F32)


def _dot_nt(a, b):
    return lax.dot_general(a, b, (((1,), (1,)), ((), ())), preferred_element_type=F32)


def _dot_tn(a, b):
    return lax.dot_general(a, b, (((0,), (0,)), ((), ())), preferred_element_type=F32)


def _dot_exact(a, b):
    return jnp.dot(a, b, preferred_element_type=F32, precision=lax.Precision.HIGHEST)


def _sigmoid(x):
    return 1.0 / (1.0 + jnp.exp(-x))


def _silu(x):
    return x * _sigmoid(x)


def _log_sigmoid(x):
    return jnp.minimum(x, 0.0) - jnp.log1p(jnp.exp(-jnp.abs(x)))


def _pad_cols(w, mult=LANES):
    pad = (-w.shape[-1]) % mult
    return w if pad == 0 else jnp.pad(w, [(0, 0)] * (w.ndim - 1) + [(0, pad)])


def _norm_linear_kernel(x_ref, g_ref, *refs, widths):
    w_refs, o_refs = refs[:len(widths)], refs[len(widths):]
    x = x_ref[...]
    xn = (x * lax.rsqrt(jnp.mean(x * x, -1, keepdims=True) + EPS) * g_ref[...]).astype(BF16)
    for w_ref, o_ref, width in zip(w_refs, o_refs, widths):
        for c0 in range(0, width, COL_CHUNK):
            c1 = min(c0 + COL_CHUNK, width)
            o_ref[:, c0:c1] = _dot(xn, w_ref[:, c0:c1])


def norm_linear(x2d, g, weights):
    m, d = x2d.shape
    ws = [_pad_cols(w).astype(BF16) for w in weights]
    widths = tuple(w.shape[1] for w in ws)
    tm = min(ROW_TILE, m)
    outs = pl.pallas_call(
        functools.partial(_norm_linear_kernel, widths=widths),
        grid=(m // tm,),
        in_specs=[pl.BlockSpec((tm, d), lambda i: (i, 0)), pl.BlockSpec((1, d), lambda i: (0, 0))]
        + [pl.BlockSpec((d, wd), lambda i: (0, 0)) for wd in widths],
        out_specs=[pl.BlockSpec((tm, wd), lambda i: (i, 0)) for wd in widths],
        out_shape=[jax.ShapeDtypeStruct((m, wd), F32) for wd in widths],
        compiler_params=_cparams("parallel"),
        name="norm_linear",
    )(x2d, g.reshape(1, d), *ws)
    return outs


def _post_kernel(*refs, n_a, final):
    a_refs = refs[:n_a]
    z_ref, x_ref, p_ref, wo_ref, png_ref, wg_ref, wp_ref = refs[n_a:n_a + 7]
    rest = refs[n_a + 7:]
    fng_ref = rest[0] if final else None
    o_ref = rest[-1]
    a = a_refs[0][...]
    for r in a_refs[1:]:
        a = a + r[...]
    h = (a * _silu(z_ref[...])).astype(BF16)
    x1 = x_ref[...] + _dot(h, wo_ref[...])
    r = (x1 * lax.rsqrt(jnp.mean(x1 * x1, -1, keepdims=True) + EPS) * png_ref[...]).astype(BF16)
    gate = _sigmoid(_dot(r, wg_ref[...]))
    x2 = x1 + gate * _dot(p_ref[...].astype(BF16), wp_ref[...])
    if final:
        x2 = x2 * lax.rsqrt(jnp.mean(x2 * x2, -1, keepdims=True) + EPS) * fng_ref[...]
    o_ref[...] = x2


def post_layer(a_list, z, x, p, w_out, ple_norm_g, ple_gate_w, ple_w, final_g=None):
    m, d = x.shape
    tm = min(ROW_TILE, m)
    final = final_g is not None
    row = lambda i: (i, 0)
    fix = lambda i: (0, 0)
    in_specs = [pl.BlockSpec((tm, d), row) for _ in a_list] + [
        pl.BlockSpec((tm, d), row), pl.BlockSpec((tm, d), row), pl.BlockSpec((tm, D_PLE), row),
        pl.BlockSpec((d, d), fix), pl.BlockSpec((1, d), fix), pl.BlockSpec((d, d), fix),
        pl.BlockSpec((D_PLE, d), fix)]
    args = list(a_list) + [z, x, p, w_out.astype(BF16), ple_norm_g.reshape(1, d), ple_gate_w.astype(BF16),
                           ple_w.astype(BF16)]
    if final:
        in_specs.append(pl.BlockSpec((1, d), fix))
        args.append(final_g.reshape(1, d))
    return pl.pallas_call(
        functools.partial(_post_kernel, n_a=len(a_list), final=final),
        grid=(m // tm,),
        in_specs=in_specs,
        out_specs=pl.BlockSpec((tm, d), row),
        out_shape=jax.ShapeDtypeStruct((m, d), F32),
        compiler_params=_cparams("parallel"),
        name="post_layer",
    )(*args)


def _t5_bucket(dist):
    d = jnp.maximum(dist, 0)
    exact = N_BUCKETS // 2
    df = jnp.maximum(d, 1).astype(F32)
    large = exact + (jnp.log(df / exact) / math.log(MAX_DISTANCE / exact) * (N_BUCKETS - exact)).astype(jnp.int32)
    return jnp.where(d < exact, d, jnp.minimum(large, N_BUCKETS - 1))


def _bias_table(rel_bias, dist, valid):
    b = jnp.moveaxis(rel_bias[_t5_bucket(dist)], -1, 0)
    return jnp.where(valid[None], b, MASKED).astype(F32)


def _toeplitz_tiles(rel_bias, t, n_delta, max_dist=None):
    i = jnp.arange(t)
    d = jnp.arange(n_delta)[:, None, None] * t + i[None, :, None] - i[None, None, :]
    valid = d >= 0
    if max_dist is not None:
        valid = valid & (d <= max_dist)
    return _bias_table(rel_bias, d, valid)


def _far_bucket_ok(t):
    d = np.arange(t + 1, t + 4096, dtype=np.float64)
    large = 16 + np.floor(np.log(d / 16) / math.log(MAX_DISTANCE / 16) * 16 - 1e-6)
    return bool(np.all(large >= N_BUCKETS - 1))


def _online_update(s, v_bf, m_ref, l_ref, acc_ref, idx):
    m_old = m_ref[idx]
    m_new = jnp.maximum(m_old, jnp.max(s, -1, keepdims=True))
    alpha = jnp.exp(m_old - m_new)
    p = jnp.exp(s - m_new)
    l_ref[idx] = alpha * l_ref[idx] + jnp.sum(p, -1, keepdims=True)
    acc_ref[idx] = alpha * acc_ref[idx] + _dot(p.astype(BF16), v_bf)
    m_ref[idx] = m_new


def _softmax_rows(s):
    m = jnp.maximum(jnp.max(s, -1, keepdims=True), M_FLOOR)
    e = jnp.exp(s - m)
    return e / jnp.maximum(jnp.sum(e, -1, keepdims=True), 1e-30)


def _select_blocks(imp, cur, k):
    lane = lax.broadcasted_iota(jnp.int32, imp.shape, 1).astype(F32)
    sel = jnp.where(lane == cur, float(NSA_GROUP + 1), jnp.where(lane < cur, imp, -1.0))
    chosen = jnp.zeros(imp.shape, F32)
    for _ in range(k):
        top = jnp.max(sel, -1, keepdims=True)
        first = jnp.min(jnp.where(sel == top, lane, float(LANES)), -1, keepdims=True)
        hit = lane == first
        chosen = jnp.where(hit, 1.0, chosen)
        sel = jnp.where(hit, -3.0, sel)
    return jnp.where(chosen > 0.0, 0.0, MASKED)


def _nsa_cmp_prompt_kernel(q_ref, rows_ref, g_ref, bias_ref, oc_ref, selb_ref, mk_scr, *, seq, tq, k_sel):
    i = pl.program_id(1)
    nb = seq // NSA_BLOCK

    @pl.when(i == 0)
    def _():
        means = jnp.sum(rows_ref[0].reshape(nb, NSA_BLOCK, 2 * NSA_KVW), axis=1) * (1.0 / NSA_BLOCK)
        mk_scr[...] = jnp.zeros(mk_scr.shape, F32)
        mk_scr[0:nb, :] = means

    q = q_ref[0]
    gate = _sigmoid(g_ref[0])
    pos = i * tq + lax.broadcasted_iota(jnp.int32, (tq, 1), 0)
    cur = (pos // NSA_BLOCK).astype(F32)
    for kh in range(NSA_KV_HEADS):
        mk = mk_scr[:, kh * NSA_DH:(kh + 1) * NSA_DH].astype(BF16)
        mv = mk_scr[:, NSA_KVW + kh * NSA_DH:NSA_KVW + (kh + 1) * NSA_DH].astype(BF16)
        imp = jnp.zeros((tq, LANES), F32)
        for g in range(NSA_GROUP):
            h = kh * NSA_GROUP + g
            qh = (q[:, h * NSA_DH:(h + 1) * NSA_DH] * (NSA_DH ** -0.5)).astype(BF16)
            p = _softmax_rows(_dot_nt(qh, mk) + bias_ref[h])
            imp = imp + p
            oc_ref[0, :, h * NSA_DH:(h + 1) * NSA_DH] = _dot(p.astype(BF16), mv) * gate[:, h:h + 1]
        selb_ref[0, kh] = _select_blocks(imp, cur, k_sel)


def nsa_cmp_prompt(q, rows, gates, rel_bias):
    n, seq, _ = q.shape
    tq = min(ATT_TILE, seq)
    nb = seq // NSA_BLOCK
    assert nb <= LANES
    t = jnp.arange(seq)[:, None]
    bend = jnp.arange(LANES)[None, :] * NSA_BLOCK + NSA_BLOCK - 1
    bias = _bias_table(rel_bias, t - bend, bend <= t)
    return pl.pallas_call(
        functools.partial(_nsa_cmp_prompt_kernel, seq=seq, tq=tq, k_sel=min(NSA_TOPK, nb)),
        grid=(n, seq // tq),
        in_specs=[pl.BlockSpec((1, tq, D_MODEL), lambda b, i: (b, i, 0)),
                  pl.BlockSpec((1, seq, 2 * NSA_KVW), lambda b, i: (b, 0, 0)),
                  pl.BlockSpec((1, tq, LANES), lambda b, i: (b, i, 0)),
                  pl.BlockSpec((NSA_HEADS, tq, LANES), lambda b, i: (0, i, 0))],
        out_specs=[pl.BlockSpec((1, tq, D_MODEL), lambda b, i: (b, i, 0)),
                   pl.BlockSpec((1, NSA_KV_HEADS, tq, LANES), lambda b, i: (b, 0, i, 0))],
        out_shape=[jax.ShapeDtypeStruct((n, seq, D_MODEL), F32),
                   jax.ShapeDtypeStruct((n, NSA_KV_HEADS, seq, LANES), F32)],
        scratch_shapes=[pltpu.VMEM((LANES, 2 * NSA_KVW), F32)],
        compiler_params=_cparams("parallel", "arbitrary"),
        name="nsa_cmp_prompt",
    )(q, rows, gates, bias)


def _nsa_flash_prompt_kernel(*refs, t, window, gate_col):
    if window:
        q_ref, k_ref, v_ref, bias_ref, g_ref, o_ref, qs_scr, m_scr, l_scr, acc_scr = refs
        selb_ref = None
    else:
        q_ref, k_ref, v_ref, bias_ref, g_ref, selb_ref, o_ref, qs_scr, m_scr, l_scr, acc_scr = refs
    qi = pl.program_id(2)
    kk = pl.program_id(3)
    n_win = NSA_WINDOW // t + 1
    ki = qi - (n_win - 1) + kk if window else kk
    first = jnp.maximum(n_win - 1 - qi, 0) if window else 0
    heads = 2 * NSA_GROUP

    @pl.when(kk == first)
    def _():
        m_scr[...] = jnp.full(m_scr.shape, M_FLOOR, F32)
        l_scr[...] = jnp.zeros(l_scr.shape, F32)
        acc_scr[...] = jnp.zeros(acc_scr.shape, F32)
        q = q_ref[0]
        for h in range(heads):
            qs_scr[h] = (q[:, h * NSA_DH:(h + 1) * NSA_DH] * (NSA_DH ** -0.5)).astype(BF16)

    @pl.when(jnp.logical_and(ki >= 0, ki <= qi))
    def _():
        k = k_ref[0]
        v = v_ref[0]
        if not window:
            blk = lax.broadcasted_iota(jnp.int32, (LANES, t), 0)
            key_blk = (ki * t + lax.broadcasted_iota(jnp.int32, (LANES, t), 1)) // NSA_BLOCK
            expand = jnp.where(blk == key_blk, 1.0, 0.0).astype(BF16)
        for kv in range(2):
            k_bf = k[:, kv * NSA_DH:(kv + 1) * NSA_DH].astype(BF16)
            v_bf = v[:, kv * NSA_DH:(kv + 1) * NSA_DH].astype(BF16)
            if not window:
                sel_bias = _dot(selb_ref[0, kv].astype(BF16), expand)
            for g in range(NSA_GROUP):
                h = kv * NSA_GROUP + g
                s = _dot_nt(qs_scr[h], k_bf) + bias_ref[h, 0]
                if not window:
                    s = s + sel_bias
                _online_update(s, v_bf, m_scr, l_scr, acc_scr, h)

    @pl.when(ki == qi)
    def _():
        u = pl.program_id(1)
        gate = _sigmoid(g_ref[0])
        for h in range(heads):
            col = gate_col + h
            g0 = jnp.where(u == 0, gate[:, col:col + 1], gate[:, col + heads:col + heads + 1])
            o_ref[0, :, h * NSA_DH:(h + 1) * NSA_DH] = acc_scr[h] / jnp.maximum(l_scr[h], 1e-30) * g0


def nsa_flash_prompt(q, kv_arr, k_blk, v_blk, gates, rel_bias, selb, window):
    n, seq, _ = q.shape
    t = min(ATT_TILE, seq)
    nq = seq // t
    heads = 2 * NSA_GROUP
    if window:
        assert NSA_WINDOW % t == 0
        n_delta = NSA_WINDOW // t + 1
        tiles = _toeplitz_tiles(rel_bias, t, n_delta, NSA_WINDOW)
        nkk = n_delta
        kidx = lambda qi, kk: jnp.maximum(qi - (n_delta - 1) + kk, 0)
        didx = lambda qi, kk: jnp.minimum(n_delta - 1 - kk, qi)
    else:
        assert _far_bucket_ok(t)
        n_delta = 3
        tiles = _toeplitz_tiles(rel_bias, t, n_delta)
        nkk = nq
        kidx = lambda qi, kk: jnp.minimum(kk, qi)
        didx = lambda qi, kk: jnp.minimum(qi - jnp.minimum(kk, qi), n_delta - 1)
    in_specs = [pl.BlockSpec((1, t, heads * NSA_DH), lambda b, u, qi, kk: (b, qi, u)),
                pl.BlockSpec((1, t, LANES), lambda b, u, qi, kk: (b, kidx(qi, kk), k_blk + u)),
                pl.BlockSpec((1, t, LANES), lambda b, u, qi, kk: (b, kidx(qi, kk), v_blk + u)),
                pl.BlockSpec((heads, 1, t, t), lambda b, u, qi, kk: (u, didx(qi, kk), 0, 0)),
                pl.BlockSpec((1, t, LANES), lambda b, u, qi, kk: (b, qi, 0))]
    args = [q, kv_arr, kv_arr, tiles, gates]
    if not window:
        in_specs.append(pl.BlockSpec((1, 2, t, LANES), lambda b, u, qi, kk: (b, u, qi, 0)))
        args.append(selb)
    return pl.pallas_call(
        functools.partial(_nsa_flash_prompt_kernel, t=t, window=window, gate_col=2 * NSA_HEADS if window else NSA_HEADS),
        grid=(n, 2, nq, nkk),
        in_specs=in_specs,
        out_specs=pl.BlockSpec((1, t, heads * NSA_DH), lambda b, u, qi, kk: (b, qi, u)),
        out_shape=jax.ShapeDtypeStruct((n, seq, D_MODEL), F32),
        scratch_shapes=[pltpu.VMEM((heads, t, NSA_DH), BF16), pltpu.VMEM((heads, t, 1), F32),
                        pltpu.VMEM((heads, t, 1), F32), pltpu.VMEM((heads, t, NSA_DH), F32)],
        compiler_params=_cparams("parallel", "parallel", "parallel", "arbitrary"),
        name="nsa_window_prompt" if window else "nsa_selected_prompt",
    )(*args)


def _diff_lambda(lam_ref, lam_init):
    lp = lam_ref[...]
    s1 = jnp.sum(lp[0:1] * lp[1:2], -1, keepdims=True)
    s2 = jnp.sum(lp[2:3] * lp[3:4], -1, keepdims=True)
    return jnp.exp(s1) - jnp.exp(s2) + lam_init


def _diff_flash_prompt_kernel(q_ref, k_ref, v_ref, bias_ref, lam_ref, hg_ref, o_ref,
                              qs_scr, m_scr, l_scr, acc_scr, *, lam_init):
    qi = pl.program_id(2)
    ki = pl.program_id(3)

    @pl.when(ki == 0)
    def _():
        m_scr[...] = jnp.full(m_scr.shape, M_FLOOR, F32)
        l_scr[...] = jnp.zeros(l_scr.shape, F32)
        acc_scr[...] = jnp.zeros(acc_scr.shape, F32)
        q = q_ref[0]
        for c in range(2):
            qs_scr[c] = (q[:, c * DF_DK:(c + 1) * DF_DK] * (DF_DK ** -0.5)).astype(BF16)

    @pl.when(ki <= qi)
    def _():
        k = k_ref[0]
        v_bf = v_ref[0].astype(BF16)
        for c in range(2):
            k_bf = k[:, c * DF_DK:(c + 1) * DF_DK].astype(BF16)
            s = _dot_nt(qs_scr[c], k_bf) + bias_ref[c, 0]
            _online_update(s, v_bf, m_scr, l_scr, acc_scr, c)

    @pl.when(ki == qi)
    def _():
        lam = _diff_lambda(lam_ref, lam_init)
        o = acc_scr[0] / jnp.maximum(l_scr[0], 1e-30) - lam * (acc_scr[1] / jnp.maximum(l_scr[1], 1e-30))
        o = o * lax.rsqrt(jnp.mean(o * o, -1, keepdims=True) + EPS) * hg_ref[0]
        o_ref[0] = o * (1.0 - lam_init)


def diff_flash_prompt(q, kv, lam_p, head_g, rel_bias, lam_init):
    n, seq, _ = q.shape
    t = min(ATT_TILE, seq)
    nq = seq // t
    assert _far_bucket_ok(t)
    n_delta = 3
    tiles = _toeplitz_tiles(rel_bias, t, n_delta)
    kidx = lambda qi, ki: jnp.minimum(ki, qi)
    didx = lambda qi, ki: jnp.minimum(qi - jnp.minimum(ki, qi), n_delta - 1)
    return pl.pallas_call(
        functools.partial(_diff_flash_prompt_kernel, lam_init=lam_init),
        grid=(n, DF_HEADS, nq, nq),
        in_specs=[pl.BlockSpec((1, t, DF_DV), lambda b, h, qi, ki: (b, qi, h)),
                  pl.BlockSpec((1, t, DF_DV), lambda b, h, qi, ki: (b, kidx(qi, ki), h)),
                  pl.BlockSpec((1, t, DF_DV), lambda b, h, qi, ki: (b, kidx(qi, ki), DF_HEADS + h)),
                  pl.BlockSpec((2, 1, t, t), lambda b, h, qi, ki: (h, didx(qi, ki), 0, 0)),
                  pl.BlockSpec((4, DF_DK), lambda b, h, qi, ki: (0, 0)),
                  pl.BlockSpec((1, 1, DF_DV), lambda b, h, qi, ki: (h, 0, 0))],
        out_specs=pl.BlockSpec((1, t, DF_DV), lambda b, h, qi, ki: (b, qi, h)),
        out_shape=jax.ShapeDtypeStruct((n, seq, D_MODEL), F32),
        scratch_shapes=[pltpu.VMEM((2, t, DF_DK), BF16), pltpu.VMEM((2, t, 1), F32),
                        pltpu.VMEM((2, t, 1), F32), pltpu.VMEM((2, t, DF_DV), F32)],
        compiler_params=_cparams("parallel", "parallel", "parallel", "arbitrary"),
        name="diff_flash_prompt",
    )(q, kv, kv, tiles, lam_p, head_g.reshape(DF_HEADS, 1, DF_DV))


def _mlstm_kernel(*refs, c, has_init):
    if has_init:
        (q_ref, k_ref, v_ref, gc_ref, gr_ref, og_ref, bc_ref, br_ref, hg_ref, c0_ref, n0_ref, m0_ref,
         a_ref, cout_ref, nout_ref, mout_ref, c_scr, n_scr, m_scr) = refs
    else:
        (q_ref, k_ref, v_ref, gc_ref, gr_ref, og_ref, bc_ref, br_ref, hg_ref,
         a_ref, cout_ref, nout_ref, mout_ref, c_scr, n_scr, m_scr) = refs
    j = pl.program_id(1)

    @pl.when(j == 0)
    def _():
        if has_init:
            c_scr[...] = c0_ref[0]
            n_scr[...] = n0_ref[0]
            m_scr[...] = m0_ref[0]
        else:
            c_scr[...] = jnp.zeros(c_scr.shape, F32)
            n_scr[...] = jnp.zeros(n_scr.shape, F32)
            m_scr[...] = jnp.zeros(m_scr.shape, F32)

    row = lax.broadcasted_iota(jnp.int32, (c, c), 0)
    col = lax.broadcasted_iota(jnp.int32, (c, c), 1)
    tri = row >= col
    gc = gc_ref[0] + bc_ref[...]
    gr = gr_ref[0] + br_ref[...]
    f_cols = _dot_exact(jnp.where(tri, 1.0, 0.0), _log_sigmoid(gc))
    f_rows = _dot_exact(_log_sigmoid(gr), jnp.where(row <= col, 1.0, 0.0))
    og = og_ref[0]
    for h in range(ML_HEADS):
        qc = q_ref[0, :, h * ML_DK:(h + 1) * ML_DK]
        kc = k_ref[0, :, h * ML_DK:(h + 1) * ML_DK] * (ML_DK ** -0.5)
        vc = v_ref[0, :, h * ML_DV:(h + 1) * ML_DV]
        q_bf, k_bf = qc.astype(BF16), kc.astype(BF16)
        f_col = f_cols[:, ML_HEADS + h:ML_HEADS + h + 1]
        f_row = f_rows[ML_HEADS + h:ML_HEADS + h + 1, :]
        i_col = gc[:, h:h + 1]
        i_row = gr[h:h + 1, :]
        m_prev = m_scr[h:h + 1, 0:1]
        dm = jnp.where(tri, f_col - f_row + i_row, -1e30)
        inter = f_col + m_prev
        mt = jnp.maximum(jnp.max(dm, -1, keepdims=True), inter)
        w = jnp.where(tri, jnp.exp(dm - mt), 0.0)
        s = _dot_nt(q_bf, k_bf) * w
        a = jnp.exp(inter - mt)
        c_old = c_scr[h]
        n_old = n_scr[h:h + 1, :]
        num = _dot(s.astype(BF16), vc.astype(BF16)) + a * _dot_nt(q_bf, c_old.astype(BF16))
        den = jnp.sum(s, -1, keepdims=True) + a * jnp.sum(qc * n_old, -1, keepdims=True)
        hh = num / jnp.maximum(jnp.abs(den), jnp.exp(-mt))
        m_new = mt[c - 1:c, :]
        wend = jnp.exp(f_col[c - 1:c, :] - f_col + i_col - m_new)
        decay = jnp.exp(inter[c - 1:c, :] - m_new)
        c_scr[h] = decay * c_old + _dot_tn((wend * vc).astype(BF16), k_bf)
        n_scr[h:h + 1, :] = decay * n_old + jnp.sum(wend * kc, 0, keepdims=True)
        m_scr[h:h + 1, :] = jnp.broadcast_to(m_new, (1, LANES))
        hn = hh * lax.rsqrt(jnp.mean(hh * hh, -1, keepdims=True) + EPS) * hg_ref[h:h + 1, :]
        a_ref[0, :, h * ML_DV:(h + 1) * ML_DV] = _sigmoid(og[:, h * ML_DV:(h + 1) * ML_DV]) * hn

    @pl.when(j == pl.num_programs(1) - 1)
    def _():
        cout_ref[0] = c_scr[...]
        nout_ref[0] = n_scr[...]
        mout_ref[0] = m_scr[...]


def mlstm_cell(q, k, v, gates, og, b_if, head_g, state):
    n, seq, _ = q.shape
    c = min(LANES, seq)
    nc = seq // c
    has_init = state is not None
    gates_t = jnp.swapaxes(gates[:, :, :2 * ML_HEADS], 1, 2)
    b_flat = b_if.reshape(2 * ML_HEADS)
    b_col = jnp.pad(b_flat, (0, LANES - 2 * ML_HEADS)).reshape(1, LANES)
    b_row = b_flat.reshape(2 * ML_HEADS, 1)
    tok = lambda w: pl.BlockSpec((1, c, w), lambda b, j: (b, j, 0))
    fix2 = lambda s: pl.BlockSpec(s, lambda b, j: (0, 0))
    in_specs = [tok(ML_QK), tok(ML_QK), tok(D_MODEL), tok(LANES),
                pl.BlockSpec((1, 2 * ML_HEADS, c), lambda b, j: (b, 0, j)), tok(D_MODEL),
                fix2((1, LANES)), fix2((2 * ML_HEADS, 1)), fix2((ML_HEADS, ML_DV))]
    args = [q, k, v, gates, gates_t, og, b_col, b_row, head_g]
    st4 = pl.BlockSpec((1, ML_HEADS, ML_DV, ML_DK), lambda b, j: (b, 0, 0, 0))
    st3 = pl.BlockSpec((1, ML_HEADS, LANES), lambda b, j: (b, 0, 0))
    if has_init:
        c0, n0, m0 = state
        in_specs += [st4, st3, st3]
        args += [c0, n0, jnp.broadcast_to(m0[:, :, None], (n, ML_HEADS, LANES))]
    a, c_new, n_new, m_new = pl.pallas_call(
        functools.partial(_mlstm_kernel, c=c, has_init=has_init),
        grid=(n, nc),
        in_specs=in_specs,
        out_specs=[tok(D_MODEL), st4, st3, st3],
        out_shape=[jax.ShapeDtypeStruct((n, seq, D_MODEL), F32),
                   jax.ShapeDtypeStruct((n, ML_HEADS, ML_DV, ML_DK), F32),
                   jax.ShapeDtypeStruct((n, ML_HEADS, ML_DK), F32),
                   jax.ShapeDtypeStruct((n, ML_HEADS, LANES), F32)],
        scratch_shapes=[pltpu.VMEM((ML_HEADS, ML_DV, ML_DK), F32), pltpu.VMEM((ML_HEADS, ML_DK), F32),
                        pltpu.VMEM((ML_HEADS, LANES), F32)],
        compiler_params=_cparams("parallel", "arbitrary"),
        name="mlstm",
    )(*args)
    return a, c_new, n_new, m_new[:, :, 0]


def _conv_kernel(a_ref, b_ref, buf_ref, w_ref, dwb_ref, lng_ref, lnb_ref, o_ref, nbuf_ref, ctx_scr, *, tt):
    j = pl.program_id(1)

    @pl.when(j == 0)
    def _():
        ctx_scr[0:CONV_HALO, :] = buf_ref[0]

    @pl.when(j > 0)
    def _():
        ctx_scr[0:CONV_HALO, :] = ctx_scr[tt:tt + CONV_HALO, :]

    ctx_scr[CONV_HALO:CONV_HALO + tt, :] = a_ref[0] * _sigmoid(b_ref[0])
    off = CONV_HALO - (CONV_WIDTH - 1)
    acc = jnp.zeros((tt, D_MODEL), F32) + dwb_ref[...]
    for tap in range(CONV_WIDTH):
        acc = acc + ctx_scr[off + tap:off + tap + tt, :] * w_ref[tap:tap + 1, :]
    mu = jnp.mean(acc, -1, keepdims=True)
    xc = acc - mu
    var = jnp.mean(xc * xc, -1, keepdims=True)
    o_ref[0] = _silu(xc * lax.rsqrt(var + EPS) * lng_ref[...] + lnb_ref[...])
    nbuf_ref[0] = ctx_scr[tt:tt + CONV_HALO, :]


def conv_block(a, b, buf, dw_w, dw_b, ln_g, ln_b):
    n, seq, d = a.shape
    tt = min(ATT_TILE, seq)
    assert seq == tt or tt >= CONV_HALO
    hist = CONV_WIDTH - 1
    if buf is None:
        buf_pad = jnp.zeros((n, CONV_HALO, d), F32)
    else:
        buf_pad = jnp.pad(buf, ((0, 0), (CONV_HALO - hist, 0), (0, 0)))
    w_pad = jnp.pad(dw_w, ((0, CONV_HALO - CONV_WIDTH), (0, 0)))
    tok = pl.BlockSpec((1, tt, d), lambda bb, j: (bb, j, 0))
    vec = pl.BlockSpec((1, d), lambda bb, j: (0, 0))
    halo = pl.BlockSpec((1, CONV_HALO, d), lambda bb, j: (bb, 0, 0))
    out, nbuf = pl.pallas_call(
        functools.partial(_conv_kernel, tt=tt),
        grid=(n, seq // tt),
        in_specs=[tok, tok, halo, pl.BlockSpec((CONV_HALO, d), lambda bb, j: (0, 0)), vec, vec, vec],
        out_specs=[tok, halo],
        out_shape=[jax.ShapeDtypeStruct((n, seq, d), F32), jax.ShapeDtypeStruct((n, CONV_HALO, d), F32)],
        scratch_shapes=[pltpu.VMEM((CONV_HALO + tt, d), F32)],
        compiler_params=_cparams("parallel", "arbitrary"),
        name="conv_block",
    )(a, b, buf_pad, w_pad, dw_b.reshape(1, d), ln_g.reshape(1, d), ln_b.reshape(1, d))
    return out, nbuf[:, CONV_HALO - hist:, :]


def _page_means_kernel(pt_ref, page_ref, o_ref):
    x = page_ref[0]
    blocks = x.shape[0] // NSA_BLOCK
    o_ref[0, 0] = jnp.sum(x.reshape(blocks, NSA_BLOCK, x.shape[1]), axis=1) * (1.0 / NSA_BLOCK)


def nsa_page_means(pool, page_table):
    n, n_pages = page_table.shape
    page = pool.shape[1]
    blocks = page // NSA_BLOCK
    out = pl.pallas_call(
        _page_means_kernel,
        grid_spec=pltpu.PrefetchScalarGridSpec(
            num_scalar_prefetch=1, grid=(n, n_pages),
            in_specs=[pl.BlockSpec((1, page, 2 * NSA_KVW), lambda b, p, pt: (pt[b, p], 0, 0))],
            out_specs=pl.BlockSpec((1, 1, blocks, 2 * NSA_KVW), lambda b, p, pt: (b, p, 0, 0))),
        out_shape=jax.ShapeDtypeStruct((n, n_pages, blocks, 2 * NSA_KVW), F32),
        compiler_params=_cparams("parallel", "arbitrary"),
        name="nsa_page_means",
    )(page_table, pool)
    return out.reshape(n, n_pages * blocks, 2 * NSA_KVW)


def _stack_group_queries(q, kh):
    parts = [q[:, (kh * NSA_GROUP + g) * NSA_DH:(kh * NSA_GROUP + g + 1) * NSA_DH] for g in range(NSA_GROUP)]
    return (jnp.concatenate(parts, axis=0) * (NSA_DH ** -0.5)).astype(BF16)


def _nsa_sample_local_kernel(q_ref, means_ref, rows_ref, win_ref, neww_ref, g_ref, bc_ref, bw_ref,
                             oc_ref, ow_ref, selb_ref, wout_ref, mk_scr, ctx_scr, *, past, seq, k_sel):
    nbp = means_ref.shape[1]
    wb = win_ref.shape[1]
    mk_scr[...] = jnp.zeros(mk_scr.shape, F32)
    mk_scr[0:nbp, :] = means_ref[0]
    mk_scr[nbp:nbp + 1, :] = jnp.sum(rows_ref[0][:, 0:2 * NSA_KVW], 0, keepdims=True) * (1.0 / NSA_BLOCK)
    ctx_scr[...] = jnp.zeros(ctx_scr.shape, F32)
    ctx_scr[0:wb, :] = win_ref[0]
    ctx_scr[wb:wb + seq, :] = neww_ref[0]
    wout_ref[0] = ctx_scr[seq:seq + wb, :]
    q = q_ref[0]
    gate = _sigmoid(g_ref[0])
    pos = past + lax.broadcasted_iota(jnp.int32, (seq, 1), 0)
    cur = (pos // NSA_BLOCK).astype(F32)
    for kh in range(NSA_KV_HEADS):
        qs = _stack_group_queries(q, kh)
        lo, hi = kh * NSA_DH, (kh + 1) * NSA_DH
        p = _softmax_rows(_dot_nt(qs, mk_scr[:, lo:hi].astype(BF16)) + bc_ref[kh])
        oc = _dot(p.astype(BF16), mk_scr[:, NSA_KVW + lo:NSA_KVW + hi].astype(BF16))
        imp = p[0:seq]
        for g in range(1, NSA_GROUP):
            imp = imp + p[g * seq:(g + 1) * seq]
        selb_ref[0, kh] = _select_blocks(imp, cur, k_sel)
        pw = _softmax_rows(_dot_nt(qs, ctx_scr[:, lo:hi].astype(BF16)) + bw_ref[kh])
        ow = _dot(pw.astype(BF16), ctx_scr[:, NSA_KVW + lo:NSA_KVW + hi].astype(BF16))
        for g in range(NSA_GROUP):
            h = kh * NSA_GROUP + g
            oc_ref[0, :, h * NSA_DH:(h + 1) * NSA_DH] = oc[g * seq:(g + 1) * seq] * gate[:, h:h + 1]
            ow_ref[0, :, h * NSA_DH:(h + 1) * NSA_DH] = (
                ow[g * seq:(g + 1) * seq] * gate[:, 2 * NSA_HEADS + h:2 * NSA_HEADS + h + 1])


def _group_rows(tab):
    return tab.reshape(NSA_KV_HEADS, NSA_GROUP * tab.shape[1], tab.shape[2])


def nsa_sample_local(q, means, rows, win_buf, new_win, gates, rel_bias, past):
    n, seq, _ = q.shape
    nbp = means.shape[1]
    wb = win_buf.shape[1]
    nb = -(-(past + seq) // NSA_BLOCK)
    assert nbp + 1 == nb and nb <= LANES and seq % 8 == 0
    lw = -(-(wb + seq) // LANES) * LANES
    qpos = past + jnp.arange(seq)[:, None]
    bend = jnp.arange(LANES)[None, :] * NSA_BLOCK + NSA_BLOCK - 1
    bc = _group_rows(_bias_table(rel_bias, qpos - bend, bend <= qpos))
    cidx = jnp.arange(lw)[None, :]
    wpos = past - wb + cidx
    dw = qpos - wpos
    bw = _group_rows(_bias_table(rel_bias, dw, (dw >= 0) & (dw <= NSA_WINDOW) & (wpos >= 0) & (cidx < wb + seq)))
    rows_q = NSA_GROUP * seq
    per_b = lambda s: pl.BlockSpec((1,) + s, lambda b: (b,) + (0,) * len(s))
    fix = lambda s: pl.BlockSpec(s, lambda b: (0,) * len(s))
    return pl.pallas_call(
        functools.partial(_nsa_sample_local_kernel, past=past, seq=seq, k_sel=min(NSA_TOPK, nb)),
        grid=(n,),
        in_specs=[per_b((seq, D_MODEL)), per_b((nbp, 2 * NSA_KVW)), per_b((seq, D_MODEL)), per_b((wb, 2 * NSA_KVW)),
                  per_b((seq, 2 * NSA_KVW)), per_b((seq, LANES)), fix((NSA_KV_HEADS, rows_q, LANES)),
                  fix((NSA_KV_HEADS, rows_q, lw))],
        out_specs=[per_b((seq, D_MODEL)), per_b((seq, D_MODEL)), per_b((NSA_KV_HEADS, seq, LANES)),
                   per_b((wb, 2 * NSA_KVW))],
        out_shape=[jax.ShapeDtypeStruct((n, seq, D_MODEL), F32), jax.ShapeDtypeStruct((n, seq, D_MODEL), F32),
                   jax.ShapeDtypeStruct((n, NSA_KV_HEADS, seq, LANES), F32),
                   jax.ShapeDtypeStruct((n, wb, 2 * NSA_KVW), F32)],
        scratch_shapes=[pltpu.VMEM((LANES, 2 * NSA_KVW), F32), pltpu.VMEM((lw, 2 * NSA_KVW), F32)],
        compiler_params=_cparams("parallel"),
        name="nsa_sample_local",
    )(q, means, rows, win_buf, new_win, gates, bc, bw)


def _nsa_sample_selected_kernel(pt_ref, q_ref, page_ref, rows_ref, selb_ref, bias_ref, g_ref, o_ref,
                                qs_scr, new_scr, m_scr, l_scr, acc_scr, *, seq, page, past):
    p = pl.program_id(1)
    n_pages = pl.num_programs(1) - 1

    @pl.when(p == 0)
    def _():
        m_scr[...] = jnp.full(m_scr.shape, M_FLOOR, F32)
        l_scr[...] = jnp.zeros(l_scr.shape, F32)
        acc_scr[...] = jnp.zeros(acc_scr.shape, F32)
        q = q_ref[0]
        for kh in range(NSA_KV_HEADS):
            qs_scr[kh] = _stack_group_queries(q, kh)
        new_scr[...] = jnp.zeros(new_scr.shape, F32)
        new_scr[0:seq, :] = rows_ref[0][:, 2 * NSA_KVW:4 * NSA_KVW]

    def step(kv, first_pos):
        blk = lax.broadcasted_iota(jnp.int32, (LANES, page), 0)
        key_blk = (first_pos + lax.broadcasted_iota(jnp.int32, (LANES, page), 1)) // NSA_BLOCK
        expand = jnp.where(blk == key_blk, 1.0, 0.0).astype(BF16)
        for kh in range(NSA_KV_HEADS):
            lo, hi = kh * NSA_DH, (kh + 1) * NSA_DH
            sb = _dot(selb_ref[0, kh].astype(BF16), expand)
            s = (_dot_nt(qs_scr[kh], kv[:, lo:hi].astype(BF16)) + bias_ref[0, kh]
                 + jnp.concatenate([sb] * NSA_GROUP, axis=0))
            _online_update(s, kv[:, NSA_KVW + lo:NSA_KVW + hi].astype(BF16), m_scr, l_scr, acc_scr, kh)

    @pl.when(p < n_pages)
    def _():
        step(page_ref[0], p * page)

    @pl.when(p == n_pages)
    def _():
        step(new_scr[...], past)
        gate = _sigmoid(g_ref[0])
        for kh in range(NSA_KV_HEADS):
            o = acc_scr[kh] / jnp.maximum(l_scr[kh], 1e-30)
            for g in range(NSA_GROUP):
                h = kh * NSA_GROUP + g
                o_ref[0, :, h * NSA_DH:(h + 1) * NSA_DH] = (
                    o[g * seq:(g + 1) * seq] * gate[:, NSA_HEADS + h:NSA_HEADS + h + 1])


def nsa_sample_selected(q, pool, page_table, rows, selb, gates, rel_bias, past):
    n, seq, _ = q.shape
    n_pages = page_table.shape[1]
    page = pool.shape[1]
    assert page == LANES and seq <= page
    qpos = past + jnp.arange(seq)[:, None]
    j = jnp.arange(page)[None, :]
    d_past = qpos[None] - (jnp.arange(n_pages)[:, None, None] * page + j[None])
    d_new = (qpos - past) - j
    dist = jnp.concatenate([d_past, d_new[None]], 0)
    valid = jnp.concatenate([d_past >= 0, ((d_new >= 0) & (j < seq))[None]], 0)
    bias = _bias_table(rel_bias, dist, valid)
    bias = jnp.moveaxis(bias, 1, 0).reshape(n_pages + 1, NSA_KV_HEADS, NSA_GROUP * seq, page)
    rows_q = NSA_GROUP * seq
    last = n_pages - 1
    return pl.pallas_call(
        functools.partial(_nsa_sample_selected_kernel, seq=seq, page=page, past=past),
        grid_spec=pltpu.PrefetchScalarGridSpec(
            num_scalar_prefetch=1, grid=(n, n_pages + 1),
            in_specs=[pl.BlockSpec((1, seq, D_MODEL), lambda b, p, pt: (b, 0, 0)),
                      pl.BlockSpec((1, page, 2 * NSA_KVW), lambda b, p, pt: (pt[b, jnp.minimum(p, last)], 0, 1)),
                      pl.BlockSpec((1, seq, D_MODEL), lambda b, p, pt: (b, 0, 0)),
                      pl.BlockSpec((1, NSA_KV_HEADS, seq, LANES), lambda b, p, pt: (b, 0, 0, 0)),
                      pl.BlockSpec((1, NSA_KV_HEADS, rows_q, page), lambda b, p, pt: (p, 0, 0, 0)),
                      pl.BlockSpec((1, seq, LANES), lambda b, p, pt: (b, 0, 0))],
            out_specs=pl.BlockSpec((1, seq, D_MODEL), lambda b, p, pt: (b, 0, 0)),
            scratch_shapes=[pltpu.VMEM((NSA_KV_HEADS, rows_q, NSA_DH), BF16), pltpu.VMEM((page, 2 * NSA_KVW), F32),
                            pltpu.VMEM((NSA_KV_HEADS, rows_q, 1), F32), pltpu.VMEM((NSA_KV_HEADS, rows_q, 1), F32),
                            pltpu.VMEM((NSA_KV_HEADS, rows_q, NSA_DH), F32)]),
        out_shape=jax.ShapeDtypeStruct((n, seq, D_MODEL), F32),
        compiler_params=_cparams("parallel", "arbitrary"),
        name="nsa_sample_selected",
    )(page_table, q, pool, rows, selb, bias, gates)


def _diff_sample_kernel(pt_ref, q_ref, kpage_ref, vpage_ref, new_ref, bias_ref, lam_ref, hg_ref, o_ref,
                        qbd_scr, newk_scr, newv_scr, m_scr, l_scr, acc_scr, *, seq, lam_init):
    p = pl.program_id(1)
    n_pages = pl.num_programs(1) - 1
    rows_q = 2 * DF_HEADS * seq

    @pl.when(p == 0)
    def _():
        m_scr[...] = jnp.full(m_scr.shape, M_FLOOR, F32)
        l_scr[...] = jnp.zeros(l_scr.shape, F32)
        acc_scr[...] = jnp.zeros(acc_scr.shape, F32)
        q = q_ref[0] * (DF_DK ** -0.5)
        rep = jnp.concatenate([q] * (2 * DF_HEADS), axis=0)
        r = lax.broadcasted_iota(jnp.int32, (rows_q, D_MODEL), 0) // seq
        c = lax.broadcasted_iota(jnp.int32, (rows_q, D_MODEL), 1) // DF_DK
        qbd_scr[...] = jnp.where(r == c, rep, 0.0).astype(BF16)
        newk_scr[...] = jnp.zeros(newk_scr.shape, F32)
        newv_scr[...] = jnp.zeros(newv_scr.shape, F32)
        newk_scr[0:seq, :] = new_ref[0][:, 0:D_MODEL]
        newv_scr[0:seq, :] = new_ref[0][:, D_MODEL:2 * D_MODEL]

    def step(k, v):
        s = _dot_nt(qbd_scr[...], k.astype(BF16)) + bias_ref[0]
        _online_update(s, v.astype(BF16), m_scr, l_scr, acc_scr, slice(None))

    @pl.when(p < n_pages)
    def _():
        step(kpage_ref[0], vpage_ref[0])

    @pl.when(p == n_pages)
    def _():
        step(newk_scr[...], newv_scr[...])
        lam = _diff_lambda(lam_ref, lam_init)
        acc = acc_scr[...] / jnp.maximum(l_scr[...], 1e-30)
        for h in range(DF_HEADS):
            r0 = 2 * h * seq
            lo, hi = h * DF_DV, (h + 1) * DF_DV
            o = acc[r0:r0 + seq, lo:hi] - lam * acc[r0 + seq:r0 + 2 * seq, lo:hi]
            o = o * lax.rsqrt(jnp.mean(o * o, -1, keepdims=True) + EPS) * hg_ref[:, lo:hi]
            o_ref[0, :, lo:hi] = o * (1.0 - lam_init)


def diff_sample(q, pool, page_table, kv_new, lam_p, head_g, rel_bias, past, lam_init):
    n, seq, _ = q.shape
    n_pages = page_table.shape[1]
    page = pool.shape[1]
    assert page == LANES and seq <= page and seq % 8 == 0
    rows_q = 2 * DF_HEADS * seq
    qpos = past + jnp.arange(seq)[:, None]
    j = jnp.arange(page)[None, :]
    d_past = qpos[None] - (jnp.arange(n_pages)[:, None, None] * page + j[None])
    d_new = (qpos - past) - j
    dist = jnp.concatenate([d_past, d_new[None]], 0)
    valid = jnp.concatenate([d_past >= 0, ((d_new >= 0) & (j < seq))[None]], 0)
    bias = _bias_table(rel_bias, dist, valid)
    bias = jnp.moveaxis(bias, 1, 0).reshape(n_pages + 1, rows_q, page)
    last = n_pages - 1
    return pl.pallas_call(
        functools.partial(_diff_sample_kernel, seq=seq, lam_init=lam_init),
        grid_spec=pltpu.PrefetchScalarGridSpec(
            num_scalar_prefetch=1, grid=(n, n_pages + 1),
            in_specs=[pl.BlockSpec((1, seq, D_MODEL), lambda b, p, pt: (b, 0, 0)),
                      pl.BlockSpec((1, page, D_MODEL), lambda b, p, pt: (pt[b, jnp.minimum(p, last)], 0, 0)),
                      pl.BlockSpec((1, page, D_MODEL), lambda b, p, pt: (pt[b, jnp.minimum(p, last)], 0, 1)),
                      pl.BlockSpec((1, seq, 2 * D_MODEL), lambda b, p, pt: (b, 0, 0)),
                      pl.BlockSpec((1, rows_q, page), lambda b, p, pt: (p, 0, 0)),
                      pl.BlockSpec((4, DF_DK), lambda b, p, pt: (0, 0)),
                      pl.BlockSpec((1, D_MODEL), lambda b, p, pt: (0, 0))],
            out_specs=pl.BlockSpec((1, seq, D_MODEL), lambda b, p, pt: (b, 0, 0)),
            scratch_shapes=[pltpu.VMEM((rows_q, D_MODEL), BF16), pltpu.VMEM((page, D_MODEL), F32),
                            pltpu.VMEM((page, D_MODEL), F32), pltpu.VMEM((rows_q, 1), F32),
                            pltpu.VMEM((rows_q, 1), F32), pltpu.VMEM((rows_q, D_MODEL), F32)]),
        out_shape=jax.ShapeDtypeStruct((n, seq, D_MODEL), F32),
        compiler_params=_cparams("parallel", "arbitrary"),
        name="diff_sample",
    )(page_table, q, pool, pool, kv_new, bias, lam_p, head_g.reshape(1, D_MODEL))


def _trunk(x, p, past, page_table, nsa_pool, nsa_win, ml_c, ml_n, ml_m, conv_buf, diff_pool, weights):
    (norm_g, final_norm_g, rel_bias, ple_w, ple_gate_w, ple_norm_g,
     nsa_w_in, nsa_w_out, mlstm_w_in, mlstm_b_if, mlstm_head_g, mlstm_w_out,
     conv_w_in, conv_dw_w, conv_dw_b, conv_ln_g, conv_ln_b, conv_w_out,
     diff_w_in, diff_lam, diff_head_g, diff_w_out) = weights
    n, seq, d = x.shape
    m = n * seq
    sample = page_table is not None
    x2 = x.reshape(m, d)
    depth = norm_g.shape[0]
    out = {}
    for i in range(depth):
        kind, j = i % 4, i // 4
        assert j == 0
        p_i = p[i].reshape(m, D_PLE)
        tail = dict(x=x2, p=p_i, ple_norm_g=ple_norm_g[i], ple_gate_w=ple_gate_w[i], ple_w=ple_w[i],
                    final_g=final_norm_g if i == depth - 1 else None)
        if kind == 0:
            w = nsa_w_in[j]
            c0, c1, c2, c3 = D_MODEL, 2 * D_MODEL, 2 * D_MODEL + 2 * NSA_KVW, 2 * D_MODEL + 2 * NSA_KVW + 3 * NSA_HEADS
            q, rows, win, gates, z = norm_linear(x2, norm_g[i], [w[:, :c0], w[:, c0:c1], w[:, c1:c2], w[:, c2:c3], w[:, c3:]])
            q3, rows3, win3, gates3 = (t.reshape(n, seq, -1) for t in (q, rows, win, gates))
            if sample:
                pool = nsa_pool[j].reshape(nsa_pool.shape[1], nsa_pool.shape[2], D_MODEL)
                wb = nsa_win.shape[2]
                means = nsa_page_means(pool, page_table)
                oc, ow, selb, win_out = nsa_sample_local(q3, means, rows3, nsa_win[j].reshape(n, wb, 2 * NSA_KVW),
                                                         win3, gates3, rel_bias, past)
                os_ = nsa_sample_selected(q3, pool, page_table, rows3, selb, gates3, rel_bias, past)
            else:
                oc, selb = nsa_cmp_prompt(q3, rows3, gates3, rel_bias)
                os_ = nsa_flash_prompt(q3, rows3, 4, 6, gates3, rel_bias, selb, window=False)
                ow = nsa_flash_prompt(q3, win3, 0, 2, gates3, rel_bias, None, window=True)
                keep = min(NSA_WINDOW, seq)
                win_out = win3[:, seq - keep:]
                if keep < NSA_WINDOW:
                    win_out = jnp.concatenate([jnp.zeros((n, NSA_WINDOW - keep, 2 * NSA_KVW), F32), win_out], 1)
            out['kv'] = rows3.reshape(1, n, seq, 4, NSA_KV_HEADS, NSA_DH)
            out['win'] = win_out.reshape(1, n, win_out.shape[1], 2, NSA_KV_HEADS, NSA_DH)
            x2 = post_layer([t.reshape(m, d) for t in (oc, os_, ow)], z, w_out=nsa_w_out[j], **tail)
        elif kind == 1:
            w = mlstm_w_in[j]
            c0, c1, c2 = ML_QK, 2 * ML_QK, 2 * ML_QK + D_MODEL
            c3, c4 = c2 + 2 * ML_HEADS, c2 + 2 * ML_HEADS + D_MODEL
            q, k, v, gates, og, z = norm_linear(
                x2, norm_g[i], [w[:, :c0], w[:, c0:c1], w[:, c1:c2], w[:, c2:c3], w[:, c3:c4], w[:, c4:]])
            state = (ml_c[j], ml_n[j], ml_m[j]) if sample else None
            a, c_new, n_new, m_new = mlstm_cell(*(t.reshape(n, seq, -1) for t in (q, k, v, gates, og)),
                                                mlstm_b_if[j], mlstm_head_g[j], state)
            out['C'], out['n'], out['m'] = c_new[None], n_new[None], m_new[None]
            x2 = post_layer([a.reshape(m, d)], z, w_out=mlstm_w_out[j], **tail)
        elif kind == 2:
            w = conv_w_in[j]
            a, b, z = norm_linear(x2, norm_g[i], [w[:, :d], w[:, d:2 * d], w[:, 2 * d:]])
            cact, nbuf = conv_block(a.reshape(n, seq, d), b.reshape(n, seq, d), conv_buf[j] if sample else None,
                                    conv_dw_w[j], conv_dw_b[j], conv_ln_g[j], conv_ln_b[j])
            out['conv'] = nbuf[None]
            x2 = post_layer([cact.reshape(m, d)], z, w_out=conv_w_out[j], **tail)
        else:
            w = diff_w_in[j]
            q, kv, z = norm_linear(x2, norm_g[i], [w[:, :d], w[:, d:3 * d], w[:, 3 * d:]])
            lam_init = 0.8 - 0.6 * math.exp(-0.3 * i)
            q3, kv3 = q.reshape(n, seq, d), kv.reshape(n, seq, 2 * d)
            if sample:
                pool = diff_pool[j].reshape(diff_pool.shape[1], diff_pool.shape[2], 2 * d)
                a = diff_sample(q3, pool, page_table, kv3, diff_lam[j], diff_head_g[j], rel_bias, past, lam_init)
            else:
                a = diff_flash_prompt(q3, kv3, diff_lam[j], diff_head_g[j], rel_bias, lam_init)
            out['diff'] = kv3.reshape(1, n, seq, 2, DF_HEADS, DF_DV)
            x2 = post_layer([a.reshape(m, d)], z, w_out=diff_w_out[j], **tail)
    return x2.reshape(n, seq, d), out


def kernel(x_prompt, x_sample, cache_nsa_kv, state_nsa_win, state_mlstm_C, state_mlstm_n, state_mlstm_m,
           state_conv, cache_diff_kv, page_table, p_prompt, p_sample, norm_g, final_norm_g, rel_bias,
           ple_w, ple_gate_w, ple_norm_g, nsa_w_in, nsa_w_out, mlstm_w_in, mlstm_b_if, mlstm_head_g,
           mlstm_w_out, conv_w_in, conv_dw_w, conv_dw_b, conv_ln_g, conv_ln_b, conv_w_out, diff_w_in,
           diff_lam, diff_head_g, diff_w_out):
    weights = (norm_g, final_norm_g, rel_bias, ple_w, ple_gate_w, ple_norm_g,
               nsa_w_in, nsa_w_out, mlstm_w_in, mlstm_b_if, mlstm_head_g, mlstm_w_out,
               conv_w_in, conv_dw_w, conv_dw_b, conv_ln_g, conv_ln_b, conv_w_out,
               diff_w_in, diff_lam, diff_head_g, diff_w_out)
    past = page_table.shape[1] * cache_nsa_kv.shape[2]
    y_p, o_p = _trunk(x_prompt, p_prompt, 0, None, None, None, None, None, None, None, None, weights)
    y_s, o_s = _trunk(x_sample, p_sample, past, page_table, cache_nsa_kv, state_nsa_win, state_mlstm_C,
                      state_mlstm_n, state_mlstm_m, state_conv, cache_diff_kv, weights)
    return (y_p, y_s, o_p['kv'], o_s['kv'], o_p['win'], o_s['win'], o_p['C'], o_s['C'], o_p['n'], o_s['n'],
            o_p['m'], o_s['m'], o_p['conv'], o_s['conv'], o_p['diff'], o_s['diff'])
```

```python
import functools
import math

import jax
import jax.numpy as jnp
import numpy as np
from jax import lax
from jax.experimental import pallas as pl
from jax.experimental.pallas import tpu as pltpu

F32 = jnp.float32
BF16 = jnp.bfloat16

D_MODEL = 1024
D_PLE = 256
N_BUCKETS = 32
MAX_DISTANCE = 128
EPS = 1e-6

NSA_HEADS = 16
NSA_KV_HEADS = 4
NSA_GROUP = 4
NSA_DH = 64
NSA_BLOCK = 32
NSA_TOPK = 16
NSA_WINDOW = 512
NSA_KVW = NSA_KV_HEADS * NSA_DH

ML_HEADS = 4
ML_DV = 256
ML_DK = 128
ML_QK = ML_HEADS * ML_DK

CONV_WIDTH = 31
CONV_HALO = 32

DF_HEADS = 8
DF_DK = 64
DF_DV = 128
DF_PAGES_PER_STEP = 8

LANES = 128
MASKED = -2e30
M_FLOOR = -1e30
FAR_DISTANCE = 113
VMEM_LIMIT = 48 * 1024 * 1024
ROW_TILE = 256
ATT_TILE = 256
COL_CHUNK = 512


def _cparams(*sem):
    return pltpu.CompilerParams(dimension_semantics=sem, vmem_limit_bytes=VMEM_LIMIT)


def _dot(a, b):
    return jnp.dot(a, b, preferred_element_type=F32)


def _dot_nt(a, b):
    return lax.dot_general(a, b, (((1,), (1,)), ((), ())), preferred_element_type=F32)


def _dot_tn(a, b):
    return lax.dot_general(a, b, (((0,), (0,)), ((), ())), preferred_element_type=F32)


def _dot_exact(a, b):
    return jnp.dot(a, b, preferred_element_type=F32, precision=lax.Precision.HIGHEST)


def _sigmoid(x):
    return 1.0 / (1.0 + jnp.exp(-x))


def _silu(x):
    return x * _sigmoid(x)


def _log_sigmoid(x):
    return jnp.minimum(x, 0.0) - jnp.log1p(jnp.exp(-jnp.abs(x)))


def _pad_cols(w, mult=LANES):
    pad = (-w.shape[-1]) % mult
    return w if pad == 0 else jnp.pad(w, [(0, 0)] * (w.ndim - 1) + [(0, pad)])


def _norm_linear_kernel(x_ref, g_ref, *refs, widths):
    w_refs, o_refs = refs[:len(widths)], refs[len(widths):]
    x = x_ref[...]
    xn = (x * lax.rsqrt(jnp.mean(x * x, -1, keepdims=True) + EPS) * g_ref[...]).astype(BF16)
    for w_ref, o_ref, width in zip(w_refs, o_refs, widths):
        for c0 in range(0, width, COL_CHUNK):
            c1 = min(c0 + COL_CHUNK, width)
            o_ref[:, c0:c1] = _dot(xn, w_ref[:, c0:c1])


def norm_linear(x2d, g, weights):
    m, d = x2d.shape
    ws = [_pad_cols(w).astype(BF16) for w in weights]
    widths = tuple(w.shape[1] for w in ws)
    tm = min(ROW_TILE, m)
    outs = pl.pallas_call(
        functools.partial(_norm_linear_kernel, widths=widths),
        grid=(m // tm,),
        in_specs=[pl.BlockSpec((tm, d), lambda i: (i, 0)), pl.BlockSpec((1, d), lambda i: (0, 0))]
        + [pl.BlockSpec((d, wd), lambda i: (0, 0)) for wd in widths],
        out_specs=[pl.BlockSpec((tm, wd), lambda i: (i, 0)) for wd in widths],
        out_shape=[jax.ShapeDtypeStruct((m, wd), F32) for wd in widths],
        compiler_params=_cparams("parallel"),
        name="norm_linear",
    )(x2d, g.reshape(1, d), *ws)
    return outs


def _post_kernel(*refs, n_a, final):
    a_refs = refs[:n_a]
    z_ref, x_ref, p_ref, wo_ref, png_ref, wg_ref, wp_ref = refs[n_a:n_a + 7]
    rest = refs[n_a + 7:]
    fng_ref = rest[0] if final else None
    o_ref = rest[-1]
    a = a_refs[0][...]
    for r in a_refs[1:]:
        a = a + r[...]
    h = (a * _silu(z_ref[...])).astype(BF16)
    x1 = x_ref[...] + _dot(h, wo_ref[...])
    r = (x1 * lax.rsqrt(jnp.mean(x1 * x1, -1, keepdims=True) + EPS) * png_ref[...]).astype(BF16)
    gate = _sigmoid(_dot(r, wg_ref[...]))
    x2 = x1 + gate * _dot(p_ref[...].astype(BF16), wp_ref[...])
    if final:
        x2 = x2 * lax.rsqrt(jnp.mean(x2 * x2, -1, keepdims=True) + EPS) * fng_ref[...]
    o_ref[...] = x2


def post_layer(a_list, z, x, p, w_out, ple_norm_g, ple_gate_w, ple_w, final_g=None):
    m, d = x.shape
    tm = min(ROW_TILE, m)
    final = final_g is not None
    row = lambda i: (i, 0)
    fix = lambda i: (0, 0)
    in_specs = [pl.BlockSpec((tm, d), row) for _ in a_list] + [
        pl.BlockSpec((tm, d), row), pl.BlockSpec((tm, d), row), pl.BlockSpec((tm, D_PLE), row),
        pl.BlockSpec((d, d), fix), pl.BlockSpec((1, d), fix), pl.BlockSpec((d, d), fix),
        pl.BlockSpec((D_PLE, d), fix)]
    args = list(a_list) + [z, x, p, w_out.astype(BF16), ple_norm_g.reshape(1, d), ple_gate_w.astype(BF16),
                           ple_w.astype(BF16)]
    if final:
        in_specs.append(pl.BlockSpec((1, d), fix))
        args.append(final_g.reshape(1, d))
    return pl.pallas_call(
        functools.partial(_post_kernel, n_a=len(a_list), final=final),
        grid=(m // tm,),
        in_specs=in_specs,
        out_specs=pl.BlockSpec((tm, d), row),
        out_shape=jax.ShapeDtypeStruct((m, d), F32),
        compiler_params=_cparams("parallel"),
        name="post_layer",
    )(*args)


def _t5_bucket(dist):
    d = jnp.maximum(dist, 0)
    exact = N_BUCKETS // 2
    df = jnp.maximum(d, 1).astype(F32)
    large = exact + (jnp.log(df / exact) / math.log(MAX_DISTANCE / exact) * (N_BUCKETS - exact)).astype(jnp.int32)
    return jnp.where(d < exact, d, jnp.minimum(large, N_BUCKETS - 1))


def _bias_table(rel_bias, dist, valid):
    bucket = _t5_bucket(dist)[None]
    col = lambda k: rel_bias[k].reshape((-1,) + (1,) * dist.ndim)
    b = jnp.broadcast_to(col(N_BUCKETS - 1), (rel_bias.shape[1],) + dist.shape)
    for k in range(N_BUCKETS - 1):
        b = jnp.where(bucket == k, col(k), b)
    return jnp.where(valid[None], b, MASKED).astype(F32)


def _toeplitz_tiles(rel_bias, t, n_delta, max_dist=None):
    i = jnp.arange(t)
    d = jnp.arange(n_delta)[:, None, None] * t + i[None, :, None] - i[None, None, :]
    valid = d >= 0
    if max_dist is not None:
        valid = valid & (d <= max_dist)
    return _bias_table(rel_bias, d, valid)


def _far_distance_ok():
    d = np.arange(FAR_DISTANCE, FAR_DISTANCE + 8192, dtype=np.float64)
    large = 16 + np.floor(np.log(d / 16) / math.log(MAX_DISTANCE / 16) * 16 - 1e-6)
    return bool(np.all(large >= N_BUCKETS - 1))


def _far_rows(rel_bias):
    return jnp.broadcast_to(rel_bias[N_BUCKETS - 1][:, None], (rel_bias.shape[1], LANES)).astype(F32)


def _flash_update(pieces, v_bf, m_ref, acc_ref, idx, l_ref=None):
    mx = pieces[0]
    for pc in pieces[1:]:
        mx = jnp.maximum(mx, pc)
    m_old = m_ref[idx]
    m_new = jnp.maximum(m_old, jnp.max(mx, -1, keepdims=True))
    alpha = jnp.exp(m_old - m_new)
    p = [jnp.exp(pc - m_new) for pc in pieces]
    p = (p[0] if len(p) == 1 else jnp.concatenate(p, axis=1)).astype(BF16)
    acc_ref[idx] = alpha * acc_ref[idx] + _dot(p, v_bf)
    if l_ref is not None:
        l_ref[idx] = alpha * l_ref[idx] + _dot(p, jnp.ones((p.shape[1], LANES), BF16))
    m_ref[idx] = m_new


def _softmax_rows(s):
    m = jnp.maximum(jnp.max(s, -1, keepdims=True), M_FLOOR)
    e = jnp.exp(s - m)
    return e / jnp.maximum(jnp.sum(e, -1, keepdims=True), 1e-30)


def _select_blocks(imp, cur, k):
    lane = lax.broadcasted_iota(jnp.int32, imp.shape, 1).astype(F32)
    sel = jnp.where(lane == cur, float(NSA_GROUP + 1), jnp.where(lane < cur, imp, -1.0))
    chosen = jnp.zeros(imp.shape, F32)
    for _ in range(k):
        top = jnp.max(sel, -1, keepdims=True)
        first = jnp.min(jnp.where(sel == top, lane, float(LANES)), -1, keepdims=True)
        hit = lane == first
        chosen = jnp.where(hit, 1.0, chosen)
        sel = jnp.where(hit, -3.0, sel)
    return jnp.where(chosen > 0.0, 0.0, MASKED)


def _block_expand(first_pos, width):
    blk = lax.broadcasted_iota(jnp.int32, (LANES, width), 0)
    key_blk = (first_pos + lax.broadcasted_iota(jnp.int32, (LANES, width), 1)) // NSA_BLOCK
    return jnp.where(blk == key_blk, 1.0, 0.0).astype(BF16)


def _nsa_cmp_prompt_kernel(q_ref, rows_ref, g_ref, bias_ref, oc_ref, selb_ref, mk_scr, *, seq, tq, k_sel):
    i = pl.program_id(1)
    nb = seq // NSA_BLOCK

    @pl.when(i == 0)
    def _():
        means = jnp.sum(rows_ref[0].reshape(nb, NSA_BLOCK, 2 * NSA_KVW), axis=1) * (1.0 / NSA_BLOCK)
        mk_scr[...] = jnp.zeros(mk_scr.shape, F32)
        mk_scr[0:nb, :] = means

    q = q_ref[0]
    gate = _sigmoid(g_ref[0])
    pos = i * tq + lax.broadcasted_iota(jnp.int32, (tq, 1), 0)
    cur = (pos // NSA_BLOCK).astype(F32)
    for kh in range(NSA_KV_HEADS):
        mk = mk_scr[:, kh * NSA_DH:(kh + 1) * NSA_DH].astype(BF16)
        mv = mk_scr[:, NSA_KVW + kh * NSA_DH:NSA_KVW + (kh + 1) * NSA_DH].astype(BF16)
        imp = jnp.zeros((tq, LANES), F32)
        for g in range(NSA_GROUP):
            h = kh * NSA_GROUP + g
            qh = (q[:, h * NSA_DH:(h + 1) * NSA_DH] * (NSA_DH ** -0.5)).astype(BF16)
            p = _softmax_rows(_dot_nt(qh, mk) + bias_ref[h])
            imp = imp + p
            oc_ref[0, :, h * NSA_DH:(h + 1) * NSA_DH] = _dot(p.astype(BF16), mv) * gate[:, h:h + 1]
        selb_ref[0, kh] = _select_blocks(imp, cur, k_sel)


def nsa_cmp_prompt(q, rows, gates, rel_bias):
    n, seq, _ = q.shape
    tq = min(ATT_TILE, seq)
    nb = seq // NSA_BLOCK
    assert nb <= LANES
    t = jnp.arange(seq)[:, None]
    bend = jnp.arange(LANES)[None, :] * NSA_BLOCK + NSA_BLOCK - 1
    bias = _bias_table(rel_bias, t - bend, bend <= t)
    return pl.pallas_call(
        functools.partial(_nsa_cmp_prompt_kernel, seq=seq, tq=tq, k_sel=min(NSA_TOPK, nb)),
        grid=(n, seq // tq),
        in_specs=[pl.BlockSpec((1, tq, D_MODEL), lambda b, i: (b, i, 0)),
                  pl.BlockSpec((1, seq, 2 * NSA_KVW), lambda b, i: (b, 0, 0)),
                  pl.BlockSpec((1, tq, LANES), lambda b, i: (b, i, 0)),
                  pl.BlockSpec((NSA_HEADS, tq, LANES), lambda b, i: (0, i, 0))],
        out_specs=[pl.BlockSpec((1, tq, D_MODEL), lambda b, i: (b, i, 0)),
                   pl.BlockSpec((1, NSA_KV_HEADS, tq, LANES), lambda b, i: (b, 0, i, 0))],
        out_shape=[jax.ShapeDtypeStruct((n, seq, D_MODEL), F32),
                   jax.ShapeDtypeStruct((n, NSA_KV_HEADS, seq, LANES), F32)],
        scratch_shapes=[pltpu.VMEM((LANES, 2 * NSA_KVW), F32)],
        compiler_params=_cparams("parallel", "arbitrary"),
        name="nsa_cmp_prompt",
    )(q, rows, gates, bias)


def _nsa_flash_prompt_kernel(*refs, t, window, gate_col):
    if window:
        q_ref, k_ref, v_ref, bias_ref, g_ref, o_ref, qs_scr, m_scr, acc_scr = refs
        far_ref = selb_ref = None
    else:
        q_ref, k_ref, v_ref, bias_ref, far_ref, g_ref, selb_ref, o_ref, qs_scr, m_scr, acc_scr = refs
    u = pl.program_id(1)
    qi = pl.program_id(2)
    heads = 2 * NSA_GROUP
    pieces = t // LANES

    m_scr[...] = jnp.full(m_scr.shape, M_FLOOR, F32)
    acc_scr[...] = jnp.zeros(acc_scr.shape, F32)
    q = q_ref[0]
    for h in range(heads):
        qs_scr[h] = (q[:, h * NSA_DH:(h + 1) * NSA_DH] * (NSA_DH ** -0.5)).astype(BF16)
    lane = lax.broadcasted_iota(jnp.int32, (t, LANES), 1)

    def tile(start, delta):
        k = k_ref[0, pl.ds(start, t), :]
        v = v_ref[0, pl.ds(start, t), :]
        if not window:
            expand = _block_expand(start, t)
        for kv in range(2):
            k_bf = k[:, kv * NSA_DH:(kv + 1) * NSA_DH].astype(BF16)
            v_aug = jnp.where((lane < NSA_DH) if kv == 0 else (lane >= NSA_DH), v, 1.0).astype(BF16)
            if not window:
                sel_bias = _dot(selb_ref[0, kv].astype(BF16), expand)
            for g in range(NSA_GROUP):
                h = kv * NSA_GROUP + g
                s = _dot_nt(qs_scr[h], k_bf)
                ps = []
                for i in range(pieces):
                    pc = s[:, i * LANES:(i + 1) * LANES]
                    if delta is None:
                        pc = pc + far_ref[h:h + 1, :]
                    else:
                        pc = pc + bias_ref[h, delta, :, i * LANES:(i + 1) * LANES]
                    if not window:
                        pc = pc + sel_bias[:, i * LANES:(i + 1) * LANES]
                    ps.append(pc)
                _flash_update(ps, v_aug, m_scr, acc_scr, h)

    if window:
        for delta in range(NSA_WINDOW // t, 0, -1):
            @pl.when(qi >= delta)
            def _(delta=delta):
                tile(pl.multiple_of((qi - delta) * t, t), delta)
    else:
        def far_body(ki, carry):
            tile(pl.multiple_of(ki * t, t), None)
            return carry
        lax.fori_loop(0, jnp.maximum(qi - 1, 0), far_body, 0)

        @pl.when(qi >= 1)
        def _():
            tile(pl.multiple_of((qi - 1) * t, t), 1)
    tile(pl.multiple_of(qi * t, t), 0)

    gate = _sigmoid(g_ref[0])
    for h in range(heads):
        col = gate_col + h
        g0 = jnp.where(u == 0, gate[:, col:col + 1], gate[:, col + heads:col + heads + 1])
        acc = acc_scr[h]
        lo, hi = acc[:, 0:NSA_DH], acc[:, NSA_DH:2 * NSA_DH]
        o, l = (lo, hi) if h < NSA_GROUP else (hi, lo)
        o_ref[0, :, h * NSA_DH:(h + 1) * NSA_DH] = o / jnp.maximum(l, 1e-30) * g0


def nsa_flash_prompt(q, kv_arr, k_blk, v_blk, gates, rel_bias, selb, window):
    n, seq, _ = q.shape
    t = min(ATT_TILE, seq)
    nq = seq // t
    heads = 2 * NSA_GROUP
    assert t % LANES == 0 and _far_distance_ok()
    if window:
        assert NSA_WINDOW % t == 0
        n_delta = NSA_WINDOW // t + 1
        tiles = _toeplitz_tiles(rel_bias, t, n_delta, NSA_WINDOW)
    else:
        assert t + 1 >= FAR_DISTANCE
        n_delta = 2
        tiles = _toeplitz_tiles(rel_bias, t, n_delta)
    in_specs = [pl.BlockSpec((1, t, heads * NSA_DH), lambda b, u, qi: (b, qi, u)),
                pl.BlockSpec((1, seq, LANES), lambda b, u, qi: (b, 0, k_blk + u)),
                pl.BlockSpec((1, seq, LANES), lambda b, u, qi: (b, 0, v_blk + u)),
                pl.BlockSpec((heads, n_delta, t, t), lambda b, u, qi: (u, 0, 0, 0))]
    args = [q, kv_arr, kv_arr, tiles]
    if not window:
        in_specs.append(pl.BlockSpec((heads, LANES), lambda b, u, qi: (u, 0)))
        args.append(_far_rows(rel_bias))
    in_specs.append(pl.BlockSpec((1, t, LANES), lambda b, u, qi: (b, qi, 0)))
    args.append(gates)
    if not window:
        in_specs.append(pl.BlockSpec((1, 2, t, LANES), lambda b, u, qi: (b, u, qi, 0)))
        args.append(selb)
    return pl.pallas_call(
        functools.partial(_nsa_flash_prompt_kernel, t=t, window=window, gate_col=2 * NSA_HEADS if window else NSA_HEADS),
        grid=(n, 2, nq),
        in_specs=in_specs,
        out_specs=pl.BlockSpec((1, t, heads * NSA_DH), lambda b, u, qi: (b, qi, u)),
        out_shape=jax.ShapeDtypeStruct((n, seq, D_MODEL), F32),
        scratch_shapes=[pltpu.VMEM((heads, t, NSA_DH), BF16), pltpu.VMEM((heads, t, LANES), F32),
                        pltpu.VMEM((heads, t, LANES), F32)],
        compiler_params=_cparams("parallel", "parallel", "arbitrary"),
        name="nsa_window_prompt" if window else "nsa_selected_prompt",
    )(*args)


def _diff_lambda(lam_ref, lam_init):
    lp = lam_ref[...]
    s1 = jnp.sum(lp[0:1] * lp[1:2], -1, keepdims=True)
    s2 = jnp.sum(lp[2:3] * lp[3:4], -1, keepdims=True)
    return jnp.exp(s1) - jnp.exp(s2) + lam_init


def _diff_flash_prompt_kernel(q_ref, k_ref, v_ref, bias_ref, far_ref, lam_ref, hg_ref, o_ref,
                              qs_scr, m_scr, l_scr, acc_scr, *, t, lam_init):
    qi = pl.program_id(2)
    pieces = t // LANES
    m_scr[...] = jnp.full(m_scr.shape, M_FLOOR, F32)
    l_scr[...] = jnp.zeros(l_scr.shape, F32)
    acc_scr[...] = jnp.zeros(acc_scr.shape, F32)
    q = q_ref[0]
    for c in range(2):
        qs_scr[c] = (q[:, c * DF_DK:(c + 1) * DF_DK] * (DF_DK ** -0.5)).astype(BF16)

    def tile(start, delta):
        k = k_ref[0, pl.ds(start, t), :]
        v_bf = v_ref[0, pl.ds(start, t), :].astype(BF16)
        for c in range(2):
            s = _dot_nt(qs_scr[c], k[:, c * DF_DK:(c + 1) * DF_DK].astype(BF16))
            ps = []
            for i in range(pieces):
                pc = s[:, i * LANES:(i + 1) * LANES]
                if delta is None:
                    pc = pc + far_ref[0, c:c + 1, :]
                else:
                    pc = pc + bias_ref[c, delta, :, i * LANES:(i + 1) * LANES]
                ps.append(pc)
            _flash_update(ps, v_bf, m_scr, acc_scr, c, l_ref=l_scr)

    def far_body(ki, carry):
        tile(pl.multiple_of(ki * t, t), None)
        return carry
    lax.fori_loop(0, jnp.maximum(qi - 1, 0), far_body, 0)

    @pl.when(qi >= 1)
    def _():
        tile(pl.multiple_of((qi - 1) * t, t), 1)
    tile(pl.multiple_of(qi * t, t), 0)

    lam = _diff_lambda(lam_ref, lam_init)
    o = acc_scr[0] / jnp.maximum(l_scr[0], 1e-30) - lam * (acc_scr[1] / jnp.maximum(l_scr[1], 1e-30))
    o = o * lax.rsqrt(jnp.mean(o * o, -1, keepdims=True) + EPS) * hg_ref[0]
    o_ref[0] = o * (1.0 - lam_init)


def diff_flash_prompt(q, kv, lam_p, head_g, rel_bias, lam_init):
    n, seq, _ = q.shape
    t = min(ATT_TILE, seq)
    nq = seq // t
    assert t % LANES == 0 and _far_distance_ok() and t + 1 >= FAR_DISTANCE
    tiles = _toeplitz_tiles(rel_bias, t, 2)
    far = _far_rows(rel_bias).reshape(DF_HEADS, 2, LANES)
    return pl.pallas_call(
        functools.partial(_diff_flash_prompt_kernel, t=t, lam_init=lam_init),
        grid=(n, DF_HEADS, nq),
        in_specs=[pl.BlockSpec((1, t, DF_DV), lambda b, h, qi: (b, qi, h)),
                  pl.BlockSpec((1, seq, DF_DV), lambda b, h, qi: (b, 0, h)),
                  pl.BlockSpec((1, seq, DF_DV), lambda b, h, qi: (b, 0, DF_HEADS + h)),
                  pl.BlockSpec((2, 2, t, t), lambda b, h, qi: (h, 0, 0, 0)),
                  pl.BlockSpec((1, 2, LANES), lambda b, h, qi: (h, 0, 0)),
                  pl.BlockSpec((4, DF_DK), lambda b, h, qi: (0, 0)),
                  pl.BlockSpec((1, 1, DF_DV), lambda b, h, qi: (h, 0, 0))],
        out_specs=pl.BlockSpec((1, t, DF_DV), lambda b, h, qi: (b, qi, h)),
        out_shape=jax.ShapeDtypeStruct((n, seq, D_MODEL), F32),
        scratch_shapes=[pltpu.VMEM((2, t, DF_DK), BF16), pltpu.VMEM((2, t, LANES), F32),
                        pltpu.VMEM((2, t, LANES), F32), pltpu.VMEM((2, t, DF_DV), F32)],
        compiler_params=_cparams("parallel", "parallel", "arbitrary"),
        name="diff_flash_prompt",
    )(q, kv, kv, tiles, far, lam_p, head_g.reshape(DF_HEADS, 1, DF_DV))


def _mlstm_kernel(*refs, c, has_init):
    if has_init:
        (q_ref, k_ref, v_ref, gc_ref, gr_ref, og_ref, bc_ref, br_ref, hg_ref, c0_ref, n0_ref, m0_ref,
         a_ref, cout_ref, nout_ref, mout_ref, c_scr, n_scr, m_scr) = refs
    else:
        (q_ref, k_ref, v_ref, gc_ref, gr_ref, og_ref, bc_ref, br_ref, hg_ref,
         a_ref, cout_ref, nout_ref, mout_ref, c_scr, n_scr, m_scr) = refs
    j = pl.program_id(1)

    @pl.when(j == 0)
    def _():
        if has_init:
            c_scr[...] = c0_ref[0]
            n_scr[...] = n0_ref[0]
            m_scr[...] = m0_ref[0]
        else:
            c_scr[...] = jnp.zeros(c_scr.shape, F32)
            n_scr[...] = jnp.zeros(n_scr.shape, F32)
            m_scr[...] = jnp.zeros(m_scr.shape, F32)

    row = lax.broadcasted_iota(jnp.int32, (c, c), 0)
    col = lax.broadcasted_iota(jnp.int32, (c, c), 1)
    tri = row >= col
    gc = gc_ref[0] + bc_ref[...]
    gr = gr_ref[0] + br_ref[...]
    f_cols = _dot_exact(jnp.where(tri, 1.0, 0.0), _log_sigmoid(gc))
    f_rows = _dot_exact(_log_sigmoid(gr), jnp.where(row <= col, 1.0, 0.0))
    og = og_ref[0]
    for h in range(ML_HEADS):
        qc = q_ref[0, :, h * ML_DK:(h + 1) * ML_DK]
        kc = k_ref[0, :, h * ML_DK:(h + 1) * ML_DK] * (ML_DK ** -0.5)
        vc = v_ref[0, :, h * ML_DV:(h + 1) * ML_DV]
        q_bf, k_bf = qc.astype(BF16), kc.astype(BF16)
        f_col = f_cols[:, ML_HEADS + h:ML_HEADS + h + 1]
        f_row = f_rows[ML_HEADS + h:ML_HEADS + h + 1, :]
        i_col = gc[:, h:h + 1]
        i_row = gr[h:h + 1, :]
        m_prev = m_scr[h:h + 1, 0:1]
        dm = jnp.where(tri, f_col - f_row + i_row, -1e30)
        inter = f_col + m_prev
        mt = jnp.maximum(jnp.max(dm, -1, keepdims=True), inter)
        w = jnp.where(tri, jnp.exp(dm - mt), 0.0)
        s = _dot_nt(q_bf, k_bf) * w
        a = jnp.exp(inter - mt)
        c_old = c_scr[h]
        n_old = n_scr[h:h + 1, :]
        num = _dot(s.astype(BF16), vc.astype(BF16)) + a * _dot_nt(q_bf, c_old.astype(BF16))
        den = jnp.sum(s, -1, keepdims=True) + a * jnp.sum(qc * n_old, -1, keepdims=True)
        hh = num / jnp.maximum(jnp.abs(den), jnp.exp(-mt))
        m_new = mt[c - 1:c, :]
        wend = jnp.exp(f_col[c - 1:c, :] - f_col + i_col - m_new)
        decay = jnp.exp(inter[c - 1:c, :] - m_new)
        c_scr[h] = decay * c_old + _dot_tn((wend * vc).astype(BF16), k_bf)
        n_scr[h:h + 1, :] = decay * n_old + jnp.sum(wend * kc, 0, keepdims=True)
        m_scr[h:h + 1, :] = jnp.broadcast_to(m_new, (1, LANES))
        hn = hh * lax.rsqrt(jnp.mean(hh * hh, -1, keepdims=True) + EPS) * hg_ref[h:h + 1, :]
        a_ref[0, :, h * ML_DV:(h + 1) * ML_DV] = _sigmoid(og[:, h * ML_DV:(h + 1) * ML_DV]) * hn

    @pl.when(j == pl.num_programs(1) - 1)
    def _():
        cout_ref[0] = c_scr[...]
        nout_ref[0] = n_scr[...]
        mout_ref[0] = m_scr[...]


def mlstm_cell(q, k, v, gates, og, b_if, head_g, state):
    n, seq, _ = q.shape
    c = min(LANES, seq)
    nc = seq // c
    has_init = state is not None
    gates_t = jnp.swapaxes(gates[:, :, :2 * ML_HEADS], 1, 2)
    b_flat = b_if.reshape(2 * ML_HEADS)
    b_col = jnp.pad(b_flat, (0, LANES - 2 * ML_HEADS)).reshape(1, LANES)
    b_row = b_flat.reshape(2 * ML_HEADS, 1)
    tok = lambda w: pl.BlockSpec((1, c, w), lambda b, j: (b, j, 0))
    fix2 = lambda s: pl.BlockSpec(s, lambda b, j: (0, 0))
    in_specs = [tok(ML_QK), tok(ML_QK), tok(D_MODEL), tok(LANES),
                pl.BlockSpec((1, 2 * ML_HEADS, c), lambda b, j: (b, 0, j)), tok(D_MODEL),
                fix2((1, LANES)), fix2((2 * ML_HEADS, 1)), fix2((ML_HEADS, ML_DV))]
    args = [q, k, v, gates, gates_t, og, b_col, b_row, head_g]
    st4 = pl.BlockSpec((1, ML_HEADS, ML_DV, ML_DK), lambda b, j: (b, 0, 0, 0))
    st3 = pl.BlockSpec((1, ML_HEADS, LANES), lambda b, j: (b, 0, 0))
    if has_init:
        c0, n0, m0 = state
        in_specs += [st4, st3, st3]
        args += [c0, n0, jnp.broadcast_to(m0[:, :, None], (n, ML_HEADS, LANES))]
    a, c_new, n_new, m_new = pl.pallas_call(
        functools.partial(_mlstm_kernel, c=c, has_init=has_init),
        grid=(n, nc),
        in_specs=in_specs,
        out_specs=[tok(D_MODEL), st4, st3, st3],
        out_shape=[jax.ShapeDtypeStruct((n, seq, D_MODEL), F32),
                   jax.ShapeDtypeStruct((n, ML_HEADS, ML_DV, ML_DK), F32),
                   jax.ShapeDtypeStruct((n, ML_HEADS, ML_DK), F32),
                   jax.ShapeDtypeStruct((n, ML_HEADS, LANES), F32)],
        scratch_shapes=[pltpu.VMEM((ML_HEADS, ML_DV, ML_DK), F32), pltpu.VMEM((ML_HEADS, ML_DK), F32),
                        pltpu.VMEM((ML_HEADS, LANES), F32)],
        compiler_params=_cparams("parallel", "arbitrary"),
        name="mlstm",
    )(*args)
    return a, c_new, n_new, m_new[:, :, 0]


def _conv_kernel(a_ref, b_ref, buf_ref, w_ref, dwb_ref, lng_ref, lnb_ref, o_ref, nbuf_ref, ctx_scr, *, tt):
    j = pl.program_id(1)

    @pl.when(j == 0)
    def _():
        ctx_scr[0:CONV_HALO, :] = buf_ref[0]

    @pl.when(j > 0)
    def _():
        ctx_scr[0:CONV_HALO, :] = ctx_scr[tt:tt + CONV_HALO, :]

    ctx_scr[CONV_HALO:CONV_HALO + tt, :] = a_ref[0] * _sigmoid(b_ref[0])
    off = CONV_HALO - (CONV_WIDTH - 1)
    acc = jnp.zeros((tt, D_MODEL), F32) + dwb_ref[...]
    for tap in range(CONV_WIDTH):
        acc = acc + ctx_scr[off + tap:off + tap + tt, :] * w_ref[tap:tap + 1, :]
    mu = jnp.mean(acc, -1, keepdims=True)
    xc = acc - mu
    var = jnp.mean(xc * xc, -1, keepdims=True)
    o_ref[0] = _silu(xc * lax.rsqrt(var + EPS) * lng_ref[...] + lnb_ref[...])
    nbuf_ref[0] = ctx_scr[tt:tt + CONV_HALO, :]


def conv_block(a, b, buf, dw_w, dw_b, ln_g, ln_b):
    n, seq, d = a.shape
    tt = min(ATT_TILE, seq)
    assert seq == tt or tt >= CONV_HALO
    hist = CONV_WIDTH - 1
    if buf is None:
        buf_pad = jnp.zeros((n, CONV_HALO, d), F32)
    else:
        buf_pad = jnp.pad(buf, ((0, 0), (CONV_HALO - hist, 0), (0, 0)))
    w_pad = jnp.pad(dw_w, ((0, CONV_HALO - CONV_WIDTH), (0, 0)))
    tok = pl.BlockSpec((1, tt, d), lambda bb, j: (bb, j, 0))
    vec = pl.BlockSpec((1, d), lambda bb, j: (0, 0))
    halo = pl.BlockSpec((1, CONV_HALO, d), lambda bb, j: (bb, 0, 0))
    out, nbuf = pl.pallas_call(
        functools.partial(_conv_kernel, tt=tt),
        grid=(n, seq // tt),
        in_specs=[tok, tok, halo, pl.BlockSpec((CONV_HALO, d), lambda bb, j: (0, 0)), vec, vec, vec],
        out_specs=[tok, halo],
        out_shape=[jax.ShapeDtypeStruct((n, seq, d), F32), jax.ShapeDtypeStruct((n, CONV_HALO, d), F32)],
        scratch_shapes=[pltpu.VMEM((CONV_HALO + tt, d), F32)],
        compiler_params=_cparams("parallel", "arbitrary"),
        name="conv_block",
    )(a, b, buf_pad, w_pad, dw_b.reshape(1, d), ln_g.reshape(1, d), ln_b.reshape(1, d))
    return out, nbuf[:, CONV_HALO - hist:, :]


def _page_spec(block, col_block, j):
    return pl.BlockSpec(block, lambda b, pt: (pt[b, j], 0, col_block))


def _page_means_kernel(pt_ref, *refs):
    page_refs, o_ref = refs[:-1], refs[-1]
    for j, page_ref in enumerate(page_refs):
        x = page_ref[0]
        blocks = x.shape[0] // NSA_BLOCK
        o_ref[0, j * blocks:(j + 1) * blocks, :] = (
            jnp.sum(x.reshape(blocks, NSA_BLOCK, x.shape[1]), axis=1) * (1.0 / NSA_BLOCK))


def nsa_page_means(pool, page_table):
    n, n_pages = page_table.shape
    page = pool.shape[1]
    nbp = n_pages * page // NSA_BLOCK
    return pl.pallas_call(
        _page_means_kernel,
        grid_spec=pltpu.PrefetchScalarGridSpec(
            num_scalar_prefetch=1, grid=(n,),
            in_specs=[_page_spec((1, page, 2 * NSA_KVW), 0, j) for j in range(n_pages)],
            out_specs=pl.BlockSpec((1, nbp, 2 * NSA_KVW), lambda b, pt: (b, 0, 0))),
        out_shape=jax.ShapeDtypeStruct((n, nbp, 2 * NSA_KVW), F32),
        compiler_params=_cparams("parallel"),
        name="nsa_page_means",
    )(page_table, *([pool] * n_pages))


def _stack_group_queries(q, kh):
    parts = [q[:, (kh * NSA_GROUP + g) * NSA_DH:(kh * NSA_GROUP + g + 1) * NSA_DH] for g in range(NSA_GROUP)]
    return (jnp.concatenate(parts, axis=0) * (NSA_DH ** -0.5)).astype(BF16)


def _nsa_sample_local_kernel(q_ref, means_ref, rows_ref, win_ref, neww_ref, g_ref, bc_ref, bw_ref,
                             oc_ref, ow_ref, selb_ref, wout_ref, mk_scr, ctx_scr, *, past, seq, k_sel):
    nbp = means_ref.shape[1]
    wb = win_ref.shape[1]
    mk_scr[...] = jnp.zeros(mk_scr.shape, F32)
    mk_scr[0:nbp, :] = means_ref[0]
    mk_scr[nbp:nbp + 1, :] = jnp.sum(rows_ref[0][:, 0:2 * NSA_KVW], 0, keepdims=True) * (1.0 / NSA_BLOCK)
    ctx_scr[...] = jnp.zeros(ctx_scr.shape, F32)
    ctx_scr[0:wb, :] = win_ref[0]
    ctx_scr[wb:wb + seq, :] = neww_ref[0]
    wout_ref[0] = ctx_scr[seq:seq + wb, :]
    q = q_ref[0]
    gate = _sigmoid(g_ref[0])
    pos = past + lax.broadcasted_iota(jnp.int32, (seq, 1), 0)
    cur = (pos // NSA_BLOCK).astype(F32)
    imps = []
    for kh in range(NSA_KV_HEADS):
        qs = _stack_group_queries(q, kh)
        lo, hi = kh * NSA_DH, (kh + 1) * NSA_DH
        p = _softmax_rows(_dot_nt(qs, mk_scr[:, lo:hi].astype(BF16)) + bc_ref[kh])
        oc = _dot(p.astype(BF16), mk_scr[:, NSA_KVW + lo:NSA_KVW + hi].astype(BF16))
        imp = p[0:seq]
        for g in range(1, NSA_GROUP):
            imp = imp + p[g * seq:(g + 1) * seq]
        imps.append(imp)
        pw = _softmax_rows(_dot_nt(qs, ctx_scr[:, lo:hi].astype(BF16)) + bw_ref[kh])
        ow = _dot(pw.astype(BF16), ctx_scr[:, NSA_KVW + lo:NSA_KVW + hi].astype(BF16))
        for g in range(NSA_GROUP):
            h = kh * NSA_GROUP + g
            oc_ref[0, :, h * NSA_DH:(h + 1) * NSA_DH] = oc[g * seq:(g + 1) * seq] * gate[:, h:h + 1]
            ow_ref[0, :, h * NSA_DH:(h + 1) * NSA_DH] = (
                ow[g * seq:(g + 1) * seq] * gate[:, 2 * NSA_HEADS + h:2 * NSA_HEADS + h + 1])
    sel = _select_blocks(jnp.concatenate(imps, axis=0), jnp.concatenate([cur] * NSA_KV_HEADS, axis=0), k_sel)
    for kh in range(NSA_KV_HEADS):
        selb_ref[0, kh] = sel[kh * seq:(kh + 1) * seq]


def _group_rows(tab):
    return tab.reshape(NSA_KV_HEADS, NSA_GROUP * tab.shape[1], tab.shape[2])


def nsa_sample_local(q, means, rows, win_buf, new_win, gates, rel_bias, past):
    n, seq, _ = q.shape
    nbp = means.shape[1]
    wb = win_buf.shape[1]
    nb = -(-(past + seq) // NSA_BLOCK)
    assert nbp + 1 == nb and nb <= LANES and seq % 8 == 0
    lw = -(-(wb + seq) // LANES) * LANES
    qpos = past + jnp.arange(seq)[:, None]
    bend = jnp.arange(LANES)[None, :] * NSA_BLOCK + NSA_BLOCK - 1
    bc = _group_rows(_bias_table(rel_bias, qpos - bend, bend <= qpos))
    cidx = jnp.arange(lw)[None, :]
    wpos = past - wb + cidx
    dw = qpos - wpos
    bw = _group_rows(_bias_table(rel_bias, dw, (dw >= 0) & (dw <= NSA_WINDOW) & (wpos >= 0) & (cidx < wb + seq)))
    rows_q = NSA_GROUP * seq
    per_b = lambda s: pl.BlockSpec((1,) + s, lambda b: (b,) + (0,) * len(s))
    fix = lambda s: pl.BlockSpec(s, lambda b: (0,) * len(s))
    return pl.pallas_call(
        functools.partial(_nsa_sample_local_kernel, past=past, seq=seq, k_sel=min(NSA_TOPK, nb)),
        grid=(n,),
        in_specs=[per_b((seq, D_MODEL)), per_b((nbp, 2 * NSA_KVW)), per_b((seq, D_MODEL)), per_b((wb, 2 * NSA_KVW)),
                  per_b((seq, 2 * NSA_KVW)), per_b((seq, LANES)), fix((NSA_KV_HEADS, rows_q, LANES)),
                  fix((NSA_KV_HEADS, rows_q, lw))],
        out_specs=[per_b((seq, D_MODEL)), per_b((seq, D_MODEL)), per_b((NSA_KV_HEADS, seq, LANES)),
                   per_b((wb, 2 * NSA_KVW))],
        out_shape=[jax.ShapeDtypeStruct((n, seq, D_MODEL), F32), jax.ShapeDtypeStruct((n, seq, D_MODEL), F32),
                   jax.ShapeDtypeStruct((n, NSA_KV_HEADS, seq, LANES), F32),
                   jax.ShapeDtypeStruct((n, wb, 2 * NSA_KVW), F32)],
        scratch_shapes=[pltpu.VMEM((LANES, 2 * NSA_KVW), F32), pltpu.VMEM((lw, 2 * NSA_KVW), F32)],
        compiler_params=_cparams("parallel"),
        name="nsa_sample_local",
    )(q, means, rows, win_buf, new_win, gates, bc, bw)


def _nsa_sample_selected_kernel(pt_ref, q_ref, rows_ref, selb_ref, bias_ref, g_ref, *refs, seq, page, past):
    n_pages = len(refs) - 2
    page_refs, o_ref, new_scr = refs[:n_pages], refs[n_pages], refs[n_pages + 1]
    new_scr[...] = jnp.zeros(new_scr.shape, F32)
    new_scr[0:seq, :] = rows_ref[0][:, 2 * NSA_KVW:4 * NSA_KVW]
    q = q_ref[0]
    gate = _sigmoid(g_ref[0])
    sources = [(page_refs[j], j * page) for j in range(n_pages)] + [(None, past)]
    expands = [_block_expand(first_pos, page) for _, first_pos in sources]
    for kh in range(NSA_KV_HEADS):
        lo, hi = kh * NSA_DH, (kh + 1) * NSA_DH
        qs = _stack_group_queries(q, kh)
        sel = selb_ref[0, kh].astype(BF16)
        scores = []
        for j, (ref, _) in enumerate(sources):
            k_bf = (new_scr[:, lo:hi] if ref is None else ref[0, :, lo:hi]).astype(BF16)
            sb = _dot(sel, expands[j])
            scores.append(_dot_nt(qs, k_bf) + bias_ref[j, kh] + jnp.concatenate([sb] * NSA_GROUP, axis=0))
        mx = scores[0]
        for s in scores[1:]:
            mx = jnp.maximum(mx, s)
        m = jnp.maximum(jnp.max(mx, -1, keepdims=True), M_FLOOR)
        total = None
        acc = None
        for j, (ref, _) in enumerate(sources):
            p = jnp.exp(scores[j] - m)
            v_bf = (new_scr[:, NSA_KVW + lo:NSA_KVW + hi] if ref is None
                    else ref[0, :, NSA_KVW + lo:NSA_KVW + hi]).astype(BF16)
            pv = _dot(p.astype(BF16), v_bf)
            total = p if total is None else total + p
            acc = pv if acc is None else acc + pv
        o = acc / jnp.maximum(jnp.sum(total, -1, keepdims=True), 1e-30)
        for g in range(NSA_GROUP):
            h = kh * NSA_GROUP + g
            o_ref[0, :, h * NSA_DH:(h + 1) * NSA_DH] = (
                o[g * seq:(g + 1) * seq] * gate[:, NSA_HEADS + h:NSA_HEADS + h + 1])


def nsa_sample_selected(q, pool, page_table, rows, selb, gates, rel_bias, past):
    n, seq, _ = q.shape
    n_pages = page_table.shape[1]
    page = pool.shape[1]
    assert page == LANES and seq <= page
    qpos = past + jnp.arange(seq)[:, None]
    j = jnp.arange(page)[None, :]
    d_past = qpos[None] - (jnp.arange(n_pages)[:, None, None] * page + j[None])
    d_new = (qpos - past) - j
    dist = jnp.concatenate([d_past, d_new[None]], 0)
    valid = jnp.concatenate([d_past >= 0, ((d_new >= 0) & (j < seq))[None]], 0)
    bias = _bias_table(rel_bias, dist, valid)
    bias = jnp.moveaxis(bias, 1, 0).reshape(n_pages + 1, NSA_KV_HEADS, NSA_GROUP * seq, page)
    rows_q = NSA_GROUP * seq
    per_b = lambda s: pl.BlockSpec((1,) + s, lambda b, pt: (b,) + (0,) * len(s))
    return pl.pallas_call(
        functools.partial(_nsa_sample_selected_kernel, seq=seq, page=page, past=past),
        grid_spec=pltpu.PrefetchScalarGridSpec(
            num_scalar_prefetch=1, grid=(n,),
            in_specs=[per_b((seq, D_MODEL)), per_b((seq, D_MODEL)), per_b((NSA_KV_HEADS, seq, LANES)),
                      pl.BlockSpec((n_pages + 1, NSA_KV_HEADS, rows_q, page), lambda b, pt: (0, 0, 0, 0)),
                      per_b((seq, LANES))]
            + [_page_spec((1, page, 2 * NSA_KVW), 1, jj) for jj in range(n_pages)],
            out_specs=per_b((seq, D_MODEL)),
            scratch_shapes=[pltpu.VMEM((page, 2 * NSA_KVW), F32)]),
        out_shape=jax.ShapeDtypeStruct((n, seq, D_MODEL), F32),
        compiler_params=_cparams("parallel"),
        name="nsa_sample_selected",
    )(page_table, q, rows, selb, bias, gates, *([pool] * n_pages))


def _diff_sample_kernel(pt_ref, q_ref, new_ref, tfar_ref, tnear_ref, tnew_ref, lam_ref, hg_ref, *refs,
                        seq, lam_init, pps, n_near):
    k_refs, v_refs = refs[:pps], refs[pps:2 * pps]
    o_ref, qall_scr, m_scr, l_scr, acc_scr = refs[2 * pps:]
    g = pl.program_id(1)
    last = pl.num_programs(1) - 1
    rows_k = k_refs[0].shape[0] * DF_HEADS
    pieces = rows_k // LANES

    @pl.when(g == 0)
    def _():
        m_scr[...] = jnp.full(m_scr.shape, M_FLOOR, F32)
        l_scr[...] = jnp.zeros(l_scr.shape, F32)
        acc_scr[...] = jnp.zeros(acc_scr.shape, F32)
        q = q_ref[0] * (DF_DK ** -0.5)
        half = lax.broadcasted_iota(jnp.int32, (seq, DF_DV), 1) // DF_DK
        parts = []
        for h in range(DF_HEADS):
            qh = q[:, h * DF_DV:(h + 1) * DF_DV]
            parts += [jnp.where(half == c, qh, 0.0) for c in range(2)]
        qall_scr[...] = jnp.concatenate(parts, axis=0).astype(BF16)

    qall = qall_scr[...]
    for jj in range(pps):
        kf = k_refs[jj][...].reshape(rows_k, DF_DV).astype(BF16)
        vf = v_refs[jj][...].reshape(rows_k, DF_DV).astype(BF16)
        s = _dot_nt(qall, kf)
        near = jj - (pps - n_near)
        ps = []
        for i in range(pieces):
            tab = tfar_ref[:, i * LANES:(i + 1) * LANES]
            if near >= 0:
                tab = jnp.where(g == last, tnear_ref[near, :, i * LANES:(i + 1) * LANES], tab)
            ps.append(s[:, i * LANES:(i + 1) * LANES] + tab)
        _flash_update(ps, vf, m_scr, acc_scr, slice(None), l_ref=l_scr)

    @pl.when(g == last)
    def _():
        new = new_ref[0]
        pad = jnp.zeros((LANES - DF_HEADS * seq, DF_DV), F32)
        kn = jnp.concatenate([new[:, h * DF_DV:(h + 1) * DF_DV] for h in range(DF_HEADS)] + [pad], axis=0)
        vn = jnp.concatenate([new[:, D_MODEL + h * DF_DV:D_MODEL + (h + 1) * DF_DV] for h in range(DF_HEADS)]
                             + [pad], axis=0)
        s = _dot_nt(qall, kn.astype(BF16)) + tnew_ref[...]
        _flash_update([s], vn.astype(BF16), m_scr, acc_scr, slice(None), l_ref=l_scr)
        lam = _diff_lambda(lam_ref, lam_init)
        acc = acc_scr[...] / jnp.maximum(l_scr[...], 1e-30)
        for h in range(DF_HEADS):
            r0 = 2 * h * seq
            o = acc[r0:r0 + seq] - lam * acc[r0 + seq:r0 + 2 * seq]
            o = o * lax.rsqrt(jnp.mean(o * o, -1, keepdims=True) + EPS) * hg_ref[:, h * DF_DV:(h + 1) * DF_DV]
            o_ref[0, :, h * DF_DV:(h + 1) * DF_DV] = o * (1.0 - lam_init)


def diff_sample(q, pool6, j_layer, page_table, kv_new, lam_p, head_g, rel_bias, past, lam_init):
    n, seq, _ = q.shape
    n_pages = page_table.shape[1]
    page = pool6.shape[2]
    pps = min(DF_PAGES_PER_STEP, n_pages)
    rows_q = 2 * DF_HEADS * seq
    rows_k = page * DF_HEADS
    assert n_pages % pps == 0 and rows_q == LANES and DF_HEADS * seq <= LANES and _far_distance_ok()
    n_near = sum(1 for p in range(n_pages) if past - (p * page + page - 1) < FAR_DISTANCE)
    assert 1 <= n_near <= pps
    head_of_row = jnp.arange(rows_q) // (2 * seq)
    same_head = head_of_row[:, None] == (jnp.arange(rows_k) % DF_HEADS)[None, :]
    far = jnp.repeat(rel_bias[N_BUCKETS - 1], seq)[:, None]
    t_far = jnp.where(same_head, far, MASKED).astype(F32)
    qpos = past + jnp.arange(seq)[:, None]
    near_pages = jnp.arange(n_pages - n_near, n_pages)
    d_near = qpos[None] - (near_pages[:, None, None] * page + jnp.arange(page)[None, None, :])
    b_near = _bias_table(rel_bias, d_near, d_near >= 0)
    b_near = jnp.moveaxis(b_near, 1, 0).reshape(n_near, rows_q, page)
    b_near = jnp.broadcast_to(b_near[..., None], (n_near, rows_q, page, DF_HEADS)).reshape(n_near, rows_q, rows_k)
    t_near = jnp.where(same_head[None], b_near, MASKED).astype(F32)
    d_new = jnp.arange(seq)[:, None] - jnp.arange(seq)[None, :]
    b_new = _bias_table(rel_bias, d_new, d_new >= 0).reshape(rows_q, 1, seq)
    b_new = jnp.broadcast_to(b_new, (rows_q, DF_HEADS, seq)).reshape(rows_q, DF_HEADS * seq)
    same_new = head_of_row[:, None] == (jnp.arange(DF_HEADS * seq) // seq)[None, :]
    t_new = jnp.pad(jnp.where(same_new, b_new, MASKED), ((0, 0), (0, LANES - DF_HEADS * seq)),
                    constant_values=MASKED).astype(F32)

    def page_spec(jj, c):
        return pl.BlockSpec((None, None, page, None, DF_HEADS, DF_DV),
                            lambda b, g, pt: (j_layer, pt[b, g * pps + jj], 0, c, 0, 0))

    fix = lambda s: pl.BlockSpec(s, lambda b, g, pt: (0,) * len(s))
    return pl.pallas_call(
        functools.partial(_diff_sample_kernel, seq=seq, lam_init=lam_init, pps=pps, n_near=n_near),
        grid_spec=pltpu.PrefetchScalarGridSpec(
            num_scalar_prefetch=1, grid=(n, n_pages // pps),
            in_specs=[pl.BlockSpec((1, seq, D_MODEL), lambda b, g, pt: (b, 0, 0)),
                      pl.BlockSpec((1, seq, 2 * D_MODEL), lambda b, g, pt: (b, 0, 0)),
                      fix((rows_q, rows_k)), fix((n_near, rows_q, rows_k)), fix((rows_q, LANES)),
                      fix((4, DF_DK)), fix((1, D_MODEL))]
            + [page_spec(jj, 0) for jj in range(pps)] + [page_spec(jj, 1) for jj in range(pps)],
            out_specs=pl.BlockSpec((1, seq, D_MODEL), lambda b, g, pt: (b, 0, 0)),
            scratch_shapes=[pltpu.VMEM((rows_q, DF_DV), BF16), pltpu.VMEM((rows_q, LANES), F32),
                            pltpu.VMEM((rows_q, LANES), F32), pltpu.VMEM((rows_q, DF_DV), F32)]),
        out_shape=jax.ShapeDtypeStruct((n, seq, D_MODEL), F32),
        compiler_params=_cparams("parallel", "arbitrary"),
        name="diff_sample",
    )(page_table, q, kv_new, t_far, t_near, t_new, lam_p, head_g.reshape(1, D_MODEL), *([pool6] * (2 * pps)))


def _trunk(x, p, past, page_table, nsa_pool, nsa_win, ml_c, ml_n, ml_m, conv_buf, diff_pool, weights):
    (norm_g, final_norm_g, rel_bias, ple_w, ple_gate_w, ple_norm_g,
     nsa_w_in, nsa_w_out, mlstm_w_in, mlstm_b_if, mlstm_head_g, mlstm_w_out,
     conv_w_in, conv_dw_w, conv_dw_b, conv_ln_g, conv_ln_b, conv_w_out,
     diff_w_in, diff_lam, diff_head_g, diff_w_out) = weights
    n, seq, d = x.shape
    m = n * seq
    sample = page_table is not None
    x2 = x.reshape(m, d)
    depth = norm_g.shape[0]
    out = {}
    for i in range(depth):
        kind, j = i % 4, i // 4
        assert j == 0
        p_i = p[i].reshape(m, D_PLE)
        tail = dict(x=x2, p=p_i, ple_norm_g=ple_norm_g[i], ple_gate_w=ple_gate_w[i], ple_w=ple_w[i],
                    final_g=final_norm_g if i == depth - 1 else None)
        if kind == 0:
            w = nsa_w_in[j]
            c0, c1, c2, c3 = D_MODEL, 2 * D_MODEL, 2 * D_MODEL + 2 * NSA_KVW, 2 * D_MODEL + 2 * NSA_KVW + 3 * NSA_HEADS
            q, rows, win, gates, z = norm_linear(x2, norm_g[i], [w[:, :c0], w[:, c0:c1], w[:, c1:c2], w[:, c2:c3], w[:, c3:]])
            q3, rows3, win3, gates3 = (t.reshape(n, seq, -1) for t in (q, rows, win, gates))
            if sample:
                pool = nsa_pool[j].reshape(nsa_pool.shape[1], nsa_pool.shape[2], D_MODEL)
                wb = nsa_win.shape[2]
                means = nsa_page_means(pool, page_table)
                oc, ow, selb, win_out = nsa_sample_local(q3, means, rows3, nsa_win[j].reshape(n, wb, 2 * NSA_KVW),
                                                         win3, gates3, rel_bias, past)
                os_ = nsa_sample_selected(q3, pool, page_table, rows3, selb, gates3, rel_bias, past)
            else:
                oc, selb = nsa_cmp_prompt(q3, rows3, gates3, rel_bias)
                os_ = nsa_flash_prompt(q3, rows3, 4, 6, gates3, rel_bias, selb, window=False)
                ow = nsa_flash_prompt(q3, win3, 0, 2, gates3, rel_bias, None, window=True)
                keep = min(NSA_WINDOW, seq)
                win_out = win3[:, seq - keep:]
                if keep < NSA_WINDOW:
                    win_out = jnp.concatenate([jnp.zeros((n, NSA_WINDOW - keep, 2 * NSA_KVW), F32), win_out], 1)
            out['kv'] = rows3.reshape(1, n, seq, 4, NSA_KV_HEADS, NSA_DH)
            out['win'] = win_out.reshape(1, n, win_out.shape[1], 2, NSA_KV_HEADS, NSA_DH)
            x2 = post_layer([t.reshape(m, d) for t in (oc, os_, ow)], z, w_out=nsa_w_out[j], **tail)
        elif kind == 1:
            w = mlstm_w_in[j]
            c0, c1, c2 = ML_QK, 2 * ML_QK, 2 * ML_QK + D_MODEL
            c3, c4 = c2 + 2 * ML_HEADS, c2 + 2 * ML_HEADS + D_MODEL
            q, k, v, gates, og, z = norm_linear(
                x2, norm_g[i], [w[:, :c0], w[:, c0:c1], w[:, c1:c2], w[:, c2:c3], w[:, c3:c4], w[:, c4:]])
            state = (ml_c[j], ml_n[j], ml_m[j]) if sample else None
            a, c_new, n_new, m_new = mlstm_cell(*(t.reshape(n, seq, -1) for t in (q, k, v, gates, og)),
                                                mlstm_b_if[j], mlstm_head_g[j], state)
            out['C'], out['n'], out['m'] = c_new[None], n_new[None], m_new[None]
            x2 = post_layer([a.reshape(m, d)], z, w_out=mlstm_w_out[j], **tail)
        elif kind == 2:
            w = conv_w_in[j]
            a, b, z = norm_linear(x2, norm_g[i], [w[:, :d], w[:, d:2 * d], w[:, 2 * d:]])
            cact, nbuf = conv_block(a.reshape(n, seq, d), b.reshape(n, seq, d), conv_buf[j] if sample else None,
                                    conv_dw_w[j], conv_dw_b[j], conv_ln_g[j], conv_ln_b[j])
            out['conv'] = nbuf[None]
            x2 = post_layer([cact.reshape(m, d)], z, w_out=conv_w_out[j], **tail)
        else:
            w = diff_w_in[j]
            q, kv, z = norm_linear(x2, norm_g[i], [w[:, :d], w[:, d:3 * d], w[:, 3 * d:]])
            lam_init = 0.8 - 0.6 * math.exp(-0.3 * i)
            q3, kv3 = q.reshape(n, seq, d), kv.reshape(n, seq, 2 * d)
            if sample:
                a = diff_sample(q3, diff_pool, j, page_table, kv3, diff_lam[j], diff_head_g[j], rel_bias, past,
                                lam_init)
            else:
                a = diff_flash_prompt(q3, kv3, diff_lam[j], diff_head_g[j], rel_bias, lam_init)
            out['diff'] = kv3.reshape(1, n, seq, 2, DF_HEADS, DF_DV)
            x2 = post_layer([a.reshape(m, d)], z, w_out=diff_w_out[j], **tail)
    return x2.reshape(n, seq, d), out


def kernel(x_prompt, x_sample, cache_nsa_kv, state_nsa_win, state_mlstm_C, state_mlstm_n, state_mlstm_m,
           state_conv, cache_diff_kv, page_table, p_prompt, p_sample, norm_g, final_norm_g, rel_bias,
           ple_w, ple_gate_w, ple_norm_g, nsa_w_in, nsa_w_out, mlstm_w_in, mlstm_b_if, mlstm_head_g,
           mlstm_w_out, conv_w_in, conv_dw_w, conv_dw_b, conv_ln_g, conv_ln_b, conv_w_out, diff_w_in,
           diff_lam, diff_head_g, diff_w_out):
    weights = (norm_g, final_norm_g, rel_bias, ple_w, ple_gate_w, ple_norm_g,
               nsa_w_in, nsa_w_out, mlstm_w_in, mlstm_b_if, mlstm_head_g, mlstm_w_out,
               conv_w_in, conv_dw_w, conv_dw_b, conv_ln_g, conv_ln_b, conv_w_out,
               diff_w_in, diff_lam, diff_head_g, diff_w_out)
    past = page_table.shape[1] * cache_nsa_kv.shape[2]
    y_p, o_p = _trunk(x_prompt, p_prompt, 0, None, None, None, None, None, None, None, None, weights)
    y_s, o_s = _trunk(x_sample, p_sample, past, page_table, cache_nsa_kv, state_nsa_win, state_mlstm_C,
                      state_mlstm_n, state_mlstm_m, state_conv, cache_diff_kv, weights)
    return (y_p, y_s, o_p['kv'], o_s['kv'], o_p['win'], o_s['win'], o_p['C'], o_s['C'], o_p['n'], o_s['n'],
            o_p['m'], o_s['m'], o_p['conv'], o_s['conv'], o_p['diff'], o_s['diff'])
```

```python
import functools
import math

import jax
import jax.numpy as jnp
import numpy as np
from jax import lax
from jax.experimental import pallas as pl
from jax.experimental.pallas import tpu as pltpu

F32 = jnp.float32
BF16 = jnp.bfloat16

D_MODEL = 1024
D_PLE = 256
N_BUCKETS = 32
MAX_DISTANCE = 128
EPS = 1e-6

NSA_HEADS = 16
NSA_KV_HEADS = 4
NSA_GROUP = 4
NSA_DH = 64
NSA_BLOCK = 32
NSA_TOPK = 16
NSA_WINDOW = 512
NSA_KVW = NSA_KV_HEADS * NSA_DH

ML_HEADS = 4
ML_DV = 256
ML_DK = 128
ML_QK = ML_HEADS * ML_DK

CONV_WIDTH = 31
CONV_HALO = 32

DF_HEADS = 8
DF_DK = 64
DF_DV = 128
DF_PAGES_PER_STEP = 8
DF_HEADS_PER_STEP = 2

LANES = 128
MASKED = -2e30
M_FLOOR = -1e30
FAR_DISTANCE = 113
VMEM_LIMIT = 48 * 1024 * 1024
ROW_TILE = 256
ATT_TILE = 256
COL_CHUNK = 512


def _cparams(*sem):
    return pltpu.CompilerParams(dimension_semantics=sem, vmem_limit_bytes=VMEM_LIMIT)


def _dot(a, b):
    return jnp.dot(a, b, preferred_element_type=F32)


def _dot_nt(a, b):
    return lax.dot_general(a, b, (((1,), (1,)), ((), ())), preferred_element_type=F32)


def _dot_tn(a, b):
    return lax.dot_general(a, b, (((0,), (0,)), ((), ())), preferred_element_type=F32)


def _dot_exact(a, b):
    return jnp.dot(a, b, preferred_element_type=F32, precision=lax.Precision.HIGHEST)


def _sigmoid(x):
    return 1.0 / (1.0 + jnp.exp(-x))


def _silu(x):
    return x * _sigmoid(x)


def _log_sigmoid(x):
    return jnp.minimum(x, 0.0) - jnp.log1p(jnp.exp(-jnp.abs(x)))


def _pad_cols(w, mult=LANES):
    pad = (-w.shape[-1]) % mult
    return w if pad == 0 else jnp.pad(w, [(0, 0)] * (w.ndim - 1) + [(0, pad)])


def _norm_linear_kernel(x_ref, g_ref, *refs, widths):
    w_refs, o_refs = refs[:len(widths)], refs[len(widths):]
    x = x_ref[...]
    xn = (x * lax.rsqrt(jnp.mean(x * x, -1, keepdims=True) + EPS) * g_ref[...]).astype(BF16)
    for w_ref, o_ref, width in zip(w_refs, o_refs, widths):
        for c0 in range(0, width, COL_CHUNK):
            c1 = min(c0 + COL_CHUNK, width)
            o_ref[:, c0:c1] = _dot(xn, w_ref[:, c0:c1])


def norm_linear(x2d, g, weights):
    m, d = x2d.shape
    ws = [_pad_cols(w).astype(BF16) for w in weights]
    widths = tuple(w.shape[1] for w in ws)
    tm = min(ROW_TILE, m)
    outs = pl.pallas_call(
        functools.partial(_norm_linear_kernel, widths=widths),
        grid=(m // tm,),
        in_specs=[pl.BlockSpec((tm, d), lambda i: (i, 0)), pl.BlockSpec((1, d), lambda i: (0, 0))]
        + [pl.BlockSpec((d, wd), lambda i: (0, 0)) for wd in widths],
        out_specs=[pl.BlockSpec((tm, wd), lambda i: (i, 0)) for wd in widths],
        out_shape=[jax.ShapeDtypeStruct((m, wd), F32) for wd in widths],
        compiler_params=_cparams("parallel"),
        name="norm_linear",
    )(x2d, g.reshape(1, d), *ws)
    return outs


def _post_kernel(*refs, n_a, final):
    a_refs = refs[:n_a]
    z_ref, x_ref, p_ref, wo_ref, png_ref, wg_ref, wp_ref = refs[n_a:n_a + 7]
    rest = refs[n_a + 7:]
    fng_ref = rest[0] if final else None
    o_ref = rest[-1]
    a = a_refs[0][...]
    for r in a_refs[1:]:
        a = a + r[...]
    h = (a * _silu(z_ref[...])).astype(BF16)
    x1 = x_ref[...] + _dot(h, wo_ref[...])
    r = (x1 * lax.rsqrt(jnp.mean(x1 * x1, -1, keepdims=True) + EPS) * png_ref[...]).astype(BF16)
    gate = _sigmoid(_dot(r, wg_ref[...]))
    x2 = x1 + gate * _dot(p_ref[...].astype(BF16), wp_ref[...])
    if final:
        x2 = x2 * lax.rsqrt(jnp.mean(x2 * x2, -1, keepdims=True) + EPS) * fng_ref[...]
    o_ref[...] = x2


def post_layer(a_list, z, x, p, w_out, ple_norm_g, ple_gate_w, ple_w, final_g=None):
    m, d = x.shape
    tm = min(ROW_TILE, m)
    final = final_g is not None
    row = lambda i: (i, 0)
    fix = lambda i: (0, 0)
    in_specs = [pl.BlockSpec((tm, d), row) for _ in a_list] + [
        pl.BlockSpec((tm, d), row), pl.BlockSpec((tm, d), row), pl.BlockSpec((tm, D_PLE), row),
        pl.BlockSpec((d, d), fix), pl.BlockSpec((1, d), fix), pl.BlockSpec((d, d), fix),
        pl.BlockSpec((D_PLE, d), fix)]
    args = list(a_list) + [z, x, p, w_out.astype(BF16), ple_norm_g.reshape(1, d), ple_gate_w.astype(BF16),
                           ple_w.astype(BF16)]
    if final:
        in_specs.append(pl.BlockSpec((1, d), fix))
        args.append(final_g.reshape(1, d))
    return pl.pallas_call(
        functools.partial(_post_kernel, n_a=len(a_list), final=final),
        grid=(m // tm,),
        in_specs=in_specs,
        out_specs=pl.BlockSpec((tm, d), row),
        out_shape=jax.ShapeDtypeStruct((m, d), F32),
        compiler_params=_cparams("parallel"),
        name="post_layer",
    )(*args)


def _t5_bucket(dist):
    d = jnp.maximum(dist, 0)
    exact = N_BUCKETS // 2
    df = jnp.maximum(d, 1).astype(F32)
    large = exact + (jnp.log(df / exact) / math.log(MAX_DISTANCE / exact) * (N_BUCKETS - exact)).astype(jnp.int32)
    return jnp.where(d < exact, d, jnp.minimum(large, N_BUCKETS - 1))


def _bias_table(rel_bias, dist, valid, rel_to_far=False):
    bucket = _t5_bucket(dist)[None]
    col = lambda k: rel_bias[k].reshape((-1,) + (1,) * dist.ndim)
    b = jnp.broadcast_to(col(N_BUCKETS - 1), (rel_bias.shape[1],) + dist.shape)
    for k in range(N_BUCKETS - 1):
        b = jnp.where(bucket == k, col(k), b)
    if rel_to_far:
        b = b - col(N_BUCKETS - 1)
    return jnp.where(valid[None], b, MASKED).astype(F32)


def _toeplitz_tiles(rel_bias, t, n_delta, max_dist=None, rel_to_far=False):
    i = jnp.arange(t)
    d = jnp.arange(n_delta)[:, None, None] * t + i[None, :, None] - i[None, None, :]
    valid = d >= 0
    if max_dist is not None:
        valid = valid & (d <= max_dist)
    return _bias_table(rel_bias, d, valid, rel_to_far)


def _far_distance_ok():
    d = np.arange(FAR_DISTANCE, FAR_DISTANCE + 8192, dtype=np.float64)
    large = 16 + np.floor(np.log(d / 16) / math.log(MAX_DISTANCE / 16) * 16 - 1e-6)
    return bool(np.all(large >= N_BUCKETS - 1))


def _flash_update(pieces, v_bf, m_ref, acc_ref, idx, l_ref=None, l_on_mxu=False):
    mx = pieces[0]
    for pc in pieces[1:]:
        mx = jnp.maximum(mx, pc)
    m_old = m_ref[idx]
    m_new = jnp.maximum(m_old, jnp.max(mx, -1, keepdims=True))
    alpha = jnp.exp(m_old - m_new)
    p = [jnp.exp(pc - m_new) for pc in pieces]
    if l_ref is not None and not l_on_mxu:
        total = p[0]
        for pc in p[1:]:
            total = total + pc
        l_ref[idx] = alpha * l_ref[idx] + jnp.sum(total, -1, keepdims=True)
    p = (p[0] if len(p) == 1 else jnp.concatenate(p, axis=1)).astype(BF16)
    if l_ref is not None and l_on_mxu:
        l_ref[idx] = alpha * l_ref[idx] + _dot(p, jnp.ones((p.shape[1], LANES), BF16))
    acc_ref[idx] = alpha * acc_ref[idx] + _dot(p, v_bf)
    m_ref[idx] = m_new


def _softmax_rows(s):
    m = jnp.maximum(jnp.max(s, -1, keepdims=True), M_FLOOR)
    e = jnp.exp(s - m)
    return e / jnp.maximum(jnp.sum(e, -1, keepdims=True), 1e-30)


def _select_blocks(imp, cur, k):
    imp_t = jnp.transpose(imp)
    blk = lax.broadcasted_iota(jnp.int32, imp_t.shape, 0).astype(F32)
    sel = jnp.where(blk == cur, float(NSA_GROUP + 1), jnp.where(blk < cur, imp_t, -1.0))
    chosen = jnp.zeros(imp_t.shape, F32)
    for _ in range(k):
        top = jnp.max(sel, 0, keepdims=True)
        first = jnp.min(jnp.where(sel == top, blk, float(LANES)), 0, keepdims=True)
        hit = blk == first
        chosen = jnp.where(hit, 1.0, chosen)
        sel = jnp.where(hit, -3.0, sel)
    return jnp.transpose(jnp.where(chosen > 0.0, 0.0, MASKED))


def _block_expand(first_pos, width):
    blk = lax.broadcasted_iota(jnp.int32, (LANES, width), 0)
    key_blk = (first_pos + lax.broadcasted_iota(jnp.int32, (LANES, width), 1)) // NSA_BLOCK
    return jnp.where(blk == key_blk, 1.0, 0.0).astype(BF16)


def _nsa_cmp_prompt_kernel(q_ref, rows_ref, g_ref, bias_ref, oc_ref, selb_ref, mk_scr, *, seq, tq, k_sel):
    i = pl.program_id(1)
    nb = seq // NSA_BLOCK

    @pl.when(i == 0)
    def _():
        means = jnp.sum(rows_ref[0].reshape(nb, NSA_BLOCK, 2 * NSA_KVW), axis=1) * (1.0 / NSA_BLOCK)
        mk_scr[...] = jnp.zeros(mk_scr.shape, F32)
        mk_scr[0:nb, :] = means

    q = q_ref[0]
    gate = _sigmoid(g_ref[0])
    pos = i * tq + lax.broadcasted_iota(jnp.int32, (1, tq), 1)
    cur = (pos // NSA_BLOCK).astype(F32)
    for kh in range(NSA_KV_HEADS):
        mk = mk_scr[:, kh * NSA_DH:(kh + 1) * NSA_DH].astype(BF16)
        mv = mk_scr[:, NSA_KVW + kh * NSA_DH:NSA_KVW + (kh + 1) * NSA_DH].astype(BF16)
        imp = jnp.zeros((tq, LANES), F32)
        for g in range(NSA_GROUP):
            h = kh * NSA_GROUP + g
            qh = (q[:, h * NSA_DH:(h + 1) * NSA_DH] * (NSA_DH ** -0.5)).astype(BF16)
            p = _softmax_rows(_dot_nt(qh, mk) + bias_ref[h])
            imp = imp + p
            oc_ref[0, :, h * NSA_DH:(h + 1) * NSA_DH] = _dot(p.astype(BF16), mv) * gate[:, h:h + 1]
        selb_ref[0, kh] = _select_blocks(imp, cur, k_sel)


def nsa_cmp_prompt(q, rows, gates, rel_bias):
    n, seq, _ = q.shape
    tq = min(ATT_TILE, seq)
    nb = seq // NSA_BLOCK
    assert nb <= LANES
    t = jnp.arange(seq)[:, None]
    bend = jnp.arange(LANES)[None, :] * NSA_BLOCK + NSA_BLOCK - 1
    bias = _bias_table(rel_bias, t - bend, bend <= t)
    return pl.pallas_call(
        functools.partial(_nsa_cmp_prompt_kernel, seq=seq, tq=tq, k_sel=min(NSA_TOPK, nb)),
        grid=(n, seq // tq),
        in_specs=[pl.BlockSpec((1, tq, D_MODEL), lambda b, i: (b, i, 0)),
                  pl.BlockSpec((1, seq, 2 * NSA_KVW), lambda b, i: (b, 0, 0)),
                  pl.BlockSpec((1, tq, LANES), lambda b, i: (b, i, 0)),
                  pl.BlockSpec((NSA_HEADS, tq, LANES), lambda b, i: (0, i, 0))],
        out_specs=[pl.BlockSpec((1, tq, D_MODEL), lambda b, i: (b, i, 0)),
                   pl.BlockSpec((1, NSA_KV_HEADS, tq, LANES), lambda b, i: (b, 0, i, 0))],
        out_shape=[jax.ShapeDtypeStruct((n, seq, D_MODEL), F32),
                   jax.ShapeDtypeStruct((n, NSA_KV_HEADS, seq, LANES), F32)],
        scratch_shapes=[pltpu.VMEM((LANES, 2 * NSA_KVW), F32)],
        compiler_params=_cparams("parallel", "arbitrary"),
        name="nsa_cmp_prompt",
    )(q, rows, gates, bias)


def _nsa_flash_prompt_kernel(*refs, t, window, gate_col):
    if window:
        q_ref, k_ref, v_ref, bias_ref, g_ref, o_ref, qs_scr, m_scr, acc_scr = refs
        selb_ref = None
    else:
        q_ref, k_ref, v_ref, bias_ref, g_ref, selb_ref, o_ref, qs_scr, m_scr, acc_scr = refs
    u = pl.program_id(1)
    qi = pl.program_id(2)
    heads = 2 * NSA_GROUP
    pieces = t // LANES

    m_scr[...] = jnp.full(m_scr.shape, M_FLOOR, F32)
    acc_scr[...] = jnp.zeros(acc_scr.shape, F32)
    q = q_ref[0]
    for h in range(heads):
        qs_scr[h] = (q[:, h * NSA_DH:(h + 1) * NSA_DH] * (NSA_DH ** -0.5)).astype(BF16)
    lane = lax.broadcasted_iota(jnp.int32, (t, LANES), 1)

    def tile(start, delta):
        k = k_ref[0, pl.ds(start, t), :]
        v = v_ref[0, pl.ds(start, t), :]
        if not window:
            expand = _block_expand(start, t)
        for kv in range(2):
            k_bf = k[:, kv * NSA_DH:(kv + 1) * NSA_DH].astype(BF16)
            v_aug = jnp.where((lane < NSA_DH) if kv == 0 else (lane >= NSA_DH), v, 1.0).astype(BF16)
            if not window:
                sel_bias = _dot(selb_ref[0, kv].astype(BF16), expand)
            for g in range(NSA_GROUP):
                h = kv * NSA_GROUP + g
                s = _dot_nt(qs_scr[h], k_bf)
                ps = []
                for i in range(pieces):
                    pc = s[:, i * LANES:(i + 1) * LANES]
                    if delta is not None:
                        pc = pc + bias_ref[h, delta, :, i * LANES:(i + 1) * LANES]
                    if not window:
                        pc = pc + sel_bias[:, i * LANES:(i + 1) * LANES]
                    ps.append(pc)
                _flash_update(ps, v_aug, m_scr, acc_scr, h)

    if window:
        for delta in range(NSA_WINDOW // t, 0, -1):
            @pl.when(qi >= delta)
            def _(delta=delta):
                tile(pl.multiple_of((qi - delta) * t, t), delta)
    else:
        def far_body(ki, carry):
            tile(pl.multiple_of(ki * t, t), None)
            return carry
        lax.fori_loop(0, jnp.maximum(qi - 1, 0), far_body, 0)

        @pl.when(qi >= 1)
        def _():
            tile(pl.multiple_of((qi - 1) * t, t), 1)
    tile(pl.multiple_of(qi * t, t), 0)

    gate = _sigmoid(g_ref[0])
    for h in range(heads):
        col = gate_col + h
        g0 = jnp.where(u == 0, gate[:, col:col + 1], gate[:, col + heads:col + heads + 1])
        acc = acc_scr[h]
        lo, hi = acc[:, 0:NSA_DH], acc[:, NSA_DH:2 * NSA_DH]
        o, l = (lo, hi) if h < NSA_GROUP else (hi, lo)
        o_ref[0, :, h * NSA_DH:(h + 1) * NSA_DH] = o / jnp.maximum(l, 1e-30) * g0


def nsa_flash_prompt(q, kv_arr, k_blk, v_blk, gates, rel_bias, selb, window):
    n, seq, _ = q.shape
    t = min(ATT_TILE, seq)
    nq = seq // t
    heads = 2 * NSA_GROUP
    assert t % LANES == 0 and _far_distance_ok()
    if window:
        assert NSA_WINDOW % t == 0
        n_delta = NSA_WINDOW // t + 1
        tiles = _toeplitz_tiles(rel_bias, t, n_delta, NSA_WINDOW)
    else:
        assert t + 1 >= FAR_DISTANCE
        n_delta = 2
        tiles = _toeplitz_tiles(rel_bias, t, n_delta, rel_to_far=True)
    in_specs = [pl.BlockSpec((1, t, heads * NSA_DH), lambda b, u, qi: (b, qi, u)),
                pl.BlockSpec((1, seq, LANES), lambda b, u, qi: (b, 0, k_blk + u)),
                pl.BlockSpec((1, seq, LANES), lambda b, u, qi: (b, 0, v_blk + u)),
                pl.BlockSpec((heads, n_delta, t, t), lambda b, u, qi: (u, 0, 0, 0)),
                pl.BlockSpec((1, t, LANES), lambda b, u, qi: (b, qi, 0))]
    args = [q, kv_arr, kv_arr, tiles, gates]
    if not window:
        in_specs.append(pl.BlockSpec((1, 2, t, LANES), lambda b, u, qi: (b, u, qi, 0)))
        args.append(selb)
    return pl.pallas_call(
        functools.partial(_nsa_flash_prompt_kernel, t=t, window=window, gate_col=2 * NSA_HEADS if window else NSA_HEADS),
        grid=(n, 2, nq),
        in_specs=in_specs,
        out_specs=pl.BlockSpec((1, t, heads * NSA_DH), lambda b, u, qi: (b, qi, u)),
        out_shape=jax.ShapeDtypeStruct((n, seq, D_MODEL), F32),
        scratch_shapes=[pltpu.VMEM((heads, t, NSA_DH), BF16), pltpu.VMEM((heads, t, LANES), F32),
                        pltpu.VMEM((heads, t, LANES), F32)],
        compiler_params=_cparams("parallel", "parallel", "arbitrary"),
        name="nsa_window_prompt" if window else "nsa_selected_prompt",
    )(*args)


def _diff_lambda(lam_ref, lam_init):
    lp = lam_ref[...]
    s1 = jnp.sum(lp[0:1] * lp[1:2], -1, keepdims=True)
    s2 = jnp.sum(lp[2:3] * lp[3:4], -1, keepdims=True)
    return jnp.exp(s1) - jnp.exp(s2) + lam_init


def _diff_flash_prompt_kernel(q_ref, k_ref, v_ref, bias_ref, lam_ref, hg_ref, o_ref,
                              qs_scr, m_scr, l_scr, acc_scr, *, t, lam_init):
    qi = pl.program_id(2)
    pieces = t // LANES
    maps = 2 * DF_HEADS_PER_STEP
    m_scr[...] = jnp.full(m_scr.shape, M_FLOOR, F32)
    l_scr[...] = jnp.zeros(l_scr.shape, F32)
    acc_scr[...] = jnp.zeros(acc_scr.shape, F32)
    q = q_ref[0]
    for c in range(maps):
        qs_scr[c] = (q[:, c * DF_DK:(c + 1) * DF_DK] * (DF_DK ** -0.5)).astype(BF16)

    def tile(start, delta):
        k = k_ref[0, pl.ds(start, t), :]
        v = v_ref[0, pl.ds(start, t), :]
        for c in range(maps):
            v_bf = v[:, (c // 2) * DF_DV:(c // 2 + 1) * DF_DV].astype(BF16)
            s = _dot_nt(qs_scr[c], k[:, c * DF_DK:(c + 1) * DF_DK].astype(BF16))
            ps = []
            for i in range(pieces):
                pc = s[:, i * LANES:(i + 1) * LANES]
                if delta is not None:
                    pc = pc + bias_ref[c, delta, :, i * LANES:(i + 1) * LANES]
                ps.append(pc)
            _flash_update(ps, v_bf, m_scr, acc_scr, c, l_ref=l_scr, l_on_mxu=True)

    def far_body(ki, carry):
        tile(pl.multiple_of(ki * t, t), None)
        return carry
    lax.fori_loop(0, jnp.maximum(qi - 1, 0), far_body, 0)

    @pl.when(qi >= 1)
    def _():
        tile(pl.multiple_of((qi - 1) * t, t), 1)
    tile(pl.multiple_of(qi * t, t), 0)

    lam = _diff_lambda(lam_ref, lam_init)
    for hh in range(DF_HEADS_PER_STEP):
        c0, c1 = 2 * hh, 2 * hh + 1
        o = (acc_scr[c0] / jnp.maximum(l_scr[c0], 1e-30)
             - lam * (acc_scr[c1] / jnp.maximum(l_scr[c1], 1e-30)))
        o = o * lax.rsqrt(jnp.mean(o * o, -1, keepdims=True) + EPS) * hg_ref[hh]
        o_ref[0, :, hh * DF_DV:(hh + 1) * DF_DV] = o * (1.0 - lam_init)


def diff_flash_prompt(q, kv, lam_p, head_g, rel_bias, lam_init):
    n, seq, _ = q.shape
    t = min(ATT_TILE, seq)
    nq = seq // t
    hps = DF_HEADS_PER_STEP
    width = hps * DF_DV
    assert t % LANES == 0 and _far_distance_ok() and t + 1 >= FAR_DISTANCE and DF_HEADS % hps == 0
    tiles = _toeplitz_tiles(rel_bias, t, 2, rel_to_far=True)
    return pl.pallas_call(
        functools.partial(_diff_flash_prompt_kernel, t=t, lam_init=lam_init),
        grid=(n, DF_HEADS // hps, nq),
        in_specs=[pl.BlockSpec((1, t, width), lambda b, h, qi: (b, qi, h)),
                  pl.BlockSpec((1, seq, width), lambda b, h, qi: (b, 0, h)),
                  pl.BlockSpec((1, seq, width), lambda b, h, qi: (b, 0, DF_HEADS // hps + h)),
                  pl.BlockSpec((2 * hps, 2, t, t), lambda b, h, qi: (h, 0, 0, 0)),
                  pl.BlockSpec((4, DF_DK), lambda b, h, qi: (0, 0)),
                  pl.BlockSpec((hps, 1, DF_DV), lambda b, h, qi: (h, 0, 0))],
        out_specs=pl.BlockSpec((1, t, width), lambda b, h, qi: (b, qi, h)),
        out_shape=jax.ShapeDtypeStruct((n, seq, D_MODEL), F32),
        scratch_shapes=[pltpu.VMEM((2 * hps, t, DF_DK), BF16), pltpu.VMEM((2 * hps, t, LANES), F32),
                        pltpu.VMEM((2 * hps, t, LANES), F32), pltpu.VMEM((2 * hps, t, DF_DV), F32)],
        compiler_params=_cparams("parallel", "parallel", "arbitrary"),
        name="diff_flash_prompt",
    )(q, kv, kv, tiles, lam_p, head_g.reshape(DF_HEADS, 1, DF_DV))


def _mlstm_kernel(*refs, c, has_init):
    if has_init:
        (q_ref, k_ref, v_ref, gc_ref, gr_ref, og_ref, bc_ref, br_ref, hg_ref, c0_ref, n0_ref, m0_ref,
         a_ref, cout_ref, nout_ref, mout_ref, c_scr, n_scr, m_scr) = refs
    else:
        (q_ref, k_ref, v_ref, gc_ref, gr_ref, og_ref, bc_ref, br_ref, hg_ref,
         a_ref, cout_ref, nout_ref, mout_ref, c_scr, n_scr, m_scr) = refs
    j = pl.program_id(1)

    @pl.when(j == 0)
    def _():
        if has_init:
            c_scr[...] = c0_ref[0]
            n_scr[...] = n0_ref[0]
            m_scr[...] = m0_ref[0]
        else:
            c_scr[...] = jnp.zeros(c_scr.shape, F32)
            n_scr[...] = jnp.zeros(n_scr.shape, F32)
            m_scr[...] = jnp.zeros(m_scr.shape, F32)

    row = lax.broadcasted_iota(jnp.int32, (c, c), 0)
    col = lax.broadcasted_iota(jnp.int32, (c, c), 1)
    tri = row >= col
    gc = gc_ref[0] + bc_ref[...]
    gr = gr_ref[0] + br_ref[...]
    f_cols = _dot_exact(jnp.where(tri, 1.0, 0.0), _log_sigmoid(gc))
    f_rows = _dot_exact(_log_sigmoid(gr), jnp.where(row <= col, 1.0, 0.0))
    og = og_ref[0]
    for h in range(ML_HEADS):
        qc = q_ref[0, :, h * ML_DK:(h + 1) * ML_DK]
        kc = k_ref[0, :, h * ML_DK:(h + 1) * ML_DK] * (ML_DK ** -0.5)
        vc = v_ref[0, :, h * ML_DV:(h + 1) * ML_DV]
        q_bf, k_bf = qc.astype(BF16), kc.astype(BF16)
        f_col = f_cols[:, ML_HEADS + h:ML_HEADS + h + 1]
        f_row = f_rows[ML_HEADS + h:ML_HEADS + h + 1, :]
        i_col = gc[:, h:h + 1]
        i_row = gr[h:h + 1, :]
        m_prev = m_scr[h:h + 1, 0:1]
        dm = jnp.where(tri, f_col - f_row + i_row, -1e30)
        inter = f_col + m_prev
        mt = jnp.maximum(jnp.max(dm, -1, keepdims=True), inter)
        w = jnp.where(tri, jnp.exp(dm - mt), 0.0)
        s = _dot_nt(q_bf, k_bf) * w
        a = jnp.exp(inter - mt)
        c_old = c_scr[h]
        n_old = n_scr[h:h + 1, :]
        num = _dot(s.astype(BF16), vc.astype(BF16)) + a * _dot_nt(q_bf, c_old.astype(BF16))
        den = jnp.sum(s, -1, keepdims=True) + a * jnp.sum(qc * n_old, -1, keepdims=True)
        hh = num / jnp.maximum(jnp.abs(den), jnp.exp(-mt))
        m_new = mt[c - 1:c, :]
        wend = jnp.exp(f_col[c - 1:c, :] - f_col + i_col - m_new)
        decay = jnp.exp(inter[c - 1:c, :] - m_new)
        c_scr[h] = decay * c_old + _dot_tn((wend * vc).astype(BF16), k_bf)
        n_scr[h:h + 1, :] = decay * n_old + jnp.sum(wend * kc, 0, keepdims=True)
        m_scr[h:h + 1, :] = jnp.broadcast_to(m_new, (1, LANES))
        hn = hh * lax.rsqrt(jnp.mean(hh * hh, -1, keepdims=True) + EPS) * hg_ref[h:h + 1, :]
        a_ref[0, :, h * ML_DV:(h + 1) * ML_DV] = _sigmoid(og[:, h * ML_DV:(h + 1) * ML_DV]) * hn

    @pl.when(j == pl.num_programs(1) - 1)
    def _():
        cout_ref[0] = c_scr[...]
        nout_ref[0] = n_scr[...]
        mout_ref[0] = m_scr[...]


def mlstm_cell(q, k, v, gates, og, b_if, head_g, state):
    n, seq, _ = q.shape
    c = min(LANES, seq)
    nc = seq // c
    has_init = state is not None
    gates_t = jnp.swapaxes(gates[:, :, :2 * ML_HEADS], 1, 2)
    b_flat = b_if.reshape(2 * ML_HEADS)
    b_col = jnp.pad(b_flat, (0, LANES - 2 * ML_HEADS)).reshape(1, LANES)
    b_row = b_flat.reshape(2 * ML_HEADS, 1)
    tok = lambda w: pl.BlockSpec((1, c, w), lambda b, j: (b, j, 0))
    fix2 = lambda s: pl.BlockSpec(s, lambda b, j: (0, 0))
    in_specs = [tok(ML_QK), tok(ML_QK), tok(D_MODEL), tok(LANES),
                pl.BlockSpec((1, 2 * ML_HEADS, c), lambda b, j: (b, 0, j)), tok(D_MODEL),
                fix2((1, LANES)), fix2((2 * ML_HEADS, 1)), fix2((ML_HEADS, ML_DV))]
    args = [q, k, v, gates, gates_t, og, b_col, b_row, head_g]
    st4 = pl.BlockSpec((1, ML_HEADS, ML_DV, ML_DK), lambda b, j: (b, 0, 0, 0))
    st3 = pl.BlockSpec((1, ML_HEADS, LANES), lambda b, j: (b, 0, 0))
    if has_init:
        c0, n0, m0 = state
        in_specs += [st4, st3, st3]
        args += [c0, n0, jnp.broadcast_to(m0[:, :, None], (n, ML_HEADS, LANES))]
    a, c_new, n_new, m_new = pl.pallas_call(
        functools.partial(_mlstm_kernel, c=c, has_init=has_init),
        grid=(n, nc),
        in_specs=in_specs,
        out_specs=[tok(D_MODEL), st4, st3, st3],
        out_shape=[jax.ShapeDtypeStruct((n, seq, D_MODEL), F32),
                   jax.ShapeDtypeStruct((n, ML_HEADS, ML_DV, ML_DK), F32),
                   jax.ShapeDtypeStruct((n, ML_HEADS, ML_DK), F32),
                   jax.ShapeDtypeStruct((n, ML_HEADS, LANES), F32)],
        scratch_shapes=[pltpu.VMEM((ML_HEADS, ML_DV, ML_DK), F32), pltpu.VMEM((ML_HEADS, ML_DK), F32),
                        pltpu.VMEM((ML_HEADS, LANES), F32)],
        compiler_params=_cparams("parallel", "arbitrary"),
        name="mlstm",
    )(*args)
    return a, c_new, n_new, m_new[:, :, 0]


def _conv_kernel(a_ref, b_ref, buf_ref, w_ref, dwb_ref, lng_ref, lnb_ref, o_ref, nbuf_ref, ctx_scr, *, tt):
    j = pl.program_id(1)

    @pl.when(j == 0)
    def _():
        ctx_scr[0:CONV_HALO, :] = buf_ref[0]

    @pl.when(j > 0)
    def _():
        ctx_scr[0:CONV_HALO, :] = ctx_scr[tt:tt + CONV_HALO, :]

    ctx_scr[CONV_HALO:CONV_HALO + tt, :] = a_ref[0] * _sigmoid(b_ref[0])
    off = CONV_HALO - (CONV_WIDTH - 1)
    acc = jnp.zeros((tt, D_MODEL), F32) + dwb_ref[...]
    sub = 8
    for r in range(sub):
        lead = off + r
        base, shift = (lead // sub) * sub, lead % sub
        rows = tt if shift == 0 else tt + sub
        part = None
        for tap in range(r, CONV_WIDTH, sub):
            lo = base + tap - r
            term = ctx_scr[lo:lo + rows, :] * w_ref[tap:tap + 1, :]
            part = term if part is None else part + term
        acc = acc + part[shift:shift + tt, :]
    mu = jnp.mean(acc, -1, keepdims=True)
    xc = acc - mu
    var = jnp.mean(xc * xc, -1, keepdims=True)
    o_ref[0] = _silu(xc * lax.rsqrt(var + EPS) * lng_ref[...] + lnb_ref[...])
    nbuf_ref[0] = ctx_scr[tt:tt + CONV_HALO, :]


def conv_block(a, b, buf, dw_w, dw_b, ln_g, ln_b):
    n, seq, d = a.shape
    tt = min(ATT_TILE, seq)
    assert seq == tt or tt >= CONV_HALO
    hist = CONV_WIDTH - 1
    if buf is None:
        buf_pad = jnp.zeros((n, CONV_HALO, d), F32)
    else:
        buf_pad = jnp.pad(buf, ((0, 0), (CONV_HALO - hist, 0), (0, 0)))
    w_pad = jnp.pad(dw_w, ((0, CONV_HALO - CONV_WIDTH), (0, 0)))
    tok = pl.BlockSpec((1, tt, d), lambda bb, j: (bb, j, 0))
    vec = pl.BlockSpec((1, d), lambda bb, j: (0, 0))
    halo = pl.BlockSpec((1, CONV_HALO, d), lambda bb, j: (bb, 0, 0))
    out, nbuf = pl.pallas_call(
        functools.partial(_conv_kernel, tt=tt),
        grid=(n, seq // tt),
        in_specs=[tok, tok, halo, pl.BlockSpec((CONV_HALO, d), lambda bb, j: (0, 0)), vec, vec, vec],
        out_specs=[tok, halo],
        out_shape=[jax.ShapeDtypeStruct((n, seq, d), F32), jax.ShapeDtypeStruct((n, CONV_HALO, d), F32)],
        scratch_shapes=[pltpu.VMEM((CONV_HALO + tt, d), F32)],
        compiler_params=_cparams("parallel", "arbitrary"),
        name="conv_block",
    )(a, b, buf_pad, w_pad, dw_b.reshape(1, d), ln_g.reshape(1, d), ln_b.reshape(1, d))
    return out, nbuf[:, CONV_HALO - hist:, :]


def _page_spec(block, col_block, j):
    return pl.BlockSpec(block, lambda b, pt: (pt[b, j], 0, col_block))


def _page_means_kernel(pt_ref, *refs):
    page_refs, o_ref = refs[:-1], refs[-1]
    for j, page_ref in enumerate(page_refs):
        x = page_ref[0]
        blocks = x.shape[0] // NSA_BLOCK
        o_ref[0, j * blocks:(j + 1) * blocks, :] = (
            jnp.sum(x.reshape(blocks, NSA_BLOCK, x.shape[1]), axis=1) * (1.0 / NSA_BLOCK))


def nsa_page_means(pool, page_table):
    n, n_pages = page_table.shape
    page = pool.shape[1]
    nbp = n_pages * page // NSA_BLOCK
    return pl.pallas_call(
        _page_means_kernel,
        grid_spec=pltpu.PrefetchScalarGridSpec(
            num_scalar_prefetch=1, grid=(n,),
            in_specs=[_page_spec((1, page, 2 * NSA_KVW), 0, j) for j in range(n_pages)],
            out_specs=pl.BlockSpec((1, nbp, 2 * NSA_KVW), lambda b, pt: (b, 0, 0))),
        out_shape=jax.ShapeDtypeStruct((n, nbp, 2 * NSA_KVW), F32),
        compiler_params=_cparams("parallel"),
        name="nsa_page_means",
    )(page_table, *([pool] * n_pages))


def _stack_group_queries(q, kh):
    parts = [q[:, (kh * NSA_GROUP + g) * NSA_DH:(kh * NSA_GROUP + g + 1) * NSA_DH] for g in range(NSA_GROUP)]
    return (jnp.concatenate(parts, axis=0) * (NSA_DH ** -0.5)).astype(BF16)


def _nsa_sample_local_kernel(q_ref, means_ref, rows_ref, win_ref, neww_ref, g_ref, bc_ref, bw_ref,
                             oc_ref, ow_ref, selb_ref, wout_ref, mk_scr, *, past, seq, k_sel):
    nbp = means_ref.shape[1]
    wb = win_ref.shape[1]
    mk_scr[...] = jnp.zeros(mk_scr.shape, F32)
    mk_scr[0:nbp, :] = means_ref[0]
    mk_scr[nbp:nbp + 1, :] = jnp.sum(rows_ref[0][:, 0:2 * NSA_KVW], 0, keepdims=True) * (1.0 / NSA_BLOCK)
    neww = neww_ref[0]
    wout_ref[0, 0:wb - seq, :] = win_ref[0, seq:wb, :]
    wout_ref[0, wb - seq:wb, :] = neww
    new_pad = jnp.concatenate([neww, jnp.zeros((LANES - seq, 2 * NSA_KVW), F32)], axis=0)
    q = q_ref[0]
    gate = _sigmoid(g_ref[0])
    imps = []
    for kh in range(NSA_KV_HEADS):
        qs = _stack_group_queries(q, kh)
        lo, hi = kh * NSA_DH, (kh + 1) * NSA_DH
        p = _softmax_rows(_dot_nt(qs, mk_scr[:, lo:hi].astype(BF16)) + bc_ref[kh])
        oc = _dot(p.astype(BF16), mk_scr[:, NSA_KVW + lo:NSA_KVW + hi].astype(BF16))
        imp = p[0:seq]
        for g in range(1, NSA_GROUP):
            imp = imp + p[g * seq:(g + 1) * seq]
        imps.append(imp)
        s_old = _dot_nt(qs, win_ref[0, :, lo:hi].astype(BF16)) + bw_ref[kh, :, 0:wb]
        s_new = _dot_nt(qs, new_pad[:, lo:hi].astype(BF16)) + bw_ref[kh, :, wb:wb + LANES]
        mw = jnp.maximum(jnp.maximum(jnp.max(s_old, -1, keepdims=True), jnp.max(s_new, -1, keepdims=True)), M_FLOOR)
        p_old, p_new = jnp.exp(s_old - mw), jnp.exp(s_new - mw)
        lw = jnp.sum(p_old, -1, keepdims=True) + jnp.sum(p_new, -1, keepdims=True)
        ow = (_dot(p_old.astype(BF16), win_ref[0, :, NSA_KVW + lo:NSA_KVW + hi].astype(BF16))
              + _dot(p_new.astype(BF16), new_pad[:, NSA_KVW + lo:NSA_KVW + hi].astype(BF16))) / jnp.maximum(lw, 1e-30)
        for g in range(NSA_GROUP):
            h = kh * NSA_GROUP + g
            oc_ref[0, :, h * NSA_DH:(h + 1) * NSA_DH] = oc[g * seq:(g + 1) * seq] * gate[:, h:h + 1]
            ow_ref[0, :, h * NSA_DH:(h + 1) * NSA_DH] = (
                ow[g * seq:(g + 1) * seq] * gate[:, 2 * NSA_HEADS + h:2 * NSA_HEADS + h + 1])
    rows_sel = NSA_KV_HEADS * seq
    imp_all = jnp.concatenate(imps + [jnp.zeros((LANES - rows_sel, LANES), F32)], axis=0)
    pos = past + lax.broadcasted_iota(jnp.int32, (1, LANES), 1) % seq
    sel = _select_blocks(imp_all, (pos // NSA_BLOCK).astype(F32), k_sel)
    for kh in range(NSA_KV_HEADS):
        selb_ref[0, kh] = sel[kh * seq:(kh + 1) * seq]


def _group_rows(tab):
    return tab.reshape(NSA_KV_HEADS, NSA_GROUP * tab.shape[1], tab.shape[2])


def nsa_sample_local(q, means, rows, win_buf, new_win, gates, rel_bias, past):
    n, seq, _ = q.shape
    nbp = means.shape[1]
    wb = win_buf.shape[1]
    nb = -(-(past + seq) // NSA_BLOCK)
    assert nbp + 1 == nb and nb <= LANES and seq % 8 == 0 and NSA_KV_HEADS * seq <= LANES
    lw = wb + LANES
    assert wb % LANES == 0 and seq <= LANES
    qpos = past + jnp.arange(seq)[:, None]
    bend = jnp.arange(LANES)[None, :] * NSA_BLOCK + NSA_BLOCK - 1
    bc = _group_rows(_bias_table(rel_bias, qpos - bend, bend <= qpos))
    cidx = jnp.arange(lw)[None, :]
    wpos = past - wb + cidx
    dw = qpos - wpos
    bw = _group_rows(_bias_table(rel_bias, dw, (dw >= 0) & (dw <= NSA_WINDOW) & (wpos >= 0) & (cidx < wb + seq)))
    rows_q = NSA_GROUP * seq
    per_b = lambda s: pl.BlockSpec((1,) + s, lambda b: (b,) + (0,) * len(s))
    fix = lambda s: pl.BlockSpec(s, lambda b: (0,) * len(s))
    return pl.pallas_call(
        functools.partial(_nsa_sample_local_kernel, past=past, seq=seq, k_sel=min(NSA_TOPK, nb)),
        grid=(n,),
        in_specs=[per_b((seq, D_MODEL)), per_b((nbp, 2 * NSA_KVW)), per_b((seq, D_MODEL)), per_b((wb, 2 * NSA_KVW)),
                  per_b((seq, 2 * NSA_KVW)), per_b((seq, LANES)), fix((NSA_KV_HEADS, rows_q, LANES)),
                  fix((NSA_KV_HEADS, rows_q, lw))],
        out_specs=[per_b((seq, D_MODEL)), per_b((seq, D_MODEL)), per_b((NSA_KV_HEADS, seq, LANES)),
                   per_b((wb, 2 * NSA_KVW))],
        out_shape=[jax.ShapeDtypeStruct((n, seq, D_MODEL), F32), jax.ShapeDtypeStruct((n, seq, D_MODEL), F32),
                   jax.ShapeDtypeStruct((n, NSA_KV_HEADS, seq, LANES), F32),
                   jax.ShapeDtypeStruct((n, wb, 2 * NSA_KVW), F32)],
        scratch_shapes=[pltpu.VMEM((LANES, 2 * NSA_KVW), F32)],
        compiler_params=_cparams("parallel"),
        name="nsa_sample_local",
    )(q, means, rows, win_buf, new_win, gates, bc, bw)


def _nsa_sample_selected_kernel(pt_ref, q_ref, rows_ref, selb_ref, bias_ref, g_ref, *refs, seq, page, past):
    n_pages = len(refs) - 2
    page_refs, o_ref, new_scr = refs[:n_pages], refs[n_pages], refs[n_pages + 1]
    new_scr[...] = jnp.zeros(new_scr.shape, F32)
    new_scr[0:seq, :] = rows_ref[0][:, 2 * NSA_KVW:4 * NSA_KVW]
    q = q_ref[0]
    gate = _sigmoid(g_ref[0])
    sources = [(page_refs[j], j * page) for j in range(n_pages)] + [(None, past)]
    expands = [_block_expand(first_pos, page) for _, first_pos in sources]
    for kh in range(NSA_KV_HEADS):
        lo, hi = kh * NSA_DH, (kh + 1) * NSA_DH
        qs = _stack_group_queries(q, kh)
        sel = selb_ref[0, kh].astype(BF16)
        scores = []
        for j, (ref, _) in enumerate(sources):
            k_bf = (new_scr[:, lo:hi] if ref is None else ref[0, :, lo:hi]).astype(BF16)
            sb = _dot(sel, expands[j])
            scores.append(_dot_nt(qs, k_bf) + bias_ref[j, kh] + jnp.concatenate([sb] * NSA_GROUP, axis=0))
        mx = scores[0]
        for s in scores[1:]:
            mx = jnp.maximum(mx, s)
        m = jnp.maximum(jnp.max(mx, -1, keepdims=True), M_FLOOR)
        total = None
        acc = None
        for j, (ref, _) in enumerate(sources):
            p = jnp.exp(scores[j] - m)
            v_bf = (new_scr[:, NSA_KVW + lo:NSA_KVW + hi] if ref is None
                    else ref[0, :, NSA_KVW + lo:NSA_KVW + hi]).astype(BF16)
            pv = _dot(p.astype(BF16), v_bf)
            total = p if total is None else total + p
            acc = pv if acc is None else acc + pv
        o = acc / jnp.maximum(jnp.sum(total, -1, keepdims=True), 1e-30)
        for g in range(NSA_GROUP):
            h = kh * NSA_GROUP + g
            o_ref[0, :, h * NSA_DH:(h + 1) * NSA_DH] = (
                o[g * seq:(g + 1) * seq] * gate[:, NSA_HEADS + h:NSA_HEADS + h + 1])


def nsa_sample_selected(q, pool, page_table, rows, selb, gates, rel_bias, past):
    n, seq, _ = q.shape
    n_pages = page_table.shape[1]
    page = pool.shape[1]
    assert page == LANES and seq <= page
    qpos = past + jnp.arange(seq)[:, None]
    j = jnp.arange(page)[None, :]
    d_past = qpos[None] - (jnp.arange(n_pages)[:, None, None] * page + j[None])
    d_new = (qpos - past) - j
    dist = jnp.concatenate([d_past, d_new[None]], 0)
    valid = jnp.concatenate([d_past >= 0, ((d_new >= 0) & (j < seq))[None]], 0)
    bias = _bias_table(rel_bias, dist, valid)
    bias = jnp.moveaxis(bias, 1, 0).reshape(n_pages + 1, NSA_KV_HEADS, NSA_GROUP * seq, page)
    rows_q = NSA_GROUP * seq
    per_b = lambda s: pl.BlockSpec((1,) + s, lambda b, pt: (b,) + (0,) * len(s))
    return pl.pallas_call(
        functools.partial(_nsa_sample_selected_kernel, seq=seq, page=page, past=past),
        grid_spec=pltpu.PrefetchScalarGridSpec(
            num_scalar_prefetch=1, grid=(n,),
            in_specs=[per_b((seq, D_MODEL)), per_b((seq, D_MODEL)), per_b((NSA_KV_HEADS, seq, LANES)),
                      pl.BlockSpec((n_pages + 1, NSA_KV_HEADS, rows_q, page), lambda b, pt: (0, 0, 0, 0)),
                      per_b((seq, LANES))]
            + [_page_spec((1, page, 2 * NSA_KVW), 1, jj) for jj in range(n_pages)],
            out_specs=per_b((seq, D_MODEL)),
            scratch_shapes=[pltpu.VMEM((page, 2 * NSA_KVW), F32)]),
        out_shape=jax.ShapeDtypeStruct((n, seq, D_MODEL), F32),
        compiler_params=_cparams("parallel"),
        name="nsa_sample_selected",
    )(page_table, q, rows, selb, bias, gates, *([pool] * n_pages))


def _diff_sample_kernel(pt_ref, q_ref, new_ref, tfar_ref, tnear_ref, tnew_ref, lam_ref, hg_ref, *refs,
                        seq, lam_init, pps, n_near):
    k_refs, v_refs = refs[:pps], refs[pps:2 * pps]
    o_ref, qall_scr, m_scr, l_scr, acc_scr = refs[2 * pps:]
    g = pl.program_id(1)
    last = pl.num_programs(1) - 1
    rows_k = k_refs[0].shape[0] * DF_HEADS
    pieces = rows_k // LANES

    @pl.when(g == 0)
    def _():
        m_scr[...] = jnp.full(m_scr.shape, M_FLOOR, F32)
        l_scr[...] = jnp.zeros(l_scr.shape, F32)
        acc_scr[...] = jnp.zeros(acc_scr.shape, F32)
        q = q_ref[0] * (DF_DK ** -0.5)
        half = lax.broadcasted_iota(jnp.int32, (seq, DF_DV), 1) // DF_DK
        parts = []
        for h in range(DF_HEADS):
            qh = q[:, h * DF_DV:(h + 1) * DF_DV]
            parts += [jnp.where(half == c, qh, 0.0) for c in range(2)]
        qall_scr[...] = jnp.concatenate(parts, axis=0).astype(BF16)

    qall = qall_scr[...]
    for jj in range(pps):
        kf = k_refs[jj][...].reshape(rows_k, DF_DV).astype(BF16)
        vf = v_refs[jj][...].reshape(rows_k, DF_DV).astype(BF16)
        s = _dot_nt(qall, kf)
        near = jj - (pps - n_near)
        ps = []
        for i in range(pieces):
            tab = tfar_ref[:, i * LANES:(i + 1) * LANES]
            if near >= 0:
                tab = jnp.where(g == last, tnear_ref[near, :, i * LANES:(i + 1) * LANES], tab)
            ps.append(s[:, i * LANES:(i + 1) * LANES] + tab)
        _flash_update(ps, vf, m_scr, acc_scr, slice(None), l_ref=l_scr)

    @pl.when(g == last)
    def _():
        new = new_ref[0]
        pad = jnp.zeros((LANES - DF_HEADS * seq, DF_DV), F32)
        kn = jnp.concatenate([new[:, h * DF_DV:(h + 1) * DF_DV] for h in range(DF_HEADS)] + [pad], axis=0)
        vn = jnp.concatenate([new[:, D_MODEL + h * DF_DV:D_MODEL + (h + 1) * DF_DV] for h in range(DF_HEADS)]
                             + [pad], axis=0)
        s = _dot_nt(qall, kn.astype(BF16)) + tnew_ref[...]
        _flash_update([s], vn.astype(BF16), m_scr, acc_scr, slice(None), l_ref=l_scr)
        lam = _diff_lambda(lam_ref, lam_init)
        acc = acc_scr[...] / jnp.maximum(l_scr[...], 1e-30)
        for h in range(DF_HEADS):
            r0 = 2 * h * seq
            o = acc[r0:r0 + seq] - lam * acc[r0 + seq:r0 + 2 * seq]
            o = o * lax.rsqrt(jnp.mean(o * o, -1, keepdims=True) + EPS) * hg_ref[:, h * DF_DV:(h + 1) * DF_DV]
            o_ref[0, :, h * DF_DV:(h + 1) * DF_DV] = o * (1.0 - lam_init)


def diff_sample(q, pool6, j_layer, page_table, kv_new, lam_p, head_g, rel_bias, past, lam_init):
    n, seq, _ = q.shape
    n_pages = page_table.shape[1]
    page = pool6.shape[2]
    pps = min(DF_PAGES_PER_STEP, n_pages)
    rows_q = 2 * DF_HEADS * seq
    rows_k = page * DF_HEADS
    assert n_pages % pps == 0 and rows_q == LANES and DF_HEADS * seq <= LANES and _far_distance_ok()
    n_near = sum(1 for p in range(n_pages) if past - (p * page + page - 1) < FAR_DISTANCE)
    assert 1 <= n_near <= pps
    head_of_row = jnp.arange(rows_q) // (2 * seq)
    same_head = head_of_row[:, None] == (jnp.arange(rows_k) % DF_HEADS)[None, :]
    far = jnp.repeat(rel_bias[N_BUCKETS - 1], seq)[:, None]
    t_far = jnp.where(same_head, far, MASKED).astype(F32)
    qpos = past + jnp.arange(seq)[:, None]
    near_pages = jnp.arange(n_pages - n_near, n_pages)
    d_near = qpos[None] - (near_pages[:, None, None] * page + jnp.arange(page)[None, None, :])
    b_near = _bias_table(rel_bias, d_near, d_near >= 0)
    b_near = jnp.moveaxis(b_near, 1, 0).reshape(n_near, rows_q, page)
    b_near = jnp.broadcast_to(b_near[..., None], (n_near, rows_q, page, DF_HEADS)).reshape(n_near, rows_q, rows_k)
    t_near = jnp.where(same_head[None], b_near, MASKED).astype(F32)
    d_new = jnp.arange(seq)[:, None] - jnp.arange(seq)[None, :]
    b_new = _bias_table(rel_bias, d_new, d_new >= 0).reshape(rows_q, 1, seq)
    b_new = jnp.broadcast_to(b_new, (rows_q, DF_HEADS, seq)).reshape(rows_q, DF_HEADS * seq)
    same_new = head_of_row[:, None] == (jnp.arange(DF_HEADS * seq) // seq)[None, :]
    t_new = jnp.pad(jnp.where(same_new, b_new, MASKED), ((0, 0), (0, LANES - DF_HEADS * seq)),
                    constant_values=MASKED).astype(F32)

    def page_spec(jj, c):
        return pl.BlockSpec((None, None, page, None, DF_HEADS, DF_DV),
                            lambda b, g, pt: (j_layer, pt[b, g * pps + jj], 0, c, 0, 0))

    fix = lambda s: pl.BlockSpec(s, lambda b, g, pt: (0,) * len(s))
    return pl.pallas_call(
        functools.partial(_diff_sample_kernel, seq=seq, lam_init=lam_init, pps=pps, n_near=n_near),
        grid_spec=pltpu.PrefetchScalarGridSpec(
            num_scalar_prefetch=1, grid=(n, n_pages // pps),
            in_specs=[pl.BlockSpec((1, seq, D_MODEL), lambda b, g, pt: (b, 0, 0)),
                      pl.BlockSpec((1, seq, 2 * D_MODEL), lambda b, g, pt: (b, 0, 0)),
                      fix((rows_q, rows_k)), fix((n_near, rows_q, rows_k)), fix((rows_q, LANES)),
                      fix((4, DF_DK)), fix((1, D_MODEL))]
            + [page_spec(jj, 0) for jj in range(pps)] + [page_spec(jj, 1) for jj in range(pps)],
            out_specs=pl.BlockSpec((1, seq, D_MODEL), lambda b, g, pt: (b, 0, 0)),
            scratch_shapes=[pltpu.VMEM((rows_q, DF_DV), BF16), pltpu.VMEM((rows_q, LANES), F32),
                            pltpu.VMEM((rows_q, LANES), F32), pltpu.VMEM((rows_q, DF_DV), F32)]),
        out_shape=jax.ShapeDtypeStruct((n, seq, D_MODEL), F32),
        compiler_params=_cparams("parallel", "arbitrary"),
        name="diff_sample",
    )(page_table, q, kv_new, t_far, t_near, t_new, lam_p, head_g.reshape(1, D_MODEL), *([pool6] * (2 * pps)))


def _trunk(x, p, past, page_table, nsa_pool, nsa_win, ml_c, ml_n, ml_m, conv_buf, diff_pool, weights):
    (norm_g, final_norm_g, rel_bias, ple_w, ple_gate_w, ple_norm_g,
     nsa_w_in, nsa_w_out, mlstm_w_in, mlstm_b_if, mlstm_head_g, mlstm_w_out,
     conv_w_in, conv_dw_w, conv_dw_b, conv_ln_g, conv_ln_b, conv_w_out,
     diff_w_in, diff_lam, diff_head_g, diff_w_out) = weights
    n, seq, d = x.shape
    m = n * seq
    sample = page_table is not None
    x2 = x.reshape(m, d)
    depth = norm_g.shape[0]
    out = {}
    for i in range(depth):
        kind, j = i % 4, i // 4
        assert j == 0
        p_i = p[i].reshape(m, D_PLE)
        tail = dict(x=x2, p=p_i, ple_norm_g=ple_norm_g[i], ple_gate_w=ple_gate_w[i], ple_w=ple_w[i],
                    final_g=final_norm_g if i == depth - 1 else None)
        if kind == 0:
            w = nsa_w_in[j]
            c0, c1, c2, c3 = D_MODEL, 2 * D_MODEL, 2 * D_MODEL + 2 * NSA_KVW, 2 * D_MODEL + 2 * NSA_KVW + 3 * NSA_HEADS
            q, rows, win, gates, z = norm_linear(x2, norm_g[i], [w[:, :c0], w[:, c0:c1], w[:, c1:c2], w[:, c2:c3], w[:, c3:]])
            q3, rows3, win3, gates3 = (t.reshape(n, seq, -1) for t in (q, rows, win, gates))
            if sample:
                pool = nsa_pool[j].reshape(nsa_pool.shape[1], nsa_pool.shape[2], D_MODEL)
                wb = nsa_win.shape[2]
                means = nsa_page_means(pool, page_table)
                oc, ow, selb, win_out = nsa_sample_local(q3, means, rows3, nsa_win[j].reshape(n, wb, 2 * NSA_KVW),
                                                         win3, gates3, rel_bias, past)
                os_ = nsa_sample_selected(q3, pool, page_table, rows3, selb, gates3, rel_bias, past)
            else:
                oc, selb = nsa_cmp_prompt(q3, rows3, gates3, rel_bias)
                os_ = nsa_flash_prompt(q3, rows3, 4, 6, gates3, rel_bias, selb, window=False)
                ow = nsa_flash_prompt(q3, win3, 0, 2, gates3, rel_bias, None, window=True)
                keep = min(NSA_WINDOW, seq)
                win_out = win3[:, seq - keep:]
                if keep < NSA_WINDOW:
                    win_out = jnp.concatenate([jnp.zeros((n, NSA_WINDOW - keep, 2 * NSA_KVW), F32), win_out], 1)
            out['kv'] = rows3.reshape(1, n, seq, 4, NSA_KV_HEADS, NSA_DH)
            out['win'] = win_out.reshape(1, n, win_out.shape[1], 2, NSA_KV_HEADS, NSA_DH)
            x2 = post_layer([t.reshape(m, d) for t in (oc, os_, ow)], z, w_out=nsa_w_out[j], **tail)
        elif kind == 1:
            w = mlstm_w_in[j]
            c0, c1, c2 = ML_QK, 2 * ML_QK, 2 * ML_QK + D_MODEL
            c3, c4 = c2 + 2 * ML_HEADS, c2 + 2 * ML_HEADS + D_MODEL
            q, k, v, gates, og, z = norm_linear(
                x2, norm_g[i], [w[:, :c0], w[:, c0:c1], w[:, c1:c2], w[:, c2:c3], w[:, c3:c4], w[:, c4:]])
            state = (ml_c[j], ml_n[j], ml_m[j]) if sample else None
            a, c_new, n_new, m_new = mlstm_cell(*(t.reshape(n, seq, -1) for t in (q, k, v, gates, og)),
                                                mlstm_b_if[j], mlstm_head_g[j], state)
            out['C'], out['n'], out['m'] = c_new[None], n_new[None], m_new[None]
            x2 = post_layer([a.reshape(m, d)], z, w_out=mlstm_w_out[j], **tail)
        elif kind == 2:
            w = conv_w_in[j]
            a, b, z = norm_linear(x2, norm_g[i], [w[:, :d], w[:, d:2 * d], w[:, 2 * d:]])
            cact, nbuf = conv_block(a.reshape(n, seq, d), b.reshape(n, seq, d), conv_buf[j] if sample else None,
                                    conv_dw_w[j], conv_dw_b[j], conv_ln_g[j], conv_ln_b[j])
            out['conv'] = nbuf[None]
            x2 = post_layer([cact.reshape(m, d)], z, w_out=conv_w_out[j], **tail)
        else:
            w = diff_w_in[j]
            q, kv, z = norm_linear(x2, norm_g[i], [w[:, :d], w[:, d:3 * d], w[:, 3 * d:]])
            lam_init = 0.8 - 0.6 * math.exp(-0.3 * i)
            q3, kv3 = q.reshape(n, seq, d), kv.reshape(n, seq, 2 * d)
            if sample:
                a = diff_sample(q3, diff_pool, j, page_table, kv3, diff_lam[j], diff_head_g[j], rel_bias, past,
                                lam_init)
            else:
                a = diff_flash_prompt(q3, kv3, diff_lam[j], diff_head_g[j], rel_bias, lam_init)
            out['diff'] = kv3.reshape(1, n, seq, 2, DF_HEADS, DF_DV)
            x2 = post_layer([a.reshape(m, d)], z, w_out=diff_w_out[j], **tail)
    return x2.reshape(n, seq, d), out


def kernel(x_prompt, x_sample, cache_nsa_kv, state_nsa_win, state_mlstm_C, state_mlstm_n, state_mlstm_m,
           state_conv, cache_diff_kv, page_table, p_prompt, p_sample, norm_g, final_norm_g, rel_bias,
           ple_w, ple_gate_w, ple_norm_g, nsa_w_in, nsa_w_out, mlstm_w_in, mlstm_b_if, mlstm_head_g,
           mlstm_w_out, conv_w_in, conv_dw_w, conv_dw_b, conv_ln_g, conv_ln_b, conv_w_out, diff_w_in,
           diff_lam, diff_head_g, diff_w_out):
    weights = (norm_g, final_norm_g, rel_bias, ple_w, ple_gate_w, ple_norm_g,
               nsa_w_in, nsa_w_out, mlstm_w_in, mlstm_b_if, mlstm_head_g, mlstm_w_out,
               conv_w_in, conv_dw_w, conv_dw_b, conv_ln_g, conv_ln_b, conv_w_out,
               diff_w_in, diff_lam, diff_head_g, diff_w_out)
    past = page_table.shape[1] * cache_nsa_kv.shape[2]
    y_p, o_p = _trunk(x_prompt, p_prompt, 0, None, None, None, None, None, None, None, None, weights)
    y_s, o_s = _trunk(x_sample, p_sample, past, page_table, cache_nsa_kv, state_nsa_win, state_mlstm_C,
                      state_mlstm_n, state_mlstm_m, state_conv, cache_diff_kv, weights)
    return (y_p, y_s, o_p['kv'], o_s['kv'], o_p['win'], o_s['win'], o_p['C'], o_s['C'], o_p['n'], o_s['n'],
            o_p['m'], o_s['m'], o_p['conv'], o_s['conv'], o_p['diff'], o_s['diff'])
```

```python
import functools
import math

import jax
import jax.numpy as jnp
import numpy as np
from jax import lax
from jax.experimental import pallas as pl
from jax.experimental.pallas import tpu as pltpu

F32 = jnp.float32
BF16 = jnp.bfloat16

D_MODEL = 1024
D_PLE = 256
N_BUCKETS = 32
MAX_DISTANCE = 128
EPS = 1e-6

NSA_HEADS = 16
NSA_KV_HEADS = 4
NSA_GROUP = 4
NSA_DH = 64
NSA_BLOCK = 32
NSA_TOPK = 16
NSA_WINDOW = 512
NSA_KVW = NSA_KV_HEADS * NSA_DH

ML_HEADS = 4
ML_DV = 256
ML_DK = 128
ML_QK = ML_HEADS * ML_DK

CONV_WIDTH = 31
CONV_HALO = 32

DF_HEADS = 8
DF_DK = 64
DF_DV = 128
DF_PAGES_PER_STEP = 16
DF_ROW_SPLIT = 1
DF_HEADS_PER_STEP = 2

LANES = 128
MASKED = -2e30
M_FLOOR = -1e30
FAR_DISTANCE = 113
VMEM_LIMIT = 48 * 1024 * 1024
ROW_TILE = 512
ATT_TILE = 256
COL_CHUNK = 512


def _cparams(*sem):
    return pltpu.CompilerParams(dimension_semantics=sem, vmem_limit_bytes=VMEM_LIMIT)


def _dot(a, b):
    return jnp.dot(a, b, preferred_element_type=F32)


def _dot_nt(a, b):
    return lax.dot_general(a, b, (((1,), (1,)), ((), ())), preferred_element_type=F32)


def _dot_tn(a, b):
    return lax.dot_general(a, b, (((0,), (0,)), ((), ())), preferred_element_type=F32)


def _dot_exact(a, b):
    return jnp.dot(a, b, preferred_element_type=F32, precision=lax.Precision.HIGHEST)


def _sigmoid(x):
    return 1.0 / (1.0 + jnp.exp(-x))


def _silu(x):
    return x * _sigmoid(x)


def _log_sigmoid(x):
    return jnp.minimum(x, 0.0) - jnp.log1p(jnp.exp(-jnp.abs(x)))


def _pad_cols(w, mult=LANES):
    pad = (-w.shape[-1]) % mult
    return w if pad == 0 else jnp.pad(w, [(0, 0)] * (w.ndim - 1) + [(0, pad)])


def _norm_linear_kernel(x_ref, g_ref, *refs, widths):
    w_refs, o_refs = refs[:len(widths)], refs[len(widths):]
    x = x_ref[...]
    xn = (x * lax.rsqrt(jnp.mean(x * x, -1, keepdims=True) + EPS) * g_ref[...]).astype(BF16)
    for w_ref, o_ref, width in zip(w_refs, o_refs, widths):
        for c0 in range(0, width, COL_CHUNK):
            c1 = min(c0 + COL_CHUNK, width)
            o_ref[:, c0:c1] = _dot(xn, w_ref[:, c0:c1])


def norm_linear(x2d, g, weights):
    m, d = x2d.shape
    ws = [_pad_cols(w).astype(BF16) for w in weights]
    widths = tuple(w.shape[1] for w in ws)
    tm = min(ROW_TILE, m)
    assert m % tm == 0
    outs = pl.pallas_call(
        functools.partial(_norm_linear_kernel, widths=widths),
        grid=(m // tm,),
        in_specs=[pl.BlockSpec((tm, d), lambda i: (i, 0)), pl.BlockSpec((1, d), lambda i: (0, 0))]
        + [pl.BlockSpec((d, wd), lambda i: (0, 0)) for wd in widths],
        out_specs=[pl.BlockSpec((tm, wd), lambda i: (i, 0)) for wd in widths],
        out_shape=[jax.ShapeDtypeStruct((m, wd), F32) for wd in widths],
        compiler_params=_cparams("parallel"),
        name="norm_linear",
    )(x2d, g.reshape(1, d), *ws)
    return outs


def _post_kernel(*refs, n_a, final):
    a_refs = refs[:n_a]
    z_ref, x_ref, p_ref, wo_ref, png_ref, wg_ref, wp_ref = refs[n_a:n_a + 7]
    rest = refs[n_a + 7:]
    fng_ref = rest[0] if final else None
    o_ref = rest[-1]
    a = a_refs[0][...]
    for r in a_refs[1:]:
        a = a + r[...]
    h = (a * _silu(z_ref[...])).astype(BF16)
    x1 = x_ref[...] + _dot(h, wo_ref[...])
    r = (x1 * lax.rsqrt(jnp.mean(x1 * x1, -1, keepdims=True) + EPS) * png_ref[...]).astype(BF16)
    gate = _sigmoid(_dot(r, wg_ref[...]))
    x2 = x1 + gate * _dot(p_ref[...].astype(BF16), wp_ref[...])
    if final:
        x2 = x2 * lax.rsqrt(jnp.mean(x2 * x2, -1, keepdims=True) + EPS) * fng_ref[...]
    o_ref[...] = x2


def post_layer(a_list, z, x, p, w_out, ple_norm_g, ple_gate_w, ple_w, final_g=None):
    m, d = x.shape
    tm = min(ROW_TILE, m)
    assert m % tm == 0
    final = final_g is not None
    row = lambda i: (i, 0)
    fix = lambda i: (0, 0)
    in_specs = [pl.BlockSpec((tm, d), row) for _ in a_list] + [
        pl.BlockSpec((tm, d), row), pl.BlockSpec((tm, d), row), pl.BlockSpec((tm, D_PLE), row),
        pl.BlockSpec((d, d), fix), pl.BlockSpec((1, d), fix), pl.BlockSpec((d, d), fix),
        pl.BlockSpec((D_PLE, d), fix)]
    args = list(a_list) + [z, x, p, w_out.astype(BF16), ple_norm_g.reshape(1, d), ple_gate_w.astype(BF16),
                           ple_w.astype(BF16)]
    if final:
        in_specs.append(pl.BlockSpec((1, d), fix))
        args.append(final_g.reshape(1, d))
    return pl.pallas_call(
        functools.partial(_post_kernel, n_a=len(a_list), final=final),
        grid=(m // tm,),
        in_specs=in_specs,
        out_specs=pl.BlockSpec((tm, d), row),
        out_shape=jax.ShapeDtypeStruct((m, d), F32),
        compiler_params=_cparams("parallel"),
        name="post_layer",
    )(*args)


def _t5_bucket(dist):
    d = jnp.maximum(dist, 0)
    exact = N_BUCKETS // 2
    df = jnp.maximum(d, 1).astype(F32)
    large = exact + (jnp.log(df / exact) / math.log(MAX_DISTANCE / exact) * (N_BUCKETS - exact)).astype(jnp.int32)
    return jnp.where(d < exact, d, jnp.minimum(large, N_BUCKETS - 1))


def _bias_table(rel_bias, dist, valid, rel_to_far=False):
    bucket = _t5_bucket(dist)[None]
    col = lambda k: rel_bias[k].reshape((-1,) + (1,) * dist.ndim)
    b = jnp.broadcast_to(col(N_BUCKETS - 1), (rel_bias.shape[1],) + dist.shape)
    for k in range(N_BUCKETS - 1):
        b = jnp.where(bucket == k, col(k), b)
    if rel_to_far:
        b = b - col(N_BUCKETS - 1)
    return jnp.where(valid[None], b, MASKED).astype(F32)


def _toeplitz_tiles(rel_bias, t, n_delta, max_dist=None, rel_to_far=False):
    i = jnp.arange(t)
    d = jnp.arange(n_delta)[:, None, None] * t + i[None, :, None] - i[None, None, :]
    valid = d >= 0
    if max_dist is not None:
        valid = valid & (d <= max_dist)
    return _bias_table(rel_bias, d, valid, rel_to_far)


def _far_distance_ok():
    d = np.arange(FAR_DISTANCE, FAR_DISTANCE + 8192, dtype=np.float64)
    large = 16 + np.floor(np.log(d / 16) / math.log(MAX_DISTANCE / 16) * 16 - 1e-6)
    return bool(np.all(large >= N_BUCKETS - 1))


def _flash_update(pieces, v_bf, m_ref, acc_ref, idx, l_ref=None, l_on_mxu=False):
    mx = pieces[0]
    for pc in pieces[1:]:
        mx = jnp.maximum(mx, pc)
    m_old = m_ref[idx]
    m_new = jnp.maximum(m_old, jnp.max(mx, -1, keepdims=True))
    alpha = jnp.exp(m_old - m_new)
    p = [jnp.exp(pc - m_new) for pc in pieces]
    if l_ref is not None and not l_on_mxu:
        total = p[0]
        for pc in p[1:]:
            total = total + pc
        l_ref[idx] = alpha * l_ref[idx] + jnp.sum(total, -1, keepdims=True)
    p = (p[0] if len(p) == 1 else jnp.concatenate(p, axis=1)).astype(BF16)
    if l_ref is not None and l_on_mxu:
        l_ref[idx] = alpha * l_ref[idx] + _dot(p, jnp.ones((p.shape[1], LANES), BF16))
    acc_ref[idx] = alpha * acc_ref[idx] + _dot(p, v_bf)
    m_ref[idx] = m_new


def _flash_step(pieces, v_aug, m_old, l_old, acc_old):
    mx = pieces[0]
    for pc in pieces[1:]:
        mx = jnp.maximum(mx, pc)
    m_new = jnp.maximum(m_old, jnp.max(mx, -1, keepdims=True))
    alpha = jnp.exp(m_old - m_new)
    p = [jnp.exp(pc - m_new) for pc in pieces]
    p = (p[0] if len(p) == 1 else jnp.concatenate(p, axis=1)).astype(BF16)
    res = _dot(p, v_aug)
    dv = acc_old.shape[1]
    return [m_new, alpha * l_old + res[:, dv:], alpha * acc_old + res[:, :dv]]


def _softmax_rows(s):
    m = jnp.maximum(jnp.max(s, -1, keepdims=True), M_FLOOR)
    e = jnp.exp(s - m)
    return e / jnp.maximum(jnp.sum(e, -1, keepdims=True), 1e-30)


def _select_blocks(imp, cur, k):
    imp_t = jnp.transpose(imp)
    blk = lax.broadcasted_iota(jnp.int32, imp_t.shape, 0).astype(F32)
    sel = jnp.where(blk == cur, float(NSA_GROUP + 1), jnp.where(blk < cur, imp_t, -1.0))
    chosen = jnp.zeros(imp_t.shape, F32)
    for _ in range(k):
        top = jnp.max(sel, 0, keepdims=True)
        first = jnp.min(jnp.where(sel == top, blk, float(LANES)), 0, keepdims=True)
        hit = blk == first
        chosen = jnp.where(hit, 1.0, chosen)
        sel = jnp.where(hit, -3.0, sel)
    return jnp.transpose(jnp.where(chosen > 0.0, 0.0, MASKED))


def _block_expand(first_pos, width):
    blk = lax.broadcasted_iota(jnp.int32, (LANES, width), 0)
    key_blk = (first_pos + lax.broadcasted_iota(jnp.int32, (LANES, width), 1)) // NSA_BLOCK
    return jnp.where(blk == key_blk, 1.0, 0.0).astype(BF16)


def _nsa_cmp_prompt_kernel(q_ref, rows_ref, g_ref, bias_ref, oc_ref, selb_ref, mk_scr, *, seq, tq, k_sel):
    i = pl.program_id(1)
    nb = seq // NSA_BLOCK

    @pl.when(i == 0)
    def _():
        means = jnp.sum(rows_ref[0].reshape(nb, NSA_BLOCK, 2 * NSA_KVW), axis=1) * (1.0 / NSA_BLOCK)
        mk_scr[...] = jnp.zeros(mk_scr.shape, F32)
        mk_scr[0:nb, :] = means

    q = q_ref[0]
    gate = _sigmoid(g_ref[0])
    pos = i * tq + lax.broadcasted_iota(jnp.int32, (1, tq), 1)
    cur = (pos // NSA_BLOCK).astype(F32)
    for kh in range(NSA_KV_HEADS):
        mk = mk_scr[:, kh * NSA_DH:(kh + 1) * NSA_DH].astype(BF16)
        mv = mk_scr[:, NSA_KVW + kh * NSA_DH:NSA_KVW + (kh + 1) * NSA_DH].astype(BF16)
        imp = jnp.zeros((tq, LANES), F32)
        for g in range(NSA_GROUP):
            h = kh * NSA_GROUP + g
            qh = (q[:, h * NSA_DH:(h + 1) * NSA_DH] * (NSA_DH ** -0.5)).astype(BF16)
            p = _softmax_rows(_dot_nt(qh, mk) + bias_ref[h])
            imp = imp + p
            oc_ref[0, :, h * NSA_DH:(h + 1) * NSA_DH] = _dot(p.astype(BF16), mv) * gate[:, h:h + 1]
        selb_ref[0, kh] = _select_blocks(imp, cur, k_sel)


def nsa_cmp_prompt(q, rows, gates, rel_bias):
    n, seq, _ = q.shape
    tq = min(ATT_TILE, seq)
    nb = seq // NSA_BLOCK
    assert nb <= LANES
    t = jnp.arange(seq)[:, None]
    bend = jnp.arange(LANES)[None, :] * NSA_BLOCK + NSA_BLOCK - 1
    bias = _bias_table(rel_bias, t - bend, bend <= t)
    return pl.pallas_call(
        functools.partial(_nsa_cmp_prompt_kernel, seq=seq, tq=tq, k_sel=min(NSA_TOPK, nb)),
        grid=(n, seq // tq),
        in_specs=[pl.BlockSpec((1, tq, D_MODEL), lambda b, i: (b, i, 0)),
                  pl.BlockSpec((1, seq, 2 * NSA_KVW), lambda b, i: (b, 0, 0)),
                  pl.BlockSpec((1, tq, LANES), lambda b, i: (b, i, 0)),
                  pl.BlockSpec((NSA_HEADS, tq, LANES), lambda b, i: (0, i, 0))],
        out_specs=[pl.BlockSpec((1, tq, D_MODEL), lambda b, i: (b, i, 0)),
                   pl.BlockSpec((1, NSA_KV_HEADS, tq, LANES), lambda b, i: (b, 0, i, 0))],
        out_shape=[jax.ShapeDtypeStruct((n, seq, D_MODEL), F32),
                   jax.ShapeDtypeStruct((n, NSA_KV_HEADS, seq, LANES), F32)],
        scratch_shapes=[pltpu.VMEM((LANES, 2 * NSA_KVW), F32)],
        compiler_params=_cparams("parallel", "arbitrary"),
        name="nsa_cmp_prompt",
    )(q, rows, gates, bias)


def _nsa_flash_prompt_kernel(*refs, t, window, gate_col):
    if window:
        q_ref, k_ref, v_ref, bias_ref, g_ref, o_ref, qs_scr, m_scr, acc_scr = refs
        selb_ref = None
    else:
        q_ref, k_ref, v_ref, bias_ref, g_ref, selb_ref, o_ref, qs_scr, m_scr, acc_scr = refs
    u = pl.program_id(1)
    qi = pl.program_id(2)
    heads = 2 * NSA_GROUP
    pieces = t // LANES

    m_scr[...] = jnp.full(m_scr.shape, M_FLOOR, F32)
    acc_scr[...] = jnp.zeros(acc_scr.shape, F32)
    q = q_ref[0]
    for h in range(heads):
        qs_scr[h] = (q[:, h * NSA_DH:(h + 1) * NSA_DH] * (NSA_DH ** -0.5)).astype(BF16)
    lane = lax.broadcasted_iota(jnp.int32, (t, LANES), 1)

    def tile(start, delta):
        k = k_ref[0, pl.ds(start, t), :]
        v = v_ref[0, pl.ds(start, t), :]
        if not window:
            expand = _block_expand(start, t)
        for kv in range(2):
            k_bf = k[:, kv * NSA_DH:(kv + 1) * NSA_DH].astype(BF16)
            v_aug = jnp.where((lane < NSA_DH) if kv == 0 else (lane >= NSA_DH), v, 1.0).astype(BF16)
            if not window:
                sel_bias = _dot(selb_ref[0, kv].astype(BF16), expand)
            for g in range(NSA_GROUP):
                h = kv * NSA_GROUP + g
                s = _dot_nt(qs_scr[h], k_bf)
                ps = []
                for i in range(pieces):
                    pc = s[:, i * LANES:(i + 1) * LANES]
                    if delta is not None:
                        pc = pc + bias_ref[h, delta, :, i * LANES:(i + 1) * LANES]
                    if not window:
                        pc = pc + sel_bias[:, i * LANES:(i + 1) * LANES]
                    ps.append(pc)
                _flash_update(ps, v_aug, m_scr, acc_scr, h)

    if window:
        for delta in range(NSA_WINDOW // t, 0, -1):
            @pl.when(qi >= delta)
            def _(delta=delta):
                tile(pl.multiple_of((qi - delta) * t, t), delta)
    else:
        def far_body(ki, carry):
            tile(pl.multiple_of(ki * t, t), None)
            return carry
        lax.fori_loop(0, jnp.maximum(qi - 1, 0), far_body, 0)

        @pl.when(qi >= 1)
        def _():
            tile(pl.multiple_of((qi - 1) * t, t), 1)
    tile(pl.multiple_of(qi * t, t), 0)

    gate = _sigmoid(g_ref[0])
    for h in range(heads):
        col = gate_col + h
        g0 = jnp.where(u == 0, gate[:, col:col + 1], gate[:, col + heads:col + heads + 1])
        acc = acc_scr[h]
        lo, hi = acc[:, 0:NSA_DH], acc[:, NSA_DH:2 * NSA_DH]
        o, l = (lo, hi) if h < NSA_GROUP else (hi, lo)
        o_ref[0, :, h * NSA_DH:(h + 1) * NSA_DH] = o / jnp.maximum(l, 1e-30) * g0


def nsa_flash_prompt(q, kv_arr, k_blk, v_blk, gates, rel_bias, selb, window):
    n, seq, _ = q.shape
    t = min(ATT_TILE, seq)
    nq = seq // t
    heads = 2 * NSA_GROUP
    assert t % LANES == 0 and _far_distance_ok()
    if window:
        assert NSA_WINDOW % t == 0
        n_delta = NSA_WINDOW // t + 1
        tiles = _toeplitz_tiles(rel_bias, t, n_delta, NSA_WINDOW)
    else:
        assert t + 1 >= FAR_DISTANCE
        n_delta = 2
        tiles = _toeplitz_tiles(rel_bias, t, n_delta, rel_to_far=True)
    in_specs = [pl.BlockSpec((1, t, heads * NSA_DH), lambda b, u, qi: (b, qi, u)),
                pl.BlockSpec((1, seq, LANES), lambda b, u, qi: (b, 0, k_blk + u)),
                pl.BlockSpec((1, seq, LANES), lambda b, u, qi: (b, 0, v_blk + u)),
                pl.BlockSpec((heads, n_delta, t, t), lambda b, u, qi: (u, 0, 0, 0)),
                pl.BlockSpec((1, t, LANES), lambda b, u, qi: (b, qi, 0))]
    args = [q, kv_arr, kv_arr, tiles, gates]
    if not window:
        in_specs.append(pl.BlockSpec((1, 2, t, LANES), lambda b, u, qi: (b, u, qi, 0)))
        args.append(selb)
    return pl.pallas_call(
        functools.partial(_nsa_flash_prompt_kernel, t=t, window=window, gate_col=2 * NSA_HEADS if window else NSA_HEADS),
        grid=(n, 2, nq),
        in_specs=in_specs,
        out_specs=pl.BlockSpec((1, t, heads * NSA_DH), lambda b, u, qi: (b, qi, u)),
        out_shape=jax.ShapeDtypeStruct((n, seq, D_MODEL), F32),
        scratch_shapes=[pltpu.VMEM((heads, t, NSA_DH), BF16), pltpu.VMEM((heads, t, LANES), F32),
                        pltpu.VMEM((heads, t, LANES), F32)],
        compiler_params=_cparams("parallel", "parallel", "arbitrary"),
        name="nsa_window_prompt" if window else "nsa_selected_prompt",
    )(*args)


def _diff_lambda(lam_ref, lam_init):
    lp = lam_ref[...]
    s1 = jnp.sum(lp[0:1] * lp[1:2], -1, keepdims=True)
    s2 = jnp.sum(lp[2:3] * lp[3:4], -1, keepdims=True)
    return jnp.exp(s1) - jnp.exp(s2) + lam_init


def _diff_flash_prompt_kernel(q_ref, k_ref, v_ref, bias_ref, lam_ref, hg_ref, o_ref, qs_scr, *, t, lam_init):
    qi = pl.program_id(2)
    pieces = t // LANES
    maps = 2 * DF_HEADS_PER_STEP
    q = q_ref[0]
    for c in range(maps):
        qs_scr[c] = (q[:, c * DF_DK:(c + 1) * DF_DK] * (DF_DK ** -0.5)).astype(BF16)
    ones = jnp.ones((t, LANES), BF16)
    rb = t // DF_ROW_SPLIT

    def tile(start, delta, state):
        k = k_ref[0, pl.ds(start, t), :]
        v = v_ref[0, pl.ds(start, t), :]
        new = []
        for c in range(maps):
            v_aug = jnp.concatenate([v[:, (c // 2) * DF_DV:(c // 2 + 1) * DF_DV].astype(BF16), ones], axis=1)
            k_bf = k[:, c * DF_DK:(c + 1) * DF_DK].astype(BF16)
            for r in range(DF_ROW_SPLIT):
                s = _dot_nt(qs_scr[c, r * rb:(r + 1) * rb, :], k_bf)
                ps = []
                for i in range(pieces):
                    pc = s[:, i * LANES:(i + 1) * LANES]
                    if delta is not None:
                        pc = pc + bias_ref[c, delta, r * rb:(r + 1) * rb, i * LANES:(i + 1) * LANES]
                    ps.append(pc)
                j = 3 * (c * DF_ROW_SPLIT + r)
                new += _flash_step(ps, v_aug, *state[j:j + 3])
        return tuple(new)

    init = (jnp.full((rb, LANES), M_FLOOR, F32), jnp.zeros((rb, LANES), F32),
            jnp.zeros((rb, DF_DV), F32)) * (maps * DF_ROW_SPLIT)
    state = lax.fori_loop(0, jnp.maximum(qi - 1, 0),
                          lambda ki, st: tile(pl.multiple_of(ki * t, t), None, st), init)
    state = lax.cond(qi >= 1, lambda st: tile(pl.multiple_of((qi - 1) * t, t), 1, st), lambda st: st, state)
    state = tile(pl.multiple_of(qi * t, t), 0, state)

    lam = _diff_lambda(lam_ref, lam_init)
    for hh in range(DF_HEADS_PER_STEP):
        for r in range(DF_ROW_SPLIT):
            j0, j1 = 3 * (2 * hh * DF_ROW_SPLIT + r), 3 * ((2 * hh + 1) * DF_ROW_SPLIT + r)
            (_, l0, a0), (_, l1, a1) = state[j0:j0 + 3], state[j1:j1 + 3]
            o = a0 / jnp.maximum(l0, 1e-30) - lam * (a1 / jnp.maximum(l1, 1e-30))
            o = o * lax.rsqrt(jnp.mean(o * o, -1, keepdims=True) + EPS) * hg_ref[hh]
            o_ref[0, r * rb:(r + 1) * rb, hh * DF_DV:(hh + 1) * DF_DV] = o * (1.0 - lam_init)


def diff_flash_prompt(q, kv, lam_p, head_g, rel_bias, lam_init):
    n, seq, _ = q.shape
    t = min(ATT_TILE, seq)
    nq = seq // t
    hps = DF_HEADS_PER_STEP
    width = hps * DF_DV
    assert t % LANES == 0 and _far_distance_ok() and t + 1 >= FAR_DISTANCE and DF_HEADS % hps == 0
    tiles = _toeplitz_tiles(rel_bias, t, 2, rel_to_far=True)
    return pl.pallas_call(
        functools.partial(_diff_flash_prompt_kernel, t=t, lam_init=lam_init),
        grid=(n, DF_HEADS // hps, nq),
        in_specs=[pl.BlockSpec((1, t, width), lambda b, h, qi: (b, qi, h)),
                  pl.BlockSpec((1, seq, width), lambda b, h, qi: (b, 0, h)),
                  pl.BlockSpec((1, seq, width), lambda b, h, qi: (b, 0, DF_HEADS // hps + h)),
                  pl.BlockSpec((2 * hps, 2, t, t), lambda b, h, qi: (h, 0, 0, 0)),
                  pl.BlockSpec((4, DF_DK), lambda b, h, qi: (0, 0)),
                  pl.BlockSpec((hps, 1, DF_DV), lambda b, h, qi: (h, 0, 0))],
        out_specs=pl.BlockSpec((1, t, width), lambda b, h, qi: (b, qi, h)),
        out_shape=jax.ShapeDtypeStruct((n, seq, D_MODEL), F32),
        scratch_shapes=[pltpu.VMEM((2 * hps, t, DF_DK), BF16)],
        compiler_params=_cparams("parallel", "parallel", "arbitrary"),
        name="diff_flash_prompt",
    )(q, kv, kv, tiles, lam_p, head_g.reshape(DF_HEADS, 1, DF_DV))


def _mlstm_kernel(*refs, c, has_init):
    if has_init:
        (q_ref, k_ref, v_ref, gc_ref, gr_ref, og_ref, bc_ref, br_ref, hg_ref, c0_ref, n0_ref, m0_ref,
         a_ref, cout_ref, nout_ref, mout_ref, c_scr, n_scr, m_scr) = refs
    else:
        (q_ref, k_ref, v_ref, gc_ref, gr_ref, og_ref, bc_ref, br_ref, hg_ref,
         a_ref, cout_ref, nout_ref, mout_ref, c_scr, n_scr, m_scr) = refs
    j = pl.program_id(1)

    @pl.when(j == 0)
    def _():
        if has_init:
            c_scr[...] = c0_ref[0]
            n_scr[...] = n0_ref[0]
            m_scr[...] = m0_ref[0]
        else:
            c_scr[...] = jnp.zeros(c_scr.shape, F32)
            n_scr[...] = jnp.zeros(n_scr.shape, F32)
            m_scr[...] = jnp.zeros(m_scr.shape, F32)

    row = lax.broadcasted_iota(jnp.int32, (c, c), 0)
    col = lax.broadcasted_iota(jnp.int32, (c, c), 1)
    tri = row >= col
    gc = gc_ref[0] + bc_ref[...]
    gr = gr_ref[0] + br_ref[...]
    f_cols = _dot_exact(jnp.where(tri, 1.0, 0.0), _log_sigmoid(gc))
    f_rows = _dot_exact(_log_sigmoid(gr), jnp.where(row <= col, 1.0, 0.0))
    og = og_ref[0]
    for h in range(ML_HEADS):
        qc = q_ref[0, :, h * ML_DK:(h + 1) * ML_DK]
        kc = k_ref[0, :, h * ML_DK:(h + 1) * ML_DK] * (ML_DK ** -0.5)
        vc = v_ref[0, :, h * ML_DV:(h + 1) * ML_DV]
        q_bf, k_bf = qc.astype(BF16), kc.astype(BF16)
        f_col = f_cols[:, ML_HEADS + h:ML_HEADS + h + 1]
        f_row = f_rows[ML_HEADS + h:ML_HEADS + h + 1, :]
        i_col = gc[:, h:h + 1]
        i_row = gr[h:h + 1, :]
        m_prev = m_scr[h:h + 1, 0:1]
        dm = jnp.where(tri, f_col - f_row + i_row, -1e30)
        inter = f_col + m_prev
        mt = jnp.maximum(jnp.max(dm, -1, keepdims=True), inter)
        w = jnp.where(tri, jnp.exp(dm - mt), 0.0)
        s = _dot_nt(q_bf, k_bf) * w
        a = jnp.exp(inter - mt)
        c_old = c_scr[h]
        n_old = n_scr[h:h + 1, :]
        num = _dot(s.astype(BF16), vc.astype(BF16)) + a * _dot_nt(q_bf, c_old.astype(BF16))
        den = jnp.sum(s, -1, keepdims=True) + a * jnp.sum(qc * n_old, -1, keepdims=True)
        hh = num / jnp.maximum(jnp.abs(den), jnp.exp(-mt))
        m_new = mt[c - 1:c, :]
        wend = jnp.exp(f_col[c - 1:c, :] - f_col + i_col - m_new)
        decay = jnp.exp(inter[c - 1:c, :] - m_new)
        c_scr[h] = decay * c_old + _dot_tn((wend * vc).astype(BF16), k_bf)
        n_scr[h:h + 1, :] = decay * n_old + jnp.sum(wend * kc, 0, keepdims=True)
        m_scr[h:h + 1, :] = jnp.broadcast_to(m_new, (1, LANES))
        hn = hh * lax.rsqrt(jnp.mean(hh * hh, -1, keepdims=True) + EPS) * hg_ref[h:h + 1, :]
        a_ref[0, :, h * ML_DV:(h + 1) * ML_DV] = _sigmoid(og[:, h * ML_DV:(h + 1) * ML_DV]) * hn

    @pl.when(j == pl.num_programs(1) - 1)
    def _():
        cout_ref[0] = c_scr[...]
        nout_ref[0] = n_scr[...]
        mout_ref[0] = m_scr[...]


def mlstm_cell(q, k, v, gates, og, b_if, head_g, state):
    n, seq, _ = q.shape
    c = min(LANES, seq)
    nc = seq // c
    has_init = state is not None
    gates_t = jnp.swapaxes(gates[:, :, :2 * ML_HEADS], 1, 2)
    b_flat = b_if.reshape(2 * ML_HEADS)
    b_col = jnp.pad(b_flat, (0, LANES - 2 * ML_HEADS)).reshape(1, LANES)
    b_row = b_flat.reshape(2 * ML_HEADS, 1)
    tok = lambda w: pl.BlockSpec((1, c, w), lambda b, j: (b, j, 0))
    fix2 = lambda s: pl.BlockSpec(s, lambda b, j: (0, 0))
    in_specs = [tok(ML_QK), tok(ML_QK), tok(D_MODEL), tok(LANES),
                pl.BlockSpec((1, 2 * ML_HEADS, c), lambda b, j: (b, 0, j)), tok(D_MODEL),
                fix2((1, LANES)), fix2((2 * ML_HEADS, 1)), fix2((ML_HEADS, ML_DV))]
    args = [q, k, v, gates, gates_t, og, b_col, b_row, head_g]
    st4 = pl.BlockSpec((1, ML_HEADS, ML_DV, ML_DK), lambda b, j: (b, 0, 0, 0))
    st3 = pl.BlockSpec((1, ML_HEADS, LANES), lambda b, j: (b, 0, 0))
    if has_init:
        c0, n0, m0 = state
        in_specs += [st4, st3, st3]
        args += [c0, n0, jnp.broadcast_to(m0[:, :, None], (n, ML_HEADS, LANES))]
    a, c_new, n_new, m_new = pl.pallas_call(
        functools.partial(_mlstm_kernel, c=c, has_init=has_init),
        grid=(n, nc),
        in_specs=in_specs,
        out_specs=[tok(D_MODEL), st4, st3, st3],
        out_shape=[jax.ShapeDtypeStruct((n, seq, D_MODEL), F32),
                   jax.ShapeDtypeStruct((n, ML_HEADS, ML_DV, ML_DK), F32),
                   jax.ShapeDtypeStruct((n, ML_HEADS, ML_DK), F32),
                   jax.ShapeDtypeStruct((n, ML_HEADS, LANES), F32)],
        scratch_shapes=[pltpu.VMEM((ML_HEADS, ML_DV, ML_DK), F32), pltpu.VMEM((ML_HEADS, ML_DK), F32),
                        pltpu.VMEM((ML_HEADS, LANES), F32)],
        compiler_params=_cparams("parallel", "arbitrary"),
        name="mlstm",
    )(*args)
    return a, c_new, n_new, m_new[:, :, 0]


def _conv_kernel(a_ref, b_ref, buf_ref, w_ref, dwb_ref, lng_ref, lnb_ref, o_ref, nbuf_ref, ctx_scr, *, tt):
    j = pl.program_id(1)

    @pl.when(j == 0)
    def _():
        ctx_scr[0:CONV_HALO, :] = buf_ref[0]

    @pl.when(j > 0)
    def _():
        ctx_scr[0:CONV_HALO, :] = ctx_scr[tt:tt + CONV_HALO, :]

    ctx_scr[CONV_HALO:CONV_HALO + tt, :] = a_ref[0] * _sigmoid(b_ref[0])
    off = CONV_HALO - (CONV_WIDTH - 1)
    acc = jnp.zeros((tt, D_MODEL), F32) + dwb_ref[...]
    sub = 8
    for r in range(sub):
        lead = off + r
        base, shift = (lead // sub) * sub, lead % sub
        rows = tt if shift == 0 else tt + sub
        part = None
        for tap in range(r, CONV_WIDTH, sub):
            lo = base + tap - r
            term = ctx_scr[lo:lo + rows, :] * w_ref[tap:tap + 1, :]
            part = term if part is None else part + term
        acc = acc + part[shift:shift + tt, :]
    mu = jnp.mean(acc, -1, keepdims=True)
    xc = acc - mu
    var = jnp.mean(xc * xc, -1, keepdims=True)
    o_ref[0] = _silu(xc * lax.rsqrt(var + EPS) * lng_ref[...] + lnb_ref[...])
    nbuf_ref[0] = ctx_scr[tt:tt + CONV_HALO, :]


def conv_block(a, b, buf, dw_w, dw_b, ln_g, ln_b):
    n, seq, d = a.shape
    tt = min(ATT_TILE, seq)
    assert seq == tt or tt >= CONV_HALO
    hist = CONV_WIDTH - 1
    if buf is None:
        buf_pad = jnp.zeros((n, CONV_HALO, d), F32)
    else:
        buf_pad = jnp.pad(buf, ((0, 0), (CONV_HALO - hist, 0), (0, 0)))
    w_pad = jnp.pad(dw_w, ((0, CONV_HALO - CONV_WIDTH), (0, 0)))
    tok = pl.BlockSpec((1, tt, d), lambda bb, j: (bb, j, 0))
    vec = pl.BlockSpec((1, d), lambda bb, j: (0, 0))
    halo = pl.BlockSpec((1, CONV_HALO, d), lambda bb, j: (bb, 0, 0))
    out, nbuf = pl.pallas_call(
        functools.partial(_conv_kernel, tt=tt),
        grid=(n, seq // tt),
        in_specs=[tok, tok, halo, pl.BlockSpec((CONV_HALO, d), lambda bb, j: (0, 0)), vec, vec, vec],
        out_specs=[tok, halo],
        out_shape=[jax.ShapeDtypeStruct((n, seq, d), F32), jax.ShapeDtypeStruct((n, CONV_HALO, d), F32)],
        scratch_shapes=[pltpu.VMEM((CONV_HALO + tt, d), F32)],
        compiler_params=_cparams("parallel", "arbitrary"),
        name="conv_block",
    )(a, b, buf_pad, w_pad, dw_b.reshape(1, d), ln_g.reshape(1, d), ln_b.reshape(1, d))
    return out, nbuf[:, CONV_HALO - hist:, :]


def _page_spec(block, col_block, j):
    return pl.BlockSpec(block, lambda b, pt: (pt[b, j], 0, col_block))


def _page_means_kernel(pt_ref, *refs):
    page_refs, o_ref = refs[:-1], refs[-1]
    for j, page_ref in enumerate(page_refs):
        x = page_ref[0]
        blocks = x.shape[0] // NSA_BLOCK
        o_ref[0, j * blocks:(j + 1) * blocks, :] = (
            jnp.sum(x.reshape(blocks, NSA_BLOCK, x.shape[1]), axis=1) * (1.0 / NSA_BLOCK))


def nsa_page_means(pool, page_table):
    n, n_pages = page_table.shape
    page = pool.shape[1]
    nbp = n_pages * page // NSA_BLOCK
    return pl.pallas_call(
        _page_means_kernel,
        grid_spec=pltpu.PrefetchScalarGridSpec(
            num_scalar_prefetch=1, grid=(n,),
            in_specs=[_page_spec((1, page, 2 * NSA_KVW), 0, j) for j in range(n_pages)],
            out_specs=pl.BlockSpec((1, nbp, 2 * NSA_KVW), lambda b, pt: (b, 0, 0))),
        out_shape=jax.ShapeDtypeStruct((n, nbp, 2 * NSA_KVW), F32),
        compiler_params=_cparams("parallel"),
        name="nsa_page_means",
    )(page_table, *([pool] * n_pages))


def _stack_group_queries(q, kh):
    parts = [q[:, (kh * NSA_GROUP + g) * NSA_DH:(kh * NSA_GROUP + g + 1) * NSA_DH] for g in range(NSA_GROUP)]
    return (jnp.concatenate(parts, axis=0) * (NSA_DH ** -0.5)).astype(BF16)


def _nsa_sample_local_kernel(q_ref, means_ref, rows_ref, win_ref, neww_ref, g_ref, bc_ref, bw_ref,
                             oc_ref, ow_ref, selb_ref, wout_ref, mk_scr, *, past, seq, k_sel):
    nbp = means_ref.shape[1]
    wb = win_ref.shape[1]
    mk_scr[...] = jnp.zeros(mk_scr.shape, F32)
    mk_scr[0:nbp, :] = means_ref[0]
    mk_scr[nbp:nbp + 1, :] = jnp.sum(rows_ref[0][:, 0:2 * NSA_KVW], 0, keepdims=True) * (1.0 / NSA_BLOCK)
    neww = neww_ref[0]
    wout_ref[0, 0:wb - seq, :] = win_ref[0, seq:wb, :]
    wout_ref[0, wb - seq:wb, :] = neww
    new_pad = jnp.concatenate([neww, jnp.zeros((LANES - seq, 2 * NSA_KVW), F32)], axis=0)
    q = q_ref[0]
    gate = _sigmoid(g_ref[0])
    imps = []
    for kh in range(NSA_KV_HEADS):
        qs = _stack_group_queries(q, kh)
        lo, hi = kh * NSA_DH, (kh + 1) * NSA_DH
        p = _softmax_rows(_dot_nt(qs, mk_scr[:, lo:hi].astype(BF16)) + bc_ref[kh])
        oc = _dot(p.astype(BF16), mk_scr[:, NSA_KVW + lo:NSA_KVW + hi].astype(BF16))
        imp = p[0:seq]
        for g in range(1, NSA_GROUP):
            imp = imp + p[g * seq:(g + 1) * seq]
        imps.append(imp)
        s_old = _dot_nt(qs, win_ref[0, :, lo:hi].astype(BF16)) + bw_ref[kh, :, 0:wb]
        s_new = _dot_nt(qs, new_pad[:, lo:hi].astype(BF16)) + bw_ref[kh, :, wb:wb + LANES]
        mw = jnp.maximum(jnp.maximum(jnp.max(s_old, -1, keepdims=True), jnp.max(s_new, -1, keepdims=True)), M_FLOOR)
        p_old, p_new = jnp.exp(s_old - mw), jnp.exp(s_new - mw)
        lw = jnp.sum(p_old, -1, keepdims=True) + jnp.sum(p_new, -1, keepdims=True)
        ow = (_dot(p_old.astype(BF16), win_ref[0, :, NSA_KVW + lo:NSA_KVW + hi].astype(BF16))
              + _dot(p_new.astype(BF16), new_pad[:, NSA_KVW + lo:NSA_KVW + hi].astype(BF16))) / jnp.maximum(lw, 1e-30)
        for g in range(NSA_GROUP):
            h = kh * NSA_GROUP + g
            oc_ref[0, :, h * NSA_DH:(h + 1) * NSA_DH] = oc[g * seq:(g + 1) * seq] * gate[:, h:h + 1]
            ow_ref[0, :, h * NSA_DH:(h + 1) * NSA_DH] = (
                ow[g * seq:(g + 1) * seq] * gate[:, 2 * NSA_HEADS + h:2 * NSA_HEADS + h + 1])
    rows_sel = NSA_KV_HEADS * seq
    imp_all = jnp.concatenate(imps + [jnp.zeros((LANES - rows_sel, LANES), F32)], axis=0)
    pos = past + lax.broadcasted_iota(jnp.int32, (1, LANES), 1) % seq
    sel = _select_blocks(imp_all, (pos // NSA_BLOCK).astype(F32), k_sel)
    for kh in range(NSA_KV_HEADS):
        selb_ref[0, kh] = sel[kh * seq:(kh + 1) * seq]


def _group_rows(tab):
    return tab.reshape(NSA_KV_HEADS, NSA_GROUP * tab.shape[1], tab.shape[2])


def nsa_sample_local(q, means, rows, win_buf, new_win, gates, rel_bias, past):
    n, seq, _ = q.shape
    nbp = means.shape[1]
    wb = win_buf.shape[1]
    nb = -(-(past + seq) // NSA_BLOCK)
    assert nbp + 1 == nb and nb <= LANES and seq % 8 == 0 and NSA_KV_HEADS * seq <= LANES
    lw = wb + LANES
    assert wb % LANES == 0 and seq <= LANES
    qpos = past + jnp.arange(seq)[:, None]
    bend = jnp.arange(LANES)[None, :] * NSA_BLOCK + NSA_BLOCK - 1
    bc = _group_rows(_bias_table(rel_bias, qpos - bend, bend <= qpos))
    cidx = jnp.arange(lw)[None, :]
    wpos = past - wb + cidx
    dw = qpos - wpos
    bw = _group_rows(_bias_table(rel_bias, dw, (dw >= 0) & (dw <= NSA_WINDOW) & (wpos >= 0) & (cidx < wb + seq)))
    rows_q = NSA_GROUP * seq
    per_b = lambda s: pl.BlockSpec((1,) + s, lambda b: (b,) + (0,) * len(s))
    fix = lambda s: pl.BlockSpec(s, lambda b: (0,) * len(s))
    return pl.pallas_call(
        functools.partial(_nsa_sample_local_kernel, past=past, seq=seq, k_sel=min(NSA_TOPK, nb)),
        grid=(n,),
        in_specs=[per_b((seq, D_MODEL)), per_b((nbp, 2 * NSA_KVW)), per_b((seq, D_MODEL)), per_b((wb, 2 * NSA_KVW)),
                  per_b((seq, 2 * NSA_KVW)), per_b((seq, LANES)), fix((NSA_KV_HEADS, rows_q, LANES)),
                  fix((NSA_KV_HEADS, rows_q, lw))],
        out_specs=[per_b((seq, D_MODEL)), per_b((seq, D_MODEL)), per_b((NSA_KV_HEADS, seq, LANES)),
                   per_b((wb, 2 * NSA_KVW))],
        out_shape=[jax.ShapeDtypeStruct((n, seq, D_MODEL), F32), jax.ShapeDtypeStruct((n, seq, D_MODEL), F32),
                   jax.ShapeDtypeStruct((n, NSA_KV_HEADS, seq, LANES), F32),
                   jax.ShapeDtypeStruct((n, wb, 2 * NSA_KVW), F32)],
        scratch_shapes=[pltpu.VMEM((LANES, 2 * NSA_KVW), F32)],
        compiler_params=_cparams("parallel"),
        name="nsa_sample_local",
    )(q, means, rows, win_buf, new_win, gates, bc, bw)


def _nsa_sample_selected_kernel(pt_ref, q_ref, rows_ref, selb_ref, bias_ref, g_ref, *refs, seq, page, past):
    n_pages = len(refs) - 2
    page_refs, o_ref, new_scr = refs[:n_pages], refs[n_pages], refs[n_pages + 1]
    new_scr[...] = jnp.zeros(new_scr.shape, F32)
    new_scr[0:seq, :] = rows_ref[0][:, 2 * NSA_KVW:4 * NSA_KVW]
    q = q_ref[0]
    gate = _sigmoid(g_ref[0])
    sources = [(page_refs[j], j * page) for j in range(n_pages)] + [(None, past)]
    expands = [_block_expand(first_pos, page) for _, first_pos in sources]
    for kh in range(NSA_KV_HEADS):
        lo, hi = kh * NSA_DH, (kh + 1) * NSA_DH
        qs = _stack_group_queries(q, kh)
        sel = selb_ref[0, kh].astype(BF16)
        scores = []
        for j, (ref, _) in enumerate(sources):
            k_bf = (new_scr[:, lo:hi] if ref is None else ref[0, :, lo:hi]).astype(BF16)
            sb = _dot(sel, expands[j])
            scores.append(_dot_nt(qs, k_bf) + bias_ref[j, kh] + jnp.concatenate([sb] * NSA_GROUP, axis=0))
        mx = scores[0]
        for s in scores[1:]:
            mx = jnp.maximum(mx, s)
        m = jnp.maximum(jnp.max(mx, -1, keepdims=True), M_FLOOR)
        total = None
        acc = None
        for j, (ref, _) in enumerate(sources):
            p = jnp.exp(scores[j] - m)
            v_bf = (new_scr[:, NSA_KVW + lo:NSA_KVW + hi] if ref is None
                    else ref[0, :, NSA_KVW + lo:NSA_KVW + hi]).astype(BF16)
            pv = _dot(p.astype(BF16), v_bf)
            total = p if total is None else total + p
            acc = pv if acc is None else acc + pv
        o = acc / jnp.maximum(jnp.sum(total, -1, keepdims=True), 1e-30)
        for g in range(NSA_GROUP):
            h = kh * NSA_GROUP + g
            o_ref[0, :, h * NSA_DH:(h + 1) * NSA_DH] = (
                o[g * seq:(g + 1) * seq] * gate[:, NSA_HEADS + h:NSA_HEADS + h + 1])


def nsa_sample_selected(q, pool, page_table, rows, selb, gates, rel_bias, past):
    n, seq, _ = q.shape
    n_pages = page_table.shape[1]
    page = pool.shape[1]
    assert page == LANES and seq <= page
    qpos = past + jnp.arange(seq)[:, None]
    j = jnp.arange(page)[None, :]
    d_past = qpos[None] - (jnp.arange(n_pages)[:, None, None] * page + j[None])
    d_new = (qpos - past) - j
    dist = jnp.concatenate([d_past, d_new[None]], 0)
    valid = jnp.concatenate([d_past >= 0, ((d_new >= 0) & (j < seq))[None]], 0)
    bias = _bias_table(rel_bias, dist, valid)
    bias = jnp.moveaxis(bias, 1, 0).reshape(n_pages + 1, NSA_KV_HEADS, NSA_GROUP * seq, page)
    rows_q = NSA_GROUP * seq
    per_b = lambda s: pl.BlockSpec((1,) + s, lambda b, pt: (b,) + (0,) * len(s))
    return pl.pallas_call(
        functools.partial(_nsa_sample_selected_kernel, seq=seq, page=page, past=past),
        grid_spec=pltpu.PrefetchScalarGridSpec(
            num_scalar_prefetch=1, grid=(n,),
            in_specs=[per_b((seq, D_MODEL)), per_b((seq, D_MODEL)), per_b((NSA_KV_HEADS, seq, LANES)),
                      pl.BlockSpec((n_pages + 1, NSA_KV_HEADS, rows_q, page), lambda b, pt: (0, 0, 0, 0)),
                      per_b((seq, LANES))]
            + [_page_spec((1, page, 2 * NSA_KVW), 1, jj) for jj in range(n_pages)],
            out_specs=per_b((seq, D_MODEL)),
            scratch_shapes=[pltpu.VMEM((page, 2 * NSA_KVW), F32)]),
        out_shape=jax.ShapeDtypeStruct((n, seq, D_MODEL), F32),
        compiler_params=_cparams("parallel"),
        name="nsa_sample_selected",
    )(page_table, q, rows, selb, bias, gates, *([pool] * n_pages))


def _diff_sample_kernel(pt_ref, q_ref, new_ref, tfar_ref, tnear_ref, tnew_ref, lam_ref, hg_ref, *refs,
                        seq, lam_init, pps, n_near):
    k_refs, v_refs = refs[:pps], refs[pps:2 * pps]
    o_ref, qall_scr, m_scr, l_scr, acc_scr = refs[2 * pps:]
    g = pl.program_id(1)
    last = pl.num_programs(1) - 1
    rows_k = k_refs[0].shape[0] * DF_HEADS
    pieces = rows_k // LANES

    @pl.when(g == 0)
    def _():
        m_scr[...] = jnp.full(m_scr.shape, M_FLOOR, F32)
        l_scr[...] = jnp.zeros(l_scr.shape, F32)
        acc_scr[...] = jnp.zeros(acc_scr.shape, F32)
        q = q_ref[0] * (DF_DK ** -0.5)
        half = lax.broadcasted_iota(jnp.int32, (seq, DF_DV), 1) // DF_DK
        parts = []
        for h in range(DF_HEADS):
            qh = q[:, h * DF_DV:(h + 1) * DF_DV]
            parts += [jnp.where(half == c, qh, 0.0) for c in range(2)]
        qall_scr[...] = jnp.concatenate(parts, axis=0).astype(BF16)

    qall = qall_scr[...]
    for jj in range(pps):
        kf = k_refs[jj][...].reshape(rows_k, DF_DV).astype(BF16)
        vf = v_refs[jj][...].reshape(rows_k, DF_DV).astype(BF16)
        s = _dot_nt(qall, kf)
        near = jj - (pps - n_near)
        ps = []
        for i in range(pieces):
            tab = tfar_ref[:, i * LANES:(i + 1) * LANES]
            if near >= 0:
                tab = jnp.where(g == last, tnear_ref[near, :, i * LANES:(i + 1) * LANES], tab)
            ps.append(s[:, i * LANES:(i + 1) * LANES] + tab)
        _flash_update(ps, vf, m_scr, acc_scr, slice(None), l_ref=l_scr)

    @pl.when(g == last)
    def _():
        new = new_ref[0]
        pad = jnp.zeros((LANES - DF_HEADS * seq, DF_DV), F32)
        kn = jnp.concatenate([new[:, h * DF_DV:(h + 1) * DF_DV] for h in range(DF_HEADS)] + [pad], axis=0)
        vn = jnp.concatenate([new[:, D_MODEL + h * DF_DV:D_MODEL + (h + 1) * DF_DV] for h in range(DF_HEADS)]
                             + [pad], axis=0)
        s = _dot_nt(qall, kn.astype(BF16)) + tnew_ref[...]
        _flash_update([s], vn.astype(BF16), m_scr, acc_scr, slice(None), l_ref=l_scr)
        lam = _diff_lambda(lam_ref, lam_init)
        acc = acc_scr[...] / jnp.maximum(l_scr[...], 1e-30)
        for h in range(DF_HEADS):
            r0 = 2 * h * seq
            o = acc[r0:r0 + seq] - lam * acc[r0 + seq:r0 + 2 * seq]
            o = o * lax.rsqrt(jnp.mean(o * o, -1, keepdims=True) + EPS) * hg_ref[:, h * DF_DV:(h + 1) * DF_DV]
            o_ref[0, :, h * DF_DV:(h + 1) * DF_DV] = o * (1.0 - lam_init)


def diff_sample(q, pool6, j_layer, page_table, kv_new, lam_p, head_g, rel_bias, past, lam_init):
    n, seq, _ = q.shape
    n_pages = page_table.shape[1]
    page = pool6.shape[2]
    pps = min(DF_PAGES_PER_STEP, n_pages)
    rows_q = 2 * DF_HEADS * seq
    rows_k = page * DF_HEADS
    assert n_pages % pps == 0 and rows_q == LANES and DF_HEADS * seq <= LANES and _far_distance_ok()
    n_near = sum(1 for p in range(n_pages) if past - (p * page + page - 1) < FAR_DISTANCE)
    assert 1 <= n_near <= pps
    head_of_row = jnp.arange(rows_q) // (2 * seq)
    same_head = head_of_row[:, None] == (jnp.arange(rows_k) % DF_HEADS)[None, :]
    far = jnp.repeat(rel_bias[N_BUCKETS - 1], seq)[:, None]
    t_far = jnp.where(same_head, far, MASKED).astype(F32)
    qpos = past + jnp.arange(seq)[:, None]
    near_pages = jnp.arange(n_pages - n_near, n_pages)
    d_near = qpos[None] - (near_pages[:, None, None] * page + jnp.arange(page)[None, None, :])
    b_near = _bias_table(rel_bias, d_near, d_near >= 0)
    b_near = jnp.moveaxis(b_near, 1, 0).reshape(n_near, rows_q, page)
    b_near = jnp.broadcast_to(b_near[..., None], (n_near, rows_q, page, DF_HEADS)).reshape(n_near, rows_q, rows_k)
    t_near = jnp.where(same_head[None], b_near, MASKED).astype(F32)
    d_new = jnp.arange(seq)[:, None] - jnp.arange(seq)[None, :]
    b_new = _bias_table(rel_bias, d_new, d_new >= 0).reshape(rows_q, 1, seq)
    b_new = jnp.broadcast_to(b_new, (rows_q, DF_HEADS, seq)).reshape(rows_q, DF_HEADS * seq)
    same_new = head_of_row[:, None] == (jnp.arange(DF_HEADS * seq) // seq)[None, :]
    t_new = jnp.pad(jnp.where(same_new, b_new, MASKED), ((0, 0), (0, LANES - DF_HEADS * seq)),
                    constant_values=MASKED).astype(F32)

    def page_spec(jj, c):
        return pl.BlockSpec((None, None, page, None, DF_HEADS, DF_DV),
                            lambda b, g, pt: (j_layer, pt[b, g * pps + jj], 0, c, 0, 0))

    fix = lambda s: pl.BlockSpec(s, lambda b, g, pt: (0,) * len(s))
    return pl.pallas_call(
        functools.partial(_diff_sample_kernel, seq=seq, lam_init=lam_init, pps=pps, n_near=n_near),
        grid_spec=pltpu.PrefetchScalarGridSpec(
            num_scalar_prefetch=1, grid=(n, n_pages // pps),
            in_specs=[pl.BlockSpec((1, seq, D_MODEL), lambda b, g, pt: (b, 0, 0)),
                      pl.BlockSpec((1, seq, 2 * D_MODEL), lambda b, g, pt: (b, 0, 0)),
                      fix((rows_q, rows_k)), fix((n_near, rows_q, rows_k)), fix((rows_q, LANES)),
                      fix((4, DF_DK)), fix((1, D_MODEL))]
            + [page_spec(jj, 0) for jj in range(pps)] + [page_spec(jj, 1) for jj in range(pps)],
            out_specs=pl.BlockSpec((1, seq, D_MODEL), lambda b, g, pt: (b, 0, 0)),
            scratch_shapes=[pltpu.VMEM((rows_q, DF_DV), BF16), pltpu.VMEM((rows_q, LANES), F32),
                            pltpu.VMEM((rows_q, LANES), F32), pltpu.VMEM((rows_q, DF_DV), F32)]),
        out_shape=jax.ShapeDtypeStruct((n, seq, D_MODEL), F32),
        compiler_params=_cparams("parallel", "arbitrary"),
        name="diff_sample",
    )(page_table, q, kv_new, t_far, t_near, t_new, lam_p, head_g.reshape(1, D_MODEL), *([pool6] * (2 * pps)))


def _trunk(x, p, past, page_table, nsa_pool, nsa_win, ml_c, ml_n, ml_m, conv_buf, diff_pool, weights):
    (norm_g, final_norm_g, rel_bias, ple_w, ple_gate_w, ple_norm_g,
     nsa_w_in, nsa_w_out, mlstm_w_in, mlstm_b_if, mlstm_head_g, mlstm_w_out,
     conv_w_in, conv_dw_w, conv_dw_b, conv_ln_g, conv_ln_b, conv_w_out,
     diff_w_in, diff_lam, diff_head_g, diff_w_out) = weights
    n, seq, d = x.shape
    m = n * seq
    sample = page_table is not None
    x2 = x.reshape(m, d)
    depth = norm_g.shape[0]
    out = {}
    for i in range(depth):
        kind, j = i % 4, i // 4
        assert j == 0
        p_i = p[i].reshape(m, D_PLE)
        tail = dict(x=x2, p=p_i, ple_norm_g=ple_norm_g[i], ple_gate_w=ple_gate_w[i], ple_w=ple_w[i],
                    final_g=final_norm_g if i == depth - 1 else None)
        if kind == 0:
            w = nsa_w_in[j]
            c0, c1, c2, c3 = D_MODEL, 2 * D_MODEL, 2 * D_MODEL + 2 * NSA_KVW, 2 * D_MODEL + 2 * NSA_KVW + 3 * NSA_HEADS
            q, rows, win, gates, z = norm_linear(x2, norm_g[i], [w[:, :c0], w[:, c0:c1], w[:, c1:c2], w[:, c2:c3], w[:, c3:]])
            q3, rows3, win3, gates3 = (t.reshape(n, seq, -1) for t in (q, rows, win, gates))
            if sample:
                pool = nsa_pool[j].reshape(nsa_pool.shape[1], nsa_pool.shape[2], D_MODEL)
                wb = nsa_win.shape[2]
                means = nsa_page_means(pool, page_table)
                oc, ow, selb, win_out = nsa_sample_local(q3, means, rows3, nsa_win[j].reshape(n, wb, 2 * NSA_KVW),
                                                         win3, gates3, rel_bias, past)
                os_ = nsa_sample_selected(q3, pool, page_table, rows3, selb, gates3, rel_bias, past)
            else:
                oc, selb = nsa_cmp_prompt(q3, rows3, gates3, rel_bias)
                os_ = nsa_flash_prompt(q3, rows3, 4, 6, gates3, rel_bias, selb, window=False)
                ow = nsa_flash_prompt(q3, win3, 0, 2, gates3, rel_bias, None, window=True)
                keep = min(NSA_WINDOW, seq)
                win_out = win3[:, seq - keep:]
                if keep < NSA_WINDOW:
                    win_out = jnp.concatenate([jnp.zeros((n, NSA_WINDOW - keep, 2 * NSA_KVW), F32), win_out], 1)
            out['kv'] = rows3.reshape(1, n, seq, 4, NSA_KV_HEADS, NSA_DH)
            out['win'] = win_out.reshape(1, n, win_out.shape[1], 2, NSA_KV_HEADS, NSA_DH)
            x2 = post_layer([t.reshape(m, d) for t in (oc, os_, ow)], z, w_out=nsa_w_out[j], **tail)
        elif kind == 1:
            w = mlstm_w_in[j]
            c0, c1, c2 = ML_QK, 2 * ML_QK, 2 * ML_QK + D_MODEL
            c3, c4 = c2 + 2 * ML_HEADS, c2 + 2 * ML_HEADS + D_MODEL
            q, k, v, gates, og, z = norm_linear(
                x2, norm_g[i], [w[:, :c0], w[:, c0:c1], w[:, c1:c2], w[:, c2:c3], w[:, c3:c4], w[:, c4:]])
            state = (ml_c[j], ml_n[j], ml_m[j]) if sample else None
            a, c_new, n_new, m_new = mlstm_cell(*(t.reshape(n, seq, -1) for t in (q, k, v, gates, og)),
                                                mlstm_b_if[j], mlstm_head_g[j], state)
            out['C'], out['n'], out['m'] = c_new[None], n_new[None], m_new[None]
            x2 = post_layer([a.reshape(m, d)], z, w_out=mlstm_w_out[j], **tail)
        elif kind == 2:
            w = conv_w_in[j]
            a, b, z = norm_linear(x2, norm_g[i], [w[:, :d], w[:, d:2 * d], w[:, 2 * d:]])
            cact, nbuf = conv_block(a.reshape(n, seq, d), b.reshape(n, seq, d), conv_buf[j] if sample else None,
                                    conv_dw_w[j], conv_dw_b[j], conv_ln_g[j], conv_ln_b[j])
            out['conv'] = nbuf[None]
            x2 = post_layer([cact.reshape(m, d)], z, w_out=conv_w_out[j], **tail)
        else:
            w = diff_w_in[j]
            q, kv, z = norm_linear(x2, norm_g[i], [w[:, :d], w[:, d:3 * d], w[:, 3 * d:]])
            lam_init = 0.8 - 0.6 * math.exp(-0.3 * i)
            q3, kv3 = q.reshape(n, seq, d), kv.reshape(n, seq, 2 * d)
            if sample:
                a = diff_sample(q3, diff_pool, j, page_table, kv3, diff_lam[j], diff_head_g[j], rel_bias, past,
                                lam_init)
            else:
                a = diff_flash_prompt(q3, kv3, diff_lam[j], diff_head_g[j], rel_bias, lam_init)
            out['diff'] = kv3.reshape(1, n, seq, 2, DF_HEADS, DF_DV)
            x2 = post_layer([a.reshape(m, d)], z, w_out=diff_w_out[j], **tail)
    return x2.reshape(n, seq, d), out


def kernel(x_prompt, x_sample, cache_nsa_kv, state_nsa_win, state_mlstm_C, state_mlstm_n, state_mlstm_m,
           state_conv, cache_diff_kv, page_table, p_prompt, p_sample, norm_g, final_norm_g, rel_bias,
           ple_w, ple_gate_w, ple_norm_g, nsa_w_in, nsa_w_out, mlstm_w_in, mlstm_b_if, mlstm_head_g,
           mlstm_w_out, conv_w_in, conv_dw_w, conv_dw_b, conv_ln_g, conv_ln_b, conv_w_out, diff_w_in,
           diff_lam, diff_head_g, diff_w_out):
    weights = (norm_g, final_norm_g, rel_bias, ple_w, ple_gate_w, ple_norm_g,
               nsa_w_in, nsa_w_out, mlstm_w_in, mlstm_b_if, mlstm_head_g, mlstm_w_out,
               conv_w_in, conv_dw_w, conv_dw_b, conv_ln_g, conv_ln_b, conv_w_out,
               diff_w_in, diff_lam, diff_head_g, diff_w_out)
    past = page_table.shape[1] * cache_nsa_kv.shape[2]
    y_p, o_p = _trunk(x_prompt, p_prompt, 0, None, None, None, None, None, None, None, None, weights)
    y_s, o_s = _trunk(x_sample, p_sample, past, page_table, cache_nsa_kv, state_nsa_win, state_mlstm_C,
                      state_mlstm_n, state_mlstm_m, state_conv, cache_diff_kv, weights)
    return (y_p, y_s, o_p['kv'], o_s['kv'], o_p['win'], o_s['win'], o_p['C'], o_s['C'], o_p['n'], o_s['n'],
            o_p['m'], o_s['m'], o_p['conv'], o_s['conv'], o_p['diff'], o_s['diff'])
```

```python
import functools
import math

import jax
import jax.numpy as jnp
import numpy as np
from jax import lax
from jax.experimental import pallas as pl
from jax.experimental.pallas import tpu as pltpu

F32 = jnp.float32
BF16 = jnp.bfloat16

D_MODEL = 1024
D_PLE = 256
N_BUCKETS = 32
MAX_DISTANCE = 128
EPS = 1e-6

NSA_HEADS = 16
NSA_KV_HEADS = 4
NSA_GROUP = 4
NSA_DH = 64
NSA_BLOCK = 32
NSA_TOPK = 16
NSA_WINDOW = 512
NSA_KVW = NSA_KV_HEADS * NSA_DH

ML_HEADS = 4
ML_DV = 256
ML_DK = 128
ML_QK = ML_HEADS * ML_DK

CONV_WIDTH = 31
CONV_HALO = 32

DF_HEADS = 8
DF_DK = 64
DF_DV = 128
DF_PAGES_PER_STEP = 16
DF_PAGES_PER_CHAIN = 4
DF_TILE = 256
DF_HEADS_PER_STEP = 2

LANES = 128
MASKED = -2e30
M_FLOOR = -1e30
FAR_DISTANCE = 113
VMEM_LIMIT = 48 * 1024 * 1024
ROW_TILE = 512
ATT_TILE = 256
COL_CHUNK = 512


def _cparams(*sem):
    return pltpu.CompilerParams(dimension_semantics=sem, vmem_limit_bytes=VMEM_LIMIT)


def _dot(a, b):
    return jnp.dot(a, b, preferred_element_type=F32)


def _dot_nt(a, b):
    return lax.dot_general(a, b, (((1,), (1,)), ((), ())), preferred_element_type=F32)


def _dot_tn(a, b):
    return lax.dot_general(a, b, (((0,), (0,)), ((), ())), preferred_element_type=F32)


def _dot_exact(a, b):
    return jnp.dot(a, b, preferred_element_type=F32, precision=lax.Precision.HIGHEST)


def _sigmoid(x):
    return 1.0 / (1.0 + jnp.exp(-x))


def _silu(x):
    return x * _sigmoid(x)


def _log_sigmoid(x):
    return jnp.minimum(x, 0.0) - jnp.log1p(jnp.exp(-jnp.abs(x)))


def _pad_cols(w, mult=LANES):
    pad = (-w.shape[-1]) % mult
    return w if pad == 0 else jnp.pad(w, [(0, 0)] * (w.ndim - 1) + [(0, pad)])


def _norm_linear_kernel(x_ref, g_ref, *refs, widths):
    w_refs, o_refs = refs[:len(widths)], refs[len(widths):]
    x = x_ref[...]
    xn = (x * lax.rsqrt(jnp.mean(x * x, -1, keepdims=True) + EPS) * g_ref[...]).astype(BF16)
    for w_ref, o_ref, width in zip(w_refs, o_refs, widths):
        for c0 in range(0, width, COL_CHUNK):
            c1 = min(c0 + COL_CHUNK, width)
            o_ref[:, c0:c1] = _dot(xn, w_ref[:, c0:c1])


def norm_linear(x2d, g, weights):
    m, d = x2d.shape
    ws = [_pad_cols(w).astype(BF16) for w in weights]
    widths = tuple(w.shape[1] for w in ws)
    tm = min(ROW_TILE, m)
    assert m % tm == 0
    outs = pl.pallas_call(
        functools.partial(_norm_linear_kernel, widths=widths),
        grid=(m // tm,),
        in_specs=[pl.BlockSpec((tm, d), lambda i: (i, 0)), pl.BlockSpec((1, d), lambda i: (0, 0))]
        + [pl.BlockSpec((d, wd), lambda i: (0, 0)) for wd in widths],
        out_specs=[pl.BlockSpec((tm, wd), lambda i: (i, 0)) for wd in widths],
        out_shape=[jax.ShapeDtypeStruct((m, wd), F32) for wd in widths],
        compiler_params=_cparams("parallel"),
        name="norm_linear",
    )(x2d, g.reshape(1, d), *ws)
    return outs


def _post_kernel(*refs, n_a, final):
    a_refs = refs[:n_a]
    z_ref, x_ref, p_ref, wo_ref, png_ref, wg_ref, wp_ref = refs[n_a:n_a + 7]
    rest = refs[n_a + 7:]
    fng_ref = rest[0] if final else None
    o_ref = rest[-1]
    a = a_refs[0][...]
    for r in a_refs[1:]:
        a = a + r[...]
    h = (a * _silu(z_ref[...])).astype(BF16)
    x1 = x_ref[...] + _dot(h, wo_ref[...])
    r = (x1 * lax.rsqrt(jnp.mean(x1 * x1, -1, keepdims=True) + EPS) * png_ref[...]).astype(BF16)
    gate = _sigmoid(_dot(r, wg_ref[...]))
    x2 = x1 + gate * _dot(p_ref[...].astype(BF16), wp_ref[...])
    if final:
        x2 = x2 * lax.rsqrt(jnp.mean(x2 * x2, -1, keepdims=True) + EPS) * fng_ref[...]
    o_ref[...] = x2


def post_layer(a_list, z, x, p, w_out, ple_norm_g, ple_gate_w, ple_w, final_g=None):
    m, d = x.shape
    tm = min(ROW_TILE, m)
    assert m % tm == 0
    final = final_g is not None
    row = lambda i: (i, 0)
    fix = lambda i: (0, 0)
    in_specs = [pl.BlockSpec((tm, d), row) for _ in a_list] + [
        pl.BlockSpec((tm, d), row), pl.BlockSpec((tm, d), row), pl.BlockSpec((tm, D_PLE), row),
        pl.BlockSpec((d, d), fix), pl.BlockSpec((1, d), fix), pl.BlockSpec((d, d), fix),
        pl.BlockSpec((D_PLE, d), fix)]
    args = list(a_list) + [z, x, p, w_out.astype(BF16), ple_norm_g.reshape(1, d), ple_gate_w.astype(BF16),
                           ple_w.astype(BF16)]
    if final:
        in_specs.append(pl.BlockSpec((1, d), fix))
        args.append(final_g.reshape(1, d))
    return pl.pallas_call(
        functools.partial(_post_kernel, n_a=len(a_list), final=final),
        grid=(m // tm,),
        in_specs=in_specs,
        out_specs=pl.BlockSpec((tm, d), row),
        out_shape=jax.ShapeDtypeStruct((m, d), F32),
        compiler_params=_cparams("parallel"),
        name="post_layer",
    )(*args)


def _t5_bucket(dist):
    d = jnp.maximum(dist, 0)
    exact = N_BUCKETS // 2
    df = jnp.maximum(d, 1).astype(F32)
    large = exact + (jnp.log(df / exact) / math.log(MAX_DISTANCE / exact) * (N_BUCKETS - exact)).astype(jnp.int32)
    return jnp.where(d < exact, d, jnp.minimum(large, N_BUCKETS - 1))


def _bias_table(rel_bias, dist, valid, rel_to_far=False):
    bucket = _t5_bucket(dist)[None]
    col = lambda k: rel_bias[k].reshape((-1,) + (1,) * dist.ndim)
    b = jnp.broadcast_to(col(N_BUCKETS - 1), (rel_bias.shape[1],) + dist.shape)
    for k in range(N_BUCKETS - 1):
        b = jnp.where(bucket == k, col(k), b)
    if rel_to_far:
        b = b - col(N_BUCKETS - 1)
    return jnp.where(valid[None], b, MASKED).astype(F32)


def _toeplitz_tiles(rel_bias, t, n_delta, max_dist=None, rel_to_far=False):
    i = jnp.arange(t)
    d = jnp.arange(n_delta)[:, None, None] * t + i[None, :, None] - i[None, None, :]
    valid = d >= 0
    if max_dist is not None:
        valid = valid & (d <= max_dist)
    return _bias_table(rel_bias, d, valid, rel_to_far)


def _far_distance_ok():
    d = np.arange(FAR_DISTANCE, FAR_DISTANCE + 8192, dtype=np.float64)
    large = 16 + np.floor(np.log(d / 16) / math.log(MAX_DISTANCE / 16) * 16 - 1e-6)
    return bool(np.all(large >= N_BUCKETS - 1))


def _flash_update(pieces, v_bf, m_ref, acc_ref, idx, l_ref=None, l_on_mxu=False):
    mx = pieces[0]
    for pc in pieces[1:]:
        mx = jnp.maximum(mx, pc)
    m_old = m_ref[idx]
    m_new = jnp.maximum(m_old, jnp.max(mx, -1, keepdims=True))
    alpha = jnp.exp(m_old - m_new)
    p = [jnp.exp(pc - m_new) for pc in pieces]
    if l_ref is not None and not l_on_mxu:
        total = p[0]
        for pc in p[1:]:
            total = total + pc
        l_ref[idx] = alpha * l_ref[idx] + jnp.sum(total, -1, keepdims=True)
    p = (p[0] if len(p) == 1 else jnp.concatenate(p, axis=1)).astype(BF16)
    if l_ref is not None and l_on_mxu:
        l_ref[idx] = alpha * l_ref[idx] + _dot(p, jnp.ones((p.shape[1], LANES), BF16))
    acc_ref[idx] = alpha * acc_ref[idx] + _dot(p, v_bf)
    m_ref[idx] = m_new


def _flash_step(pieces, v_aug, m_old, l_old, acc_old):
    mx = pieces[0]
    for pc in pieces[1:]:
        mx = jnp.maximum(mx, pc)
    m_new = jnp.maximum(m_old, jnp.max(mx, -1, keepdims=True))
    alpha = jnp.exp(m_old - m_new)
    p = [jnp.exp(pc - m_new) for pc in pieces]
    p = (p[0] if len(p) == 1 else jnp.concatenate(p, axis=1)).astype(BF16)
    res = _dot(p, v_aug)
    dv = acc_old.shape[1]
    return [m_new, alpha * l_old + res[:, dv:], alpha * acc_old + res[:, :dv]]


def _softmax_rows(s):
    m = jnp.maximum(jnp.max(s, -1, keepdims=True), M_FLOOR)
    e = jnp.exp(s - m)
    return e / jnp.maximum(jnp.sum(e, -1, keepdims=True), 1e-30)


def _select_blocks(imp, cur, k):
    imp_t = jnp.transpose(imp)
    blk = lax.broadcasted_iota(jnp.int32, imp_t.shape, 0).astype(F32)
    sel = jnp.where(blk == cur, float(NSA_GROUP + 1), jnp.where(blk < cur, imp_t, -1.0))
    chosen = jnp.zeros(imp_t.shape, F32)
    for _ in range(k):
        top = jnp.max(sel, 0, keepdims=True)
        first = jnp.min(jnp.where(sel == top, blk, float(LANES)), 0, keepdims=True)
        hit = blk == first
        chosen = jnp.where(hit, 1.0, chosen)
        sel = jnp.where(hit, -3.0, sel)
    return jnp.transpose(jnp.where(chosen > 0.0, 0.0, MASKED))


def _block_expand(first_pos, width):
    blk = lax.broadcasted_iota(jnp.int32, (LANES, width), 0)
    key_blk = (first_pos + lax.broadcasted_iota(jnp.int32, (LANES, width), 1)) // NSA_BLOCK
    return jnp.where(blk == key_blk, 1.0, 0.0).astype(BF16)


def _nsa_cmp_prompt_kernel(q_ref, rows_ref, g_ref, bias_ref, oc_ref, selb_ref, mk_scr, *, seq, tq, k_sel):
    i = pl.program_id(1)
    nb = seq // NSA_BLOCK

    @pl.when(i == 0)
    def _():
        means = jnp.sum(rows_ref[0].reshape(nb, NSA_BLOCK, 2 * NSA_KVW), axis=1) * (1.0 / NSA_BLOCK)
        mk_scr[...] = jnp.zeros(mk_scr.shape, F32)
        mk_scr[0:nb, :] = means

    q = q_ref[0]
    gate = _sigmoid(g_ref[0])
    pos = i * tq + lax.broadcasted_iota(jnp.int32, (1, tq), 1)
    cur = (pos // NSA_BLOCK).astype(F32)
    for kh in range(NSA_KV_HEADS):
        mk = mk_scr[:, kh * NSA_DH:(kh + 1) * NSA_DH].astype(BF16)
        mv = mk_scr[:, NSA_KVW + kh * NSA_DH:NSA_KVW + (kh + 1) * NSA_DH].astype(BF16)
        bias = jnp.concatenate([bias_ref[kh * NSA_GROUP + g] for g in range(NSA_GROUP)], axis=0)
        p = _softmax_rows(_dot_nt(_stack_group_queries(q, kh), mk) + bias)
        oc = _dot(p.astype(BF16), mv)
        imp = p[0:tq]
        for g in range(1, NSA_GROUP):
            imp = imp + p[g * tq:(g + 1) * tq]
        for g in range(NSA_GROUP):
            h = kh * NSA_GROUP + g
            oc_ref[0, :, h * NSA_DH:(h + 1) * NSA_DH] = oc[g * tq:(g + 1) * tq] * gate[:, h:h + 1]
        selb_ref[0, kh] = _select_blocks(imp, cur, k_sel)


def nsa_cmp_prompt(q, rows, gates, rel_bias):
    n, seq, _ = q.shape
    tq = min(ATT_TILE, seq)
    nb = seq // NSA_BLOCK
    assert nb <= LANES
    t = jnp.arange(seq)[:, None]
    bend = jnp.arange(LANES)[None, :] * NSA_BLOCK + NSA_BLOCK - 1
    bias = _bias_table(rel_bias, t - bend, bend <= t)
    return pl.pallas_call(
        functools.partial(_nsa_cmp_prompt_kernel, seq=seq, tq=tq, k_sel=min(NSA_TOPK, nb)),
        grid=(n, seq // tq),
        in_specs=[pl.BlockSpec((1, tq, D_MODEL), lambda b, i: (b, i, 0)),
                  pl.BlockSpec((1, seq, 2 * NSA_KVW), lambda b, i: (b, 0, 0)),
                  pl.BlockSpec((1, tq, LANES), lambda b, i: (b, i, 0)),
                  pl.BlockSpec((NSA_HEADS, tq, LANES), lambda b, i: (0, i, 0))],
        out_specs=[pl.BlockSpec((1, tq, D_MODEL), lambda b, i: (b, i, 0)),
                   pl.BlockSpec((1, NSA_KV_HEADS, tq, LANES), lambda b, i: (b, 0, i, 0))],
        out_shape=[jax.ShapeDtypeStruct((n, seq, D_MODEL), F32),
                   jax.ShapeDtypeStruct((n, NSA_KV_HEADS, seq, LANES), F32)],
        scratch_shapes=[pltpu.VMEM((LANES, 2 * NSA_KVW), F32)],
        compiler_params=_cparams("parallel", "arbitrary"),
        name="nsa_cmp_prompt",
    )(q, rows, gates, bias)


def _nsa_flash_prompt_kernel(*refs, t, window, gate_col):
    if window:
        q_ref, k_ref, v_ref, bias_ref, g_ref, o_ref, qs_scr, m_scr, acc_scr = refs
        selb_ref = None
    else:
        q_ref, k_ref, v_ref, bias_ref, g_ref, selb_ref, o_ref, qs_scr, m_scr, acc_scr = refs
    u = pl.program_id(1)
    qi = pl.program_id(2)
    heads = 2 * NSA_GROUP
    pieces = t // LANES

    m_scr[...] = jnp.full(m_scr.shape, M_FLOOR, F32)
    acc_scr[...] = jnp.zeros(acc_scr.shape, F32)
    q = q_ref[0]
    for h in range(heads):
        qs_scr[h] = (q[:, h * NSA_DH:(h + 1) * NSA_DH] * (NSA_DH ** -0.5)).astype(BF16)

    def span(start, deltas):
        start = start if isinstance(start, int) else pl.multiple_of(start, t)
        width = len(deltas) * t
        k = k_ref[0, pl.ds(start, width), :]
        v = v_ref[0, pl.ds(start, width), :]
        lane = lax.broadcasted_iota(jnp.int32, (width, LANES), 1)
        if not window:
            expand = _block_expand(start, width)
        for kv in range(2):
            k_bf = k[:, kv * NSA_DH:(kv + 1) * NSA_DH].astype(BF16)
            v_aug = jnp.where((lane < NSA_DH) if kv == 0 else (lane >= NSA_DH), v, 1.0).astype(BF16)
            if not window:
                sel_bias = _dot(selb_ref[0, kv].astype(BF16), expand)
            for g in range(NSA_GROUP):
                h = kv * NSA_GROUP + g
                s = _dot_nt(qs_scr[h], k_bf)
                ps = []
                for j, delta in enumerate(deltas):
                    for i in range(pieces):
                        cols = slice(j * t + i * LANES, j * t + (i + 1) * LANES)
                        pc = s[:, cols]
                        if delta is not None:
                            pc = pc + bias_ref[h, delta, :, i * LANES:(i + 1) * LANES]
                        if not window:
                            pc = pc + sel_bias[:, cols]
                        ps.append(pc)
                _flash_update(ps, v_aug, m_scr, acc_scr, h)

    if window:
        n_win = NSA_WINDOW // t
        for back in range(n_win + 1):
            @pl.when((qi >= back) if back == n_win else (qi == back))
            def _(back=back):
                span((qi - back) * t, list(range(back, -1, -1)))
    else:
        def far_body(ki, carry):
            span(ki * t, [None])
            return carry
        lax.fori_loop(0, jnp.maximum(qi - 1, 0), far_body, 0)

        @pl.when(qi >= 1)
        def _():
            span((qi - 1) * t, [1])
        span(qi * t, [0])

    gate = _sigmoid(g_ref[0])
    for h in range(heads):
        col = gate_col + h
        g0 = jnp.where(u == 0, gate[:, col:col + 1], gate[:, col + heads:col + heads + 1])
        acc = acc_scr[h]
        lo, hi = acc[:, 0:NSA_DH], acc[:, NSA_DH:2 * NSA_DH]
        o, l = (lo, hi) if h < NSA_GROUP else (hi, lo)
        o_ref[0, :, h * NSA_DH:(h + 1) * NSA_DH] = o / jnp.maximum(l, 1e-30) * g0


def nsa_flash_prompt(q, kv_arr, k_blk, v_blk, gates, rel_bias, selb, window):
    n, seq, _ = q.shape
    t = min(ATT_TILE, seq)
    nq = seq // t
    heads = 2 * NSA_GROUP
    assert t % LANES == 0 and _far_distance_ok()
    if window:
        assert NSA_WINDOW % t == 0
        n_delta = NSA_WINDOW // t + 1
        tiles = _toeplitz_tiles(rel_bias, t, n_delta, NSA_WINDOW)
    else:
        assert t + 1 >= FAR_DISTANCE
        n_delta = 2
        tiles = _toeplitz_tiles(rel_bias, t, n_delta, rel_to_far=True)
    in_specs = [pl.BlockSpec((1, t, heads * NSA_DH), lambda b, u, qi: (b, qi, u)),
                pl.BlockSpec((1, seq, LANES), lambda b, u, qi: (b, 0, k_blk + u)),
                pl.BlockSpec((1, seq, LANES), lambda b, u, qi: (b, 0, v_blk + u)),
                pl.BlockSpec((heads, n_delta, t, t), lambda b, u, qi: (u, 0, 0, 0)),
                pl.BlockSpec((1, t, LANES), lambda b, u, qi: (b, qi, 0))]
    args = [q, kv_arr, kv_arr, tiles, gates]
    if not window:
        in_specs.append(pl.BlockSpec((1, 2, t, LANES), lambda b, u, qi: (b, u, qi, 0)))
        args.append(selb)
    return pl.pallas_call(
        functools.partial(_nsa_flash_prompt_kernel, t=t, window=window, gate_col=2 * NSA_HEADS if window else NSA_HEADS),
        grid=(n, 2, nq),
        in_specs=in_specs,
        out_specs=pl.BlockSpec((1, t, heads * NSA_DH), lambda b, u, qi: (b, qi, u)),
        out_shape=jax.ShapeDtypeStruct((n, seq, D_MODEL), F32),
        scratch_shapes=[pltpu.VMEM((heads, t, NSA_DH), BF16), pltpu.VMEM((heads, t, LANES), F32),
                        pltpu.VMEM((heads, t, LANES), F32)],
        compiler_params=_cparams("parallel", "parallel", "arbitrary"),
        name="nsa_window_prompt" if window else "nsa_selected_prompt",
    )(*args)


def _diff_lambda(lam_ref, lam_init):
    lp = lam_ref[...]
    s1 = jnp.sum(lp[0:1] * lp[1:2], -1, keepdims=True)
    s2 = jnp.sum(lp[2:3] * lp[3:4], -1, keepdims=True)
    return jnp.exp(s1) - jnp.exp(s2) + lam_init


def _diff_flash_prompt_kernel(q_ref, k_ref, v_ref, bias_ref, lam_ref, hg_ref, o_ref, qs_scr, *, t, lam_init):
    qi = pl.program_id(2)
    pieces = t // LANES
    maps = 2 * DF_HEADS_PER_STEP
    q = q_ref[0]
    for c in range(maps):
        qs_scr[c] = (q[:, c * DF_DK:(c + 1) * DF_DK] * (DF_DK ** -0.5)).astype(BF16)
    def span(start, deltas, state):
        start = start if isinstance(start, int) else pl.multiple_of(start, t)
        width = len(deltas) * t
        k = k_ref[0, pl.ds(start, width), :]
        v = v_ref[0, pl.ds(start, width), :]
        ones = jnp.ones((width, LANES), BF16)
        new = []
        for c in range(maps):
            v_aug = jnp.concatenate([v[:, (c // 2) * DF_DV:(c // 2 + 1) * DF_DV].astype(BF16), ones], axis=1)
            s = _dot_nt(qs_scr[c], k[:, c * DF_DK:(c + 1) * DF_DK].astype(BF16))
            ps = []
            for j, delta in enumerate(deltas):
                for i in range(pieces):
                    pc = s[:, j * t + i * LANES:j * t + (i + 1) * LANES]
                    if delta is not None:
                        pc = pc + bias_ref[c, delta, :, i * LANES:(i + 1) * LANES]
                    ps.append(pc)
            new += _flash_step(ps, v_aug, *state[3 * c:3 * c + 3])
        return tuple(new)

    init = (jnp.full((t, LANES), M_FLOOR, F32), jnp.zeros((t, LANES), F32), jnp.zeros((t, DF_DV), F32)) * maps
    n_far = jnp.maximum(qi - 1, 0)
    state = lax.fori_loop(0, n_far // 2, lambda ki, st: span(ki * (2 * t), [None, None], st), init)
    state = lax.cond(n_far % 2 == 1, lambda st: span((n_far - 1) * t, [None], st), lambda st: st, state)
    state = lax.cond(qi >= 1, lambda st: span(n_far * t, [1, 0], st), lambda st: span(0, [0], st), state)

    lam = _diff_lambda(lam_ref, lam_init)
    for hh in range(DF_HEADS_PER_STEP):
        (_, l0, a0), (_, l1, a1) = state[6 * hh:6 * hh + 3], state[6 * hh + 3:6 * hh + 6]
        o = a0 / jnp.maximum(l0, 1e-30) - lam * (a1 / jnp.maximum(l1, 1e-30))
        o = o * lax.rsqrt(jnp.mean(o * o, -1, keepdims=True) + EPS) * hg_ref[hh]
        o_ref[0, :, hh * DF_DV:(hh + 1) * DF_DV] = o * (1.0 - lam_init)


def diff_flash_prompt(q, kv, lam_p, head_g, rel_bias, lam_init):
    n, seq, _ = q.shape
    t = min(DF_TILE, seq)
    nq = seq // t
    hps = DF_HEADS_PER_STEP
    width = hps * DF_DV
    assert t % LANES == 0 and _far_distance_ok() and t + 1 >= FAR_DISTANCE and DF_HEADS % hps == 0
    tiles = _toeplitz_tiles(rel_bias, t, 2, rel_to_far=True)
    return pl.pallas_call(
        functools.partial(_diff_flash_prompt_kernel, t=t, lam_init=lam_init),
        grid=(n, DF_HEADS // hps, nq),
        in_specs=[pl.BlockSpec((1, t, width), lambda b, h, qi: (b, qi, h)),
                  pl.BlockSpec((1, seq, width), lambda b, h, qi: (b, 0, h)),
                  pl.BlockSpec((1, seq, width), lambda b, h, qi: (b, 0, DF_HEADS // hps + h)),
                  pl.BlockSpec((2 * hps, 2, t, t), lambda b, h, qi: (h, 0, 0, 0)),
                  pl.BlockSpec((4, DF_DK), lambda b, h, qi: (0, 0)),
                  pl.BlockSpec((hps, 1, DF_DV), lambda b, h, qi: (h, 0, 0))],
        out_specs=pl.BlockSpec((1, t, width), lambda b, h, qi: (b, qi, h)),
        out_shape=jax.ShapeDtypeStruct((n, seq, D_MODEL), F32),
        scratch_shapes=[pltpu.VMEM((2 * hps, t, DF_DK), BF16)],
        compiler_params=_cparams("parallel", "parallel", "arbitrary"),
        name="diff_flash_prompt",
    )(q, kv, kv, tiles, lam_p, head_g.reshape(DF_HEADS, 1, DF_DV))


def _mlstm_kernel(*refs, c, has_init):
    if has_init:
        (q_ref, k_ref, v_ref, gc_ref, gr_ref, og_ref, bc_ref, br_ref, hg_ref, c0_ref, n0_ref, m0_ref,
         a_ref, cout_ref, nout_ref, mout_ref, c_scr, n_scr, m_scr) = refs
    else:
        (q_ref, k_ref, v_ref, gc_ref, gr_ref, og_ref, bc_ref, br_ref, hg_ref,
         a_ref, cout_ref, nout_ref, mout_ref, c_scr, n_scr, m_scr) = refs
    j = pl.program_id(1)

    @pl.when(j == 0)
    def _():
        if has_init:
            c_scr[...] = c0_ref[0]
            n_scr[...] = n0_ref[0]
            m_scr[...] = m0_ref[0]
        else:
            c_scr[...] = jnp.zeros(c_scr.shape, F32)
            n_scr[...] = jnp.zeros(n_scr.shape, F32)
            m_scr[...] = jnp.zeros(m_scr.shape, F32)

    row = lax.broadcasted_iota(jnp.int32, (c, c), 0)
    col = lax.broadcasted_iota(jnp.int32, (c, c), 1)
    tri = row >= col
    gc = gc_ref[0] + bc_ref[...]
    gr = gr_ref[0] + br_ref[...]
    f_cols = _dot_exact(jnp.where(tri, 1.0, 0.0), _log_sigmoid(gc))
    f_rows = _dot_exact(_log_sigmoid(gr), jnp.where(row <= col, 1.0, 0.0))
    og = og_ref[0]
    for h in range(ML_HEADS):
        qc = q_ref[0, :, h * ML_DK:(h + 1) * ML_DK]
        kc = k_ref[0, :, h * ML_DK:(h + 1) * ML_DK] * (ML_DK ** -0.5)
        vc = v_ref[0, :, h * ML_DV:(h + 1) * ML_DV]
        q_bf, k_bf = qc.astype(BF16), kc.astype(BF16)
        f_col = f_cols[:, ML_HEADS + h:ML_HEADS + h + 1]
        f_row = f_rows[ML_HEADS + h:ML_HEADS + h + 1, :]
        i_col = gc[:, h:h + 1]
        i_row = gr[h:h + 1, :]
        m_prev = m_scr[h:h + 1, 0:1]
        dm = jnp.where(tri, f_col - f_row + i_row, -1e30)
        inter = f_col + m_prev
        mt = jnp.maximum(jnp.max(dm, -1, keepdims=True), inter)
        w = jnp.where(tri, jnp.exp(dm - mt), 0.0)
        s = _dot_nt(q_bf, k_bf) * w
        a = jnp.exp(inter - mt)
        c_old = c_scr[h]
        n_old = n_scr[h:h + 1, :]
        num = _dot(s.astype(BF16), vc.astype(BF16)) + a * _dot_nt(q_bf, c_old.astype(BF16))
        den = jnp.sum(s, -1, keepdims=True) + a * jnp.sum(qc * n_old, -1, keepdims=True)
        hh = num / jnp.maximum(jnp.abs(den), jnp.exp(-mt))
        m_new = mt[c - 1:c, :]
        wend = jnp.exp(f_col[c - 1:c, :] - f_col + i_col - m_new)
        decay = jnp.exp(inter[c - 1:c, :] - m_new)
        c_scr[h] = decay * c_old + _dot_tn((wend * vc).astype(BF16), k_bf)
        n_scr[h:h + 1, :] = decay * n_old + jnp.sum(wend * kc, 0, keepdims=True)
        m_scr[h:h + 1, :] = jnp.broadcast_to(m_new, (1, LANES))
        hn = hh * lax.rsqrt(jnp.mean(hh * hh, -1, keepdims=True) + EPS) * hg_ref[h:h + 1, :]
        a_ref[0, :, h * ML_DV:(h + 1) * ML_DV] = _sigmoid(og[:, h * ML_DV:(h + 1) * ML_DV]) * hn

    @pl.when(j == pl.num_programs(1) - 1)
    def _():
        cout_ref[0] = c_scr[...]
        nout_ref[0] = n_scr[...]
        mout_ref[0] = m_scr[...]


def mlstm_cell(q, k, v, gates, og, b_if, head_g, state):
    n, seq, _ = q.shape
    c = min(LANES, seq)
    nc = seq // c
    has_init = state is not None
    gates_t = jnp.swapaxes(gates[:, :, :2 * ML_HEADS], 1, 2)
    b_flat = b_if.reshape(2 * ML_HEADS)
    b_col = jnp.pad(b_flat, (0, LANES - 2 * ML_HEADS)).reshape(1, LANES)
    b_row = b_flat.reshape(2 * ML_HEADS, 1)
    tok = lambda w: pl.BlockSpec((1, c, w), lambda b, j: (b, j, 0))
    fix2 = lambda s: pl.BlockSpec(s, lambda b, j: (0, 0))
    in_specs = [tok(ML_QK), tok(ML_QK), tok(D_MODEL), tok(LANES),
                pl.BlockSpec((1, 2 * ML_HEADS, c), lambda b, j: (b, 0, j)), tok(D_MODEL),
                fix2((1, LANES)), fix2((2 * ML_HEADS, 1)), fix2((ML_HEADS, ML_DV))]
    args = [q, k, v, gates, gates_t, og, b_col, b_row, head_g]
    st4 = pl.BlockSpec((1, ML_HEADS, ML_DV, ML_DK), lambda b, j: (b, 0, 0, 0))
    st3 = pl.BlockSpec((1, ML_HEADS, LANES), lambda b, j: (b, 0, 0))
    if has_init:
        c0, n0, m0 = state
        in_specs += [st4, st3, st3]
        args += [c0, n0, jnp.broadcast_to(m0[:, :, None], (n, ML_HEADS, LANES))]
    a, c_new, n_new, m_new = pl.pallas_call(
        functools.partial(_mlstm_kernel, c=c, has_init=has_init),
        grid=(n, nc),
        in_specs=in_specs,
        out_specs=[tok(D_MODEL), st4, st3, st3],
        out_shape=[jax.ShapeDtypeStruct((n, seq, D_MODEL), F32),
                   jax.ShapeDtypeStruct((n, ML_HEADS, ML_DV, ML_DK), F32),
                   jax.ShapeDtypeStruct((n, ML_HEADS, ML_DK), F32),
                   jax.ShapeDtypeStruct((n, ML_HEADS, LANES), F32)],
        scratch_shapes=[pltpu.VMEM((ML_HEADS, ML_DV, ML_DK), F32), pltpu.VMEM((ML_HEADS, ML_DK), F32),
                        pltpu.VMEM((ML_HEADS, LANES), F32)],
        compiler_params=_cparams("parallel", "arbitrary"),
        name="mlstm",
    )(*args)
    return a, c_new, n_new, m_new[:, :, 0]


def _conv_kernel(a_ref, b_ref, buf_ref, w_ref, dwb_ref, lng_ref, lnb_ref, o_ref, nbuf_ref, ctx_scr, *, tt):
    j = pl.program_id(1)

    @pl.when(j == 0)
    def _():
        ctx_scr[0:CONV_HALO, :] = buf_ref[0]

    @pl.when(j > 0)
    def _():
        ctx_scr[0:CONV_HALO, :] = ctx_scr[tt:tt + CONV_HALO, :]

    ctx_scr[CONV_HALO:CONV_HALO + tt, :] = a_ref[0] * _sigmoid(b_ref[0])
    off = CONV_HALO - (CONV_WIDTH - 1)
    acc = jnp.zeros((tt, D_MODEL), F32) + dwb_ref[...]
    sub = 8
    for r in range(sub):
        lead = off + r
        base, shift = (lead // sub) * sub, lead % sub
        rows = tt if shift == 0 else tt + sub
        part = None
        for tap in range(r, CONV_WIDTH, sub):
            lo = base + tap - r
            term = ctx_scr[lo:lo + rows, :] * w_ref[tap:tap + 1, :]
            part = term if part is None else part + term
        acc = acc + part[shift:shift + tt, :]
    mu = jnp.mean(acc, -1, keepdims=True)
    xc = acc - mu
    var = jnp.mean(xc * xc, -1, keepdims=True)
    o_ref[0] = _silu(xc * lax.rsqrt(var + EPS) * lng_ref[...] + lnb_ref[...])
    nbuf_ref[0] = ctx_scr[tt:tt + CONV_HALO, :]


def conv_block(a, b, buf, dw_w, dw_b, ln_g, ln_b):
    n, seq, d = a.shape
    tt = min(ATT_TILE, seq)
    assert seq == tt or tt >= CONV_HALO
    hist = CONV_WIDTH - 1
    if buf is None:
        buf_pad = jnp.zeros((n, CONV_HALO, d), F32)
    else:
        buf_pad = jnp.pad(buf, ((0, 0), (CONV_HALO - hist, 0), (0, 0)))
    w_pad = jnp.pad(dw_w, ((0, CONV_HALO - CONV_WIDTH), (0, 0)))
    tok = pl.BlockSpec((1, tt, d), lambda bb, j: (bb, j, 0))
    vec = pl.BlockSpec((1, d), lambda bb, j: (0, 0))
    halo = pl.BlockSpec((1, CONV_HALO, d), lambda bb, j: (bb, 0, 0))
    out, nbuf = pl.pallas_call(
        functools.partial(_conv_kernel, tt=tt),
        grid=(n, seq // tt),
        in_specs=[tok, tok, halo, pl.BlockSpec((CONV_HALO, d), lambda bb, j: (0, 0)), vec, vec, vec],
        out_specs=[tok, halo],
        out_shape=[jax.ShapeDtypeStruct((n, seq, d), F32), jax.ShapeDtypeStruct((n, CONV_HALO, d), F32)],
        scratch_shapes=[pltpu.VMEM((CONV_HALO + tt, d), F32)],
        compiler_params=_cparams("parallel", "arbitrary"),
        name="conv_block",
    )(a, b, buf_pad, w_pad, dw_b.reshape(1, d), ln_g.reshape(1, d), ln_b.reshape(1, d))
    return out, nbuf[:, CONV_HALO - hist:, :]


def _page_spec(block, col_block, j):
    return pl.BlockSpec(block, lambda b, pt: (pt[b, j], 0, col_block))


def _page_means_kernel(pt_ref, *refs):
    page_refs, o_ref = refs[:-1], refs[-1]
    for j, page_ref in enumerate(page_refs):
        x = page_ref[0]
        blocks = x.shape[0] // NSA_BLOCK
        o_ref[0, j * blocks:(j + 1) * blocks, :] = (
            jnp.sum(x.reshape(blocks, NSA_BLOCK, x.shape[1]), axis=1) * (1.0 / NSA_BLOCK))


def nsa_page_means(pool, page_table):
    n, n_pages = page_table.shape
    page = pool.shape[1]
    nbp = n_pages * page // NSA_BLOCK
    return pl.pallas_call(
        _page_means_kernel,
        grid_spec=pltpu.PrefetchScalarGridSpec(
            num_scalar_prefetch=1, grid=(n,),
            in_specs=[_page_spec((1, page, 2 * NSA_KVW), 0, j) for j in range(n_pages)],
            out_specs=pl.BlockSpec((1, nbp, 2 * NSA_KVW), lambda b, pt: (b, 0, 0))),
        out_shape=jax.ShapeDtypeStruct((n, nbp, 2 * NSA_KVW), F32),
        compiler_params=_cparams("parallel"),
        name="nsa_page_means",
    )(page_table, *([pool] * n_pages))


def _stack_group_queries(q, kh):
    parts = [q[:, (kh * NSA_GROUP + g) * NSA_DH:(kh * NSA_GROUP + g + 1) * NSA_DH] for g in range(NSA_GROUP)]
    return (jnp.concatenate(parts, axis=0) * (NSA_DH ** -0.5)).astype(BF16)


def _nsa_sample_local_kernel(q_ref, means_ref, rows_ref, win_ref, neww_ref, g_ref, bc_ref, bw_ref,
                             oc_ref, ow_ref, selb_ref, wout_ref, mk_scr, *, past, seq, k_sel):
    nbp = means_ref.shape[1]
    wb = win_ref.shape[1]
    mk_scr[...] = jnp.zeros(mk_scr.shape, F32)
    mk_scr[0:nbp, :] = means_ref[0]
    mk_scr[nbp:nbp + 1, :] = jnp.sum(rows_ref[0][:, 0:2 * NSA_KVW], 0, keepdims=True) * (1.0 / NSA_BLOCK)
    neww = neww_ref[0]
    wout_ref[0, 0:wb - seq, :] = win_ref[0, seq:wb, :]
    wout_ref[0, wb - seq:wb, :] = neww
    new_pad = jnp.concatenate([neww, jnp.zeros((LANES - seq, 2 * NSA_KVW), F32)], axis=0)
    q = q_ref[0]
    gate = _sigmoid(g_ref[0])
    imps = []
    for kh in range(NSA_KV_HEADS):
        qs = _stack_group_queries(q, kh)
        lo, hi = kh * NSA_DH, (kh + 1) * NSA_DH
        p = _softmax_rows(_dot_nt(qs, mk_scr[:, lo:hi].astype(BF16)) + bc_ref[kh])
        oc = _dot(p.astype(BF16), mk_scr[:, NSA_KVW + lo:NSA_KVW + hi].astype(BF16))
        imp = p[0:seq]
        for g in range(1, NSA_GROUP):
            imp = imp + p[g * seq:(g + 1) * seq]
        imps.append(imp)
        s_old = _dot_nt(qs, win_ref[0, :, lo:hi].astype(BF16)) + bw_ref[kh, :, 0:wb]
        s_new = _dot_nt(qs, new_pad[:, lo:hi].astype(BF16)) + bw_ref[kh, :, wb:wb + LANES]
        mw = jnp.maximum(jnp.maximum(jnp.max(s_old, -1, keepdims=True), jnp.max(s_new, -1, keepdims=True)), M_FLOOR)
        p_old, p_new = jnp.exp(s_old - mw), jnp.exp(s_new - mw)
        lw = jnp.sum(p_old, -1, keepdims=True) + jnp.sum(p_new, -1, keepdims=True)
        ow = (_dot(p_old.astype(BF16), win_ref[0, :, NSA_KVW + lo:NSA_KVW + hi].astype(BF16))
              + _dot(p_new.astype(BF16), new_pad[:, NSA_KVW + lo:NSA_KVW + hi].astype(BF16))) / jnp.maximum(lw, 1e-30)
        for g in range(NSA_GROUP):
            h = kh * NSA_GROUP + g
            oc_ref[0, :, h * NSA_DH:(h + 1) * NSA_DH] = oc[g * seq:(g + 1) * seq] * gate[:, h:h + 1]
            ow_ref[0, :, h * NSA_DH:(h + 1) * NSA_DH] = (
                ow[g * seq:(g + 1) * seq] * gate[:, 2 * NSA_HEADS + h:2 * NSA_HEADS + h + 1])
    rows_sel = NSA_KV_HEADS * seq
    imp_all = jnp.concatenate(imps + [jnp.zeros((LANES - rows_sel, LANES), F32)], axis=0)
    pos = past + lax.broadcasted_iota(jnp.int32, (1, LANES), 1) % seq
    sel = _select_blocks(imp_all, (pos // NSA_BLOCK).astype(F32), k_sel)
    for kh in range(NSA_KV_HEADS):
        selb_ref[0, kh] = sel[kh * seq:(kh + 1) * seq]


def _group_rows(tab):
    return tab.reshape(NSA_KV_HEADS, NSA_GROUP * tab.shape[1], tab.shape[2])


def nsa_sample_local(q, means, rows, win_buf, new_win, gates, rel_bias, past):
    n, seq, _ = q.shape
    nbp = means.shape[1]
    wb = win_buf.shape[1]
    nb = -(-(past + seq) // NSA_BLOCK)
    assert nbp + 1 == nb and nb <= LANES and seq % 8 == 0 and NSA_KV_HEADS * seq <= LANES
    lw = wb + LANES
    assert wb % LANES == 0 and seq <= LANES
    qpos = past + jnp.arange(seq)[:, None]
    bend = jnp.arange(LANES)[None, :] * NSA_BLOCK + NSA_BLOCK - 1
    bc = _group_rows(_bias_table(rel_bias, qpos - bend, bend <= qpos))
    cidx = jnp.arange(lw)[None, :]
    wpos = past - wb + cidx
    dw = qpos - wpos
    bw = _group_rows(_bias_table(rel_bias, dw, (dw >= 0) & (dw <= NSA_WINDOW) & (wpos >= 0) & (cidx < wb + seq)))
    rows_q = NSA_GROUP * seq
    per_b = lambda s: pl.BlockSpec((1,) + s, lambda b: (b,) + (0,) * len(s))
    fix = lambda s: pl.BlockSpec(s, lambda b: (0,) * len(s))
    return pl.pallas_call(
        functools.partial(_nsa_sample_local_kernel, past=past, seq=seq, k_sel=min(NSA_TOPK, nb)),
        grid=(n,),
        in_specs=[per_b((seq, D_MODEL)), per_b((nbp, 2 * NSA_KVW)), per_b((seq, D_MODEL)), per_b((wb, 2 * NSA_KVW)),
                  per_b((seq, 2 * NSA_KVW)), per_b((seq, LANES)), fix((NSA_KV_HEADS, rows_q, LANES)),
                  fix((NSA_KV_HEADS, rows_q, lw))],
        out_specs=[per_b((seq, D_MODEL)), per_b((seq, D_MODEL)), per_b((NSA_KV_HEADS, seq, LANES)),
                   per_b((wb, 2 * NSA_KVW))],
        out_shape=[jax.ShapeDtypeStruct((n, seq, D_MODEL), F32), jax.ShapeDtypeStruct((n, seq, D_MODEL), F32),
                   jax.ShapeDtypeStruct((n, NSA_KV_HEADS, seq, LANES), F32),
                   jax.ShapeDtypeStruct((n, wb, 2 * NSA_KVW), F32)],
        scratch_shapes=[pltpu.VMEM((LANES, 2 * NSA_KVW), F32)],
        compiler_params=_cparams("parallel"),
        name="nsa_sample_local",
    )(q, means, rows, win_buf, new_win, gates, bc, bw)


def _nsa_sample_selected_kernel(pt_ref, q_ref, rows_ref, selb_ref, bias_ref, g_ref, *refs, seq, page, past):
    n_pages = len(refs) - 2
    page_refs, o_ref, new_scr = refs[:n_pages], refs[n_pages], refs[n_pages + 1]
    new_scr[...] = jnp.zeros(new_scr.shape, F32)
    new_scr[0:seq, :] = rows_ref[0][:, 2 * NSA_KVW:4 * NSA_KVW]
    q = q_ref[0]
    gate = _sigmoid(g_ref[0])
    sources = [(page_refs[j], j * page) for j in range(n_pages)] + [(None, past)]
    expands = [_block_expand(first_pos, page) for _, first_pos in sources]
    for kh in range(NSA_KV_HEADS):
        lo, hi = kh * NSA_DH, (kh + 1) * NSA_DH
        qs = _stack_group_queries(q, kh)
        sel = selb_ref[0, kh].astype(BF16)
        scores = []
        for j, (ref, _) in enumerate(sources):
            k_bf = (new_scr[:, lo:hi] if ref is None else ref[0, :, lo:hi]).astype(BF16)
            sb = _dot(sel, expands[j])
            scores.append(_dot_nt(qs, k_bf) + bias_ref[j, kh] + jnp.concatenate([sb] * NSA_GROUP, axis=0))
        mx = scores[0]
        for s in scores[1:]:
            mx = jnp.maximum(mx, s)
        m = jnp.maximum(jnp.max(mx, -1, keepdims=True), M_FLOOR)
        total = None
        acc = None
        for j, (ref, _) in enumerate(sources):
            p = jnp.exp(scores[j] - m)
            v_bf = (new_scr[:, NSA_KVW + lo:NSA_KVW + hi] if ref is None
                    else ref[0, :, NSA_KVW + lo:NSA_KVW + hi]).astype(BF16)
            pv = _dot(p.astype(BF16), v_bf)
            total = p if total is None else total + p
            acc = pv if acc is None else acc + pv
        o = acc / jnp.maximum(jnp.sum(total, -1, keepdims=True), 1e-30)
        for g in range(NSA_GROUP):
            h = kh * NSA_GROUP + g
            o_ref[0, :, h * NSA_DH:(h + 1) * NSA_DH] = (
                o[g * seq:(g + 1) * seq] * gate[:, NSA_HEADS + h:NSA_HEADS + h + 1])


def nsa_sample_selected(q, pool, page_table, rows, selb, gates, rel_bias, past):
    n, seq, _ = q.shape
    n_pages = page_table.shape[1]
    page = pool.shape[1]
    assert page == LANES and seq <= page
    qpos = past + jnp.arange(seq)[:, None]
    j = jnp.arange(page)[None, :]
    d_past = qpos[None] - (jnp.arange(n_pages)[:, None, None] * page + j[None])
    d_new = (qpos - past) - j
    dist = jnp.concatenate([d_past, d_new[None]], 0)
    valid = jnp.concatenate([d_past >= 0, ((d_new >= 0) & (j < seq))[None]], 0)
    bias = _bias_table(rel_bias, dist, valid)
    bias = jnp.moveaxis(bias, 1, 0).reshape(n_pages + 1, NSA_KV_HEADS, NSA_GROUP * seq, page)
    rows_q = NSA_GROUP * seq
    per_b = lambda s: pl.BlockSpec((1,) + s, lambda b, pt: (b,) + (0,) * len(s))
    return pl.pallas_call(
        functools.partial(_nsa_sample_selected_kernel, seq=seq, page=page, past=past),
        grid_spec=pltpu.PrefetchScalarGridSpec(
            num_scalar_prefetch=1, grid=(n,),
            in_specs=[per_b((seq, D_MODEL)), per_b((seq, D_MODEL)), per_b((NSA_KV_HEADS, seq, LANES)),
                      pl.BlockSpec((n_pages + 1, NSA_KV_HEADS, rows_q, page), lambda b, pt: (0, 0, 0, 0)),
                      per_b((seq, LANES))]
            + [_page_spec((1, page, 2 * NSA_KVW), 1, jj) for jj in range(n_pages)],
            out_specs=per_b((seq, D_MODEL)),
            scratch_shapes=[pltpu.VMEM((page, 2 * NSA_KVW), F32)]),
        out_shape=jax.ShapeDtypeStruct((n, seq, D_MODEL), F32),
        compiler_params=_cparams("parallel"),
        name="nsa_sample_selected",
    )(page_table, q, rows, selb, bias, gates, *([pool] * n_pages))


def _diff_sample_kernel(pt_ref, q_ref, new_ref, tfar_ref, tnear_ref, tnew_ref, lam_ref, hg_ref, *refs,
                        seq, lam_init, pps, n_near):
    k_refs, v_refs = refs[:pps], refs[pps:2 * pps]
    o_ref, qall_scr, m_scr, l_scr, acc_scr = refs[2 * pps:]
    g = pl.program_id(1)
    last = pl.num_programs(1) - 1
    rows_k = k_refs[0].shape[0] * DF_HEADS
    pieces = rows_k // LANES

    @pl.when(g == 0)
    def _():
        m_scr[...] = jnp.full(m_scr.shape, M_FLOOR, F32)
        l_scr[...] = jnp.zeros(l_scr.shape, F32)
        acc_scr[...] = jnp.zeros(acc_scr.shape, F32)
        q = q_ref[0] * (DF_DK ** -0.5)
        half = lax.broadcasted_iota(jnp.int32, (seq, DF_DV), 1) // DF_DK
        parts = []
        for h in range(DF_HEADS):
            qh = q[:, h * DF_DV:(h + 1) * DF_DV]
            parts += [jnp.where(half == c, qh, 0.0) for c in range(2)]
        qall_scr[...] = jnp.concatenate(parts, axis=0).astype(BF16)

    qall = qall_scr[...]
    ppc = math.gcd(pps, DF_PAGES_PER_CHAIN)
    for j0 in range(0, pps, ppc):
        pages = range(j0, j0 + ppc)
        kf = jnp.concatenate([k_refs[jj][...].reshape(rows_k, DF_DV) for jj in pages], axis=0).astype(BF16)
        vf = jnp.concatenate([v_refs[jj][...].reshape(rows_k, DF_DV) for jj in pages], axis=0).astype(BF16)
        s = _dot_nt(qall, kf)
        ps = []
        for a, jj in enumerate(pages):
            near = jj - (pps - n_near)
            for i in range(pieces):
                tab = tfar_ref[:, i * LANES:(i + 1) * LANES]
                if near >= 0:
                    tab = jnp.where(g == last, tnear_ref[near, :, i * LANES:(i + 1) * LANES], tab)
                ps.append(s[:, a * rows_k + i * LANES:a * rows_k + (i + 1) * LANES] + tab)
        _flash_update(ps, vf, m_scr, acc_scr, slice(None), l_ref=l_scr)

    @pl.when(g == last)
    def _():
        new = new_ref[0]
        pad = jnp.zeros((LANES - DF_HEADS * seq, DF_DV), F32)
        kn = jnp.concatenate([new[:, h * DF_DV:(h + 1) * DF_DV] for h in range(DF_HEADS)] + [pad], axis=0)
        vn = jnp.concatenate([new[:, D_MODEL + h * DF_DV:D_MODEL + (h + 1) * DF_DV] for h in range(DF_HEADS)]
                             + [pad], axis=0)
        s = _dot_nt(qall, kn.astype(BF16)) + tnew_ref[...]
        _flash_update([s], vn.astype(BF16), m_scr, acc_scr, slice(None), l_ref=l_scr)
        lam = _diff_lambda(lam_ref, lam_init)
        acc = acc_scr[...] / jnp.maximum(l_scr[...], 1e-30)
        for h in range(DF_HEADS):
            r0 = 2 * h * seq
            o = acc[r0:r0 + seq] - lam * acc[r0 + seq:r0 + 2 * seq]
            o = o * lax.rsqrt(jnp.mean(o * o, -1, keepdims=True) + EPS) * hg_ref[:, h * DF_DV:(h + 1) * DF_DV]
            o_ref[0, :, h * DF_DV:(h + 1) * DF_DV] = o * (1.0 - lam_init)


def diff_sample(q, pool6, j_layer, page_table, kv_new, lam_p, head_g, rel_bias, past, lam_init):
    n, seq, _ = q.shape
    n_pages = page_table.shape[1]
    page = pool6.shape[2]
    pps = min(DF_PAGES_PER_STEP, n_pages)
    rows_q = 2 * DF_HEADS * seq
    rows_k = page * DF_HEADS
    assert n_pages % pps == 0 and rows_q == LANES and DF_HEADS * seq <= LANES and _far_distance_ok()
    n_near = sum(1 for p in range(n_pages) if past - (p * page + page - 1) < FAR_DISTANCE)
    assert 1 <= n_near <= pps
    head_of_row = jnp.arange(rows_q) // (2 * seq)
    same_head = head_of_row[:, None] == (jnp.arange(rows_k) % DF_HEADS)[None, :]
    far = jnp.repeat(rel_bias[N_BUCKETS - 1], seq)[:, None]
    t_far = jnp.where(same_head, far, MASKED).astype(F32)
    qpos = past + jnp.arange(seq)[:, None]
    near_pages = jnp.arange(n_pages - n_near, n_pages)
    d_near = qpos[None] - (near_pages[:, None, None] * page + jnp.arange(page)[None, None, :])
    b_near = _bias_table(rel_bias, d_near, d_near >= 0)
    b_near = jnp.moveaxis(b_near, 1, 0).reshape(n_near, rows_q, page)
    b_near = jnp.broadcast_to(b_near[..., None], (n_near, rows_q, page, DF_HEADS)).reshape(n_near, rows_q, rows_k)
    t_near = jnp.where(same_head[None], b_near, MASKED).astype(F32)
    d_new = jnp.arange(seq)[:, None] - jnp.arange(seq)[None, :]
    b_new = _bias_table(rel_bias, d_new, d_new >= 0).reshape(rows_q, 1, seq)
    b_new = jnp.broadcast_to(b_new, (rows_q, DF_HEADS, seq)).reshape(rows_q, DF_HEADS * seq)
    same_new = head_of_row[:, None] == (jnp.arange(DF_HEADS * seq) // seq)[None, :]
    t_new = jnp.pad(jnp.where(same_new, b_new, MASKED), ((0, 0), (0, LANES - DF_HEADS * seq)),
                    constant_values=MASKED).astype(F32)

    def page_spec(jj, c):
        return pl.BlockSpec((None, None, page, None, DF_HEADS, DF_DV),
                            lambda b, g, pt: (j_layer, pt[b, g * pps + jj], 0, c, 0, 0))

    fix = lambda s: pl.BlockSpec(s, lambda b, g, pt: (0,) * len(s))
    return pl.pallas_call(
        functools.partial(_diff_sample_kernel, seq=seq, lam_init=lam_init, pps=pps, n_near=n_near),
        grid_spec=pltpu.PrefetchScalarGridSpec(
            num_scalar_prefetch=1, grid=(n, n_pages // pps),
            in_specs=[pl.BlockSpec((1, seq, D_MODEL), lambda b, g, pt: (b, 0, 0)),
                      pl.BlockSpec((1, seq, 2 * D_MODEL), lambda b, g, pt: (b, 0, 0)),
                      fix((rows_q, rows_k)), fix((n_near, rows_q, rows_k)), fix((rows_q, LANES)),
                      fix((4, DF_DK)), fix((1, D_MODEL))]
            + [page_spec(jj, 0) for jj in range(pps)] + [page_spec(jj, 1) for jj in range(pps)],
            out_specs=pl.BlockSpec((1, seq, D_MODEL), lambda b, g, pt: (b, 0, 0)),
            scratch_shapes=[pltpu.VMEM((rows_q, DF_DV), BF16), pltpu.VMEM((rows_q, LANES), F32),
                            pltpu.VMEM((rows_q, LANES), F32), pltpu.VMEM((rows_q, DF_DV), F32)]),
        out_shape=jax.ShapeDtypeStruct((n, seq, D_MODEL), F32),
        compiler_params=_cparams("parallel", "arbitrary"),
        name="diff_sample",
    )(page_table, q, kv_new, t_far, t_near, t_new, lam_p, head_g.reshape(1, D_MODEL), *([pool6] * (2 * pps)))


def _trunk(x, p, past, page_table, nsa_pool, nsa_win, ml_c, ml_n, ml_m, conv_buf, diff_pool, weights):
    (norm_g, final_norm_g, rel_bias, ple_w, ple_gate_w, ple_norm_g,
     nsa_w_in, nsa_w_out, mlstm_w_in, mlstm_b_if, mlstm_head_g, mlstm_w_out,
     conv_w_in, conv_dw_w, conv_dw_b, conv_ln_g, conv_ln_b, conv_w_out,
     diff_w_in, diff_lam, diff_head_g, diff_w_out) = weights
    n, seq, d = x.shape
    m = n * seq
    sample = page_table is not None
    x2 = x.reshape(m, d)
    depth = norm_g.shape[0]
    out = {}
    for i in range(depth):
        kind, j = i % 4, i // 4
        assert j == 0
        p_i = p[i].reshape(m, D_PLE)
        tail = dict(x=x2, p=p_i, ple_norm_g=ple_norm_g[i], ple_gate_w=ple_gate_w[i], ple_w=ple_w[i],
                    final_g=final_norm_g if i == depth - 1 else None)
        if kind == 0:
            w = nsa_w_in[j]
            c0, c1, c2, c3 = D_MODEL, 2 * D_MODEL, 2 * D_MODEL + 2 * NSA_KVW, 2 * D_MODEL + 2 * NSA_KVW + 3 * NSA_HEADS
            q, rows, win, gates, z = norm_linear(x2, norm_g[i], [w[:, :c0], w[:, c0:c1], w[:, c1:c2], w[:, c2:c3], w[:, c3:]])
            q3, rows3, win3, gates3 = (t.reshape(n, seq, -1) for t in (q, rows, win, gates))
            if sample:
                pool = nsa_pool[j].reshape(nsa_pool.shape[1], nsa_pool.shape[2], D_MODEL)
                wb = nsa_win.shape[2]
                means = nsa_page_means(pool, page_table)
                oc, ow, selb, win_out = nsa_sample_local(q3, means, rows3, nsa_win[j].reshape(n, wb, 2 * NSA_KVW),
                                                         win3, gates3, rel_bias, past)
                os_ = nsa_sample_selected(q3, pool, page_table, rows3, selb, gates3, rel_bias, past)
            else:
                oc, selb = nsa_cmp_prompt(q3, rows3, gates3, rel_bias)
                os_ = nsa_flash_prompt(q3, rows3, 4, 6, gates3, rel_bias, selb, window=False)
                ow = nsa_flash_prompt(q3, win3, 0, 2, gates3, rel_bias, None, window=True)
                keep = min(NSA_WINDOW, seq)
                win_out = win3[:, seq - keep:]
                if keep < NSA_WINDOW:
                    win_out = jnp.concatenate([jnp.zeros((n, NSA_WINDOW - keep, 2 * NSA_KVW), F32), win_out], 1)
            out['kv'] = rows3.reshape(1, n, seq, 4, NSA_KV_HEADS, NSA_DH)
            out['win'] = win_out.reshape(1, n, win_out.shape[1], 2, NSA_KV_HEADS, NSA_DH)
            x2 = post_layer([t.reshape(m, d) for t in (oc, os_, ow)], z, w_out=nsa_w_out[j], **tail)
        elif kind == 1:
            w = mlstm_w_in[j]
            c0, c1, c2 = ML_QK, 2 * ML_QK, 2 * ML_QK + D_MODEL
            c3, c4 = c2 + 2 * ML_HEADS, c2 + 2 * ML_HEADS + D_MODEL
            q, k, v, gates, og, z = norm_linear(
                x2, norm_g[i], [w[:, :c0], w[:, c0:c1], w[:, c1:c2], w[:, c2:c3], w[:, c3:c4], w[:, c4:]])
            state = (ml_c[j], ml_n[j], ml_m[j]) if sample else None
            a, c_new, n_new, m_new = mlstm_cell(*(t.reshape(n, seq, -1) for t in (q, k, v, gates, og)),
                                                mlstm_b_if[j], mlstm_head_g[j], state)
            out['C'], out['n'], out['m'] = c_new[None], n_new[None], m_new[None]
            x2 = post_layer([a.reshape(m, d)], z, w_out=mlstm_w_out[j], **tail)
        elif kind == 2:
            w = conv_w_in[j]
            a, b, z = norm_linear(x2, norm_g[i], [w[:, :d], w[:, d:2 * d], w[:, 2 * d:]])
            cact, nbuf = conv_block(a.reshape(n, seq, d), b.reshape(n, seq, d), conv_buf[j] if sample else None,
                                    conv_dw_w[j], conv_dw_b[j], conv_ln_g[j], conv_ln_b[j])
            out['conv'] = nbuf[None]
            x2 = post_layer([cact.reshape(m, d)], z, w_out=conv_w_out[j], **tail)
        else:
            w = diff_w_in[j]
            q, kv, z = norm_linear(x2, norm_g[i], [w[:, :d], w[:, d:3 * d], w[:, 3 * d:]])
            lam_init = 0.8 - 0.6 * math.exp(-0.3 * i)
            q3, kv3 = q.reshape(n, seq, d), kv.reshape(n, seq, 2 * d)
            if sample:
                a = diff_sample(q3, diff_pool, j, page_table, kv3, diff_lam[j], diff_head_g[j], rel_bias, past,
                                lam_init)
            else:
                a = diff_flash_prompt(q3, kv3, diff_lam[j], diff_head_g[j], rel_bias, lam_init)
            out['diff'] = kv3.reshape(1, n, seq, 2, DF_HEADS, DF_DV)
            x2 = post_layer([a.reshape(m, d)], z, w_out=diff_w_out[j], **tail)
    return x2.reshape(n, seq, d), out


def kernel(x_prompt, x_sample, cache_nsa_kv, state_nsa_win, state_mlstm_C, state_mlstm_n, state_mlstm_m,
           state_conv, cache_diff_kv, page_table, p_prompt, p_sample, norm_g, final_norm_g, rel_bias,
           ple_w, ple_gate_w, ple_norm_g, nsa_w_in, nsa_w_out, mlstm_w_in, mlstm_b_if, mlstm_head_g,
           mlstm_w_out, conv_w_in, conv_dw_w, conv_dw_b, conv_ln_g, conv_ln_b, conv_w_out, diff_w_in,
           diff_lam, diff_head_g, diff_w_out):
    weights = (norm_g, final_norm_g, rel_bias, ple_w, ple_gate_w, ple_norm_g,
               nsa_w_in, nsa_w_out, mlstm_w_in, mlstm_b_if, mlstm_head_g, mlstm_w_out,
               conv_w_in, conv_dw_w, conv_dw_b, conv_ln_g, conv_ln_b, conv_w_out,
               diff_w_in, diff_lam, diff_head_g, diff_w_out)
    past = page_table.shape[1] * cache_nsa_kv.shape[2]
    y_p, o_p = _trunk(x_prompt, p_prompt, 0, None, None, None, None, None, None, None, None, weights)
    y_s, o_s = _trunk(x_sample, p_sample, past, page_table, cache_nsa_kv, state_nsa_win, state_mlstm_C,
                      state_mlstm_n, state_mlstm_m, state_conv, cache_diff_kv, weights)
    return (y_p, y_s, o_p['kv'], o_s['kv'], o_p['win'], o_s['win'], o_p['C'], o_s['C'], o_p['n'], o_s['n'],
            o_p['m'], o_s['m'], o_p['conv'], o_s['conv'], o_p['diff'], o_s['diff'])
```

```python
import functools
import math

import jax
import jax.numpy as jnp
import numpy as np
from jax import lax
from jax.experimental import pallas as pl
from jax.experimental.pallas import tpu as pltpu

F32 = jnp.float32
BF16 = jnp.bfloat16

D_MODEL = 1024
D_PLE = 256
N_BUCKETS = 32
MAX_DISTANCE = 128
EPS = 1e-6

NSA_HEADS = 16
NSA_KV_HEADS = 4
NSA_GROUP = 4
NSA_DH = 64
NSA_BLOCK = 32
NSA_TOPK = 16
NSA_WINDOW = 512
NSA_KVW = NSA_KV_HEADS * NSA_DH

ML_HEADS = 4
ML_DV = 256
ML_DK = 128
ML_QK = ML_HEADS * ML_DK
ML_CHUNK = 256

CONV_WIDTH = 31
CONV_HALO = 32

DF_HEADS = 8
DF_DK = 64
DF_DV = 128
DF_PAGES_PER_STEP = 16
DF_PAGES_PER_CHAIN = 8
DF_TILE = 256
DF_HEADS_PER_STEP = 2

LANES = 128
MASKED = -2e30
M_FLOOR = -1e30
FAR_DISTANCE = 113
VMEM_LIMIT = 48 * 1024 * 1024
ROW_TILE = 512
ATT_TILE = 256
COL_CHUNK = 512


def _cparams(*sem):
    return pltpu.CompilerParams(dimension_semantics=sem, vmem_limit_bytes=VMEM_LIMIT)


def _dot(a, b):
    return jnp.dot(a, b, preferred_element_type=F32)


def _dot_nt(a, b):
    return lax.dot_general(a, b, (((1,), (1,)), ((), ())), preferred_element_type=F32)


def _dot_tn(a, b):
    return lax.dot_general(a, b, (((0,), (0,)), ((), ())), preferred_element_type=F32)


def _dot_exact(a, b):
    return jnp.dot(a, b, preferred_element_type=F32, precision=lax.Precision.HIGHEST)


def _sigmoid(x):
    return 1.0 / (1.0 + jnp.exp(-x))


def _silu(x):
    return x * _sigmoid(x)


def _log_sigmoid(x):
    return jnp.minimum(x, 0.0) - jnp.log1p(jnp.exp(-jnp.abs(x)))


def _pad_cols(w, mult=LANES):
    pad = (-w.shape[-1]) % mult
    return w if pad == 0 else jnp.pad(w, [(0, 0)] * (w.ndim - 1) + [(0, pad)])


def _norm_linear_kernel(x_ref, g_ref, *refs, widths):
    w_refs, o_refs = refs[:len(widths)], refs[len(widths):]
    x = x_ref[...]
    xn = (x * lax.rsqrt(jnp.mean(x * x, -1, keepdims=True) + EPS) * g_ref[...]).astype(BF16)
    for w_ref, o_ref, width in zip(w_refs, o_refs, widths):
        for c0 in range(0, width, COL_CHUNK):
            c1 = min(c0 + COL_CHUNK, width)
            o_ref[:, c0:c1] = _dot(xn, w_ref[:, c0:c1])


def norm_linear(x2d, g, weights):
    m, d = x2d.shape
    ws = [_pad_cols(w).astype(BF16) for w in weights]
    widths = tuple(w.shape[1] for w in ws)
    tm = min(ROW_TILE, m)
    assert m % tm == 0
    outs = pl.pallas_call(
        functools.partial(_norm_linear_kernel, widths=widths),
        grid=(m // tm,),
        in_specs=[pl.BlockSpec((tm, d), lambda i: (i, 0)), pl.BlockSpec((1, d), lambda i: (0, 0))]
        + [pl.BlockSpec((d, wd), lambda i: (0, 0)) for wd in widths],
        out_specs=[pl.BlockSpec((tm, wd), lambda i: (i, 0)) for wd in widths],
        out_shape=[jax.ShapeDtypeStruct((m, wd), F32) for wd in widths],
        compiler_params=_cparams("parallel"),
        name="norm_linear",
    )(x2d, g.reshape(1, d), *ws)
    return outs


def _post_kernel(*refs, n_a, final):
    a_refs = refs[:n_a]
    z_ref, x_ref, p_ref, wo_ref, png_ref, wg_ref, wp_ref = refs[n_a:n_a + 7]
    rest = refs[n_a + 7:]
    fng_ref = rest[0] if final else None
    o_ref = rest[-1]
    a = a_refs[0][...]
    for r in a_refs[1:]:
        a = a + r[...]
    h = (a * _silu(z_ref[...])).astype(BF16)
    x1 = x_ref[...] + _dot(h, wo_ref[...])
    r = (x1 * lax.rsqrt(jnp.mean(x1 * x1, -1, keepdims=True) + EPS) * png_ref[...]).astype(BF16)
    gate = _sigmoid(_dot(r, wg_ref[...]))
    x2 = x1 + gate * _dot(p_ref[...].astype(BF16), wp_ref[...])
    if final:
        x2 = x2 * lax.rsqrt(jnp.mean(x2 * x2, -1, keepdims=True) + EPS) * fng_ref[...]
    o_ref[...] = x2


def post_layer(a_list, z, x, p, w_out, ple_norm_g, ple_gate_w, ple_w, final_g=None):
    m, d = x.shape
    tm = min(ROW_TILE, m)
    assert m % tm == 0
    final = final_g is not None
    row = lambda i: (i, 0)
    fix = lambda i: (0, 0)
    in_specs = [pl.BlockSpec((tm, d), row) for _ in a_list] + [
        pl.BlockSpec((tm, d), row), pl.BlockSpec((tm, d), row), pl.BlockSpec((tm, D_PLE), row),
        pl.BlockSpec((d, d), fix), pl.BlockSpec((1, d), fix), pl.BlockSpec((d, d), fix),
        pl.BlockSpec((D_PLE, d), fix)]
    args = list(a_list) + [z, x, p, w_out.astype(BF16), ple_norm_g.reshape(1, d), ple_gate_w.astype(BF16),
                           ple_w.astype(BF16)]
    if final:
        in_specs.append(pl.BlockSpec((1, d), fix))
        args.append(final_g.reshape(1, d))
    return pl.pallas_call(
        functools.partial(_post_kernel, n_a=len(a_list), final=final),
        grid=(m // tm,),
        in_specs=in_specs,
        out_specs=pl.BlockSpec((tm, d), row),
        out_shape=jax.ShapeDtypeStruct((m, d), F32),
        compiler_params=_cparams("parallel"),
        name="post_layer",
    )(*args)


def _t5_bucket(dist):
    d = jnp.maximum(dist, 0)
    exact = N_BUCKETS // 2
    df = jnp.maximum(d, 1).astype(F32)
    large = exact + (jnp.log(df / exact) / math.log(MAX_DISTANCE / exact) * (N_BUCKETS - exact)).astype(jnp.int32)
    return jnp.where(d < exact, d, jnp.minimum(large, N_BUCKETS - 1))


def _bias_table(rel_bias, dist, valid, rel_to_far=False):
    bucket = _t5_bucket(dist)[None]
    col = lambda k: rel_bias[k].reshape((-1,) + (1,) * dist.ndim)
    b = jnp.broadcast_to(col(N_BUCKETS - 1), (rel_bias.shape[1],) + dist.shape)
    for k in range(N_BUCKETS - 1):
        b = jnp.where(bucket == k, col(k), b)
    if rel_to_far:
        b = b - col(N_BUCKETS - 1)
    return jnp.where(valid[None], b, MASKED).astype(F32)


def _toeplitz_tiles(rel_bias, t, n_delta, max_dist=None, rel_to_far=False):
    i = jnp.arange(t)
    d = jnp.arange(n_delta)[:, None, None] * t + i[None, :, None] - i[None, None, :]
    valid = d >= 0
    if max_dist is not None:
        valid = valid & (d <= max_dist)
    return _bias_table(rel_bias, d, valid, rel_to_far)


def _far_distance_ok():
    d = np.arange(FAR_DISTANCE, FAR_DISTANCE + 8192, dtype=np.float64)
    large = 16 + np.floor(np.log(d / 16) / math.log(MAX_DISTANCE / 16) * 16 - 1e-6)
    return bool(np.all(large >= N_BUCKETS - 1))


def _flash_update(pieces, v_bf, m_ref, acc_ref, idx, l_ref=None, l_on_mxu=False):
    mx = pieces[0]
    for pc in pieces[1:]:
        mx = jnp.maximum(mx, pc)
    m_old = m_ref[idx]
    m_new = jnp.maximum(m_old, jnp.max(mx, -1, keepdims=True))
    alpha = jnp.exp(m_old - m_new)
    p = [jnp.exp(pc - m_new) for pc in pieces]
    if l_ref is not None and not l_on_mxu:
        total = p[0]
        for pc in p[1:]:
            total = total + pc
        l_ref[idx] = alpha * l_ref[idx] + jnp.sum(total, -1, keepdims=True)
    p = (p[0] if len(p) == 1 else jnp.concatenate(p, axis=1)).astype(BF16)
    if l_ref is not None and l_on_mxu:
        l_ref[idx] = alpha * l_ref[idx] + _dot(p, jnp.ones((p.shape[1], LANES), BF16))
    acc_ref[idx] = alpha * acc_ref[idx] + _dot(p, v_bf)
    m_ref[idx] = m_new


def _flash_step(pieces, v_aug, m_old, l_old, acc_old):
    mx = pieces[0]
    for pc in pieces[1:]:
        mx = jnp.maximum(mx, pc)
    m_new = jnp.maximum(m_old, jnp.max(mx, -1, keepdims=True))
    alpha = jnp.exp(m_old - m_new)
    p = [jnp.exp(pc - m_new) for pc in pieces]
    p = (p[0] if len(p) == 1 else jnp.concatenate(p, axis=1)).astype(BF16)
    res = _dot(p, v_aug)
    dv = acc_old.shape[1]
    return [m_new, alpha * l_old + res[:, dv:], alpha * acc_old + res[:, :dv]]


def _softmax_rows(s):
    m = jnp.maximum(jnp.max(s, -1, keepdims=True), M_FLOOR)
    e = jnp.exp(s - m)
    return e / jnp.maximum(jnp.sum(e, -1, keepdims=True), 1e-30)


def _select_blocks(imp, cur, k):
    imp_t = jnp.transpose(imp)
    blk = lax.broadcasted_iota(jnp.int32, imp_t.shape, 0).astype(F32)
    sel = jnp.where(blk == cur, float(NSA_GROUP + 1), jnp.where(blk < cur, imp_t, -1.0))
    chosen = jnp.zeros(imp_t.shape, F32)
    for _ in range(k):
        top = jnp.max(sel, 0, keepdims=True)
        first = jnp.min(jnp.where(sel == top, blk, float(LANES)), 0, keepdims=True)
        hit = blk == first
        chosen = jnp.where(hit, 1.0, chosen)
        sel = jnp.where(hit, -3.0, sel)
    return jnp.transpose(jnp.where(chosen > 0.0, 0.0, MASKED))


def _block_expand(first_pos, width):
    blk = lax.broadcasted_iota(jnp.int32, (LANES, width), 0)
    key_blk = (first_pos + lax.broadcasted_iota(jnp.int32, (LANES, width), 1)) // NSA_BLOCK
    return jnp.where(blk == key_blk, 1.0, 0.0).astype(BF16)


def _nsa_cmp_prompt_kernel(q_ref, rows_ref, g_ref, bias_ref, oc_ref, selb_ref, mk_scr, *, seq, tq, k_sel):
    i = pl.program_id(1)
    nb = seq // NSA_BLOCK

    @pl.when(i == 0)
    def _():
        means = jnp.sum(rows_ref[0].reshape(nb, NSA_BLOCK, 2 * NSA_KVW), axis=1) * (1.0 / NSA_BLOCK)
        mk_scr[...] = jnp.zeros(mk_scr.shape, F32)
        mk_scr[0:nb, :] = means

    q = q_ref[0]
    gate = _sigmoid(g_ref[0])
    pos = i * tq + lax.broadcasted_iota(jnp.int32, (1, tq), 1)
    cur = (pos // NSA_BLOCK).astype(F32)
    for kh in range(NSA_KV_HEADS):
        mk = mk_scr[:, kh * NSA_DH:(kh + 1) * NSA_DH].astype(BF16)
        mv = mk_scr[:, NSA_KVW + kh * NSA_DH:NSA_KVW + (kh + 1) * NSA_DH].astype(BF16)
        bias = jnp.concatenate([bias_ref[kh * NSA_GROUP + g] for g in range(NSA_GROUP)], axis=0)
        p = _softmax_rows(_dot_nt(_stack_group_queries(q, kh), mk) + bias)
        oc = _dot(p.astype(BF16), mv)
        imp = p[0:tq]
        for g in range(1, NSA_GROUP):
            imp = imp + p[g * tq:(g + 1) * tq]
        for g in range(NSA_GROUP):
            h = kh * NSA_GROUP + g
            oc_ref[0, :, h * NSA_DH:(h + 1) * NSA_DH] = oc[g * tq:(g + 1) * tq] * gate[:, h:h + 1]
        selb_ref[0, kh] = _select_blocks(imp, cur, k_sel)


def nsa_cmp_prompt(q, rows, gates, rel_bias):
    n, seq, _ = q.shape
    tq = min(ATT_TILE, seq)
    nb = seq // NSA_BLOCK
    assert nb <= LANES
    t = jnp.arange(seq)[:, None]
    bend = jnp.arange(LANES)[None, :] * NSA_BLOCK + NSA_BLOCK - 1
    bias = _bias_table(rel_bias, t - bend, bend <= t)
    return pl.pallas_call(
        functools.partial(_nsa_cmp_prompt_kernel, seq=seq, tq=tq, k_sel=min(NSA_TOPK, nb)),
        grid=(n, seq // tq),
        in_specs=[pl.BlockSpec((1, tq, D_MODEL), lambda b, i: (b, i, 0)),
                  pl.BlockSpec((1, seq, 2 * NSA_KVW), lambda b, i: (b, 0, 0)),
                  pl.BlockSpec((1, tq, LANES), lambda b, i: (b, i, 0)),
                  pl.BlockSpec((NSA_HEADS, tq, LANES), lambda b, i: (0, i, 0))],
        out_specs=[pl.BlockSpec((1, tq, D_MODEL), lambda b, i: (b, i, 0)),
                   pl.BlockSpec((1, NSA_KV_HEADS, tq, LANES), lambda b, i: (b, 0, i, 0))],
        out_shape=[jax.ShapeDtypeStruct((n, seq, D_MODEL), F32),
                   jax.ShapeDtypeStruct((n, NSA_KV_HEADS, seq, LANES), F32)],
        scratch_shapes=[pltpu.VMEM((LANES, 2 * NSA_KVW), F32)],
        compiler_params=_cparams("parallel", "arbitrary"),
        name="nsa_cmp_prompt",
    )(q, rows, gates, bias)


def _nsa_flash_prompt_kernel(*refs, t, window, gate_col):
    if window:
        q_ref, k_ref, v_ref, bias_ref, g_ref, o_ref, qs_scr, m_scr, acc_scr = refs
        selb_ref = None
    else:
        q_ref, k_ref, v_ref, bias_ref, g_ref, selb_ref, o_ref, qs_scr, m_scr, acc_scr = refs
    u = pl.program_id(1)
    qi = pl.program_id(2)
    heads = 2 * NSA_GROUP
    pieces = t // LANES

    m_scr[...] = jnp.full(m_scr.shape, M_FLOOR, F32)
    acc_scr[...] = jnp.zeros(acc_scr.shape, F32)
    q = q_ref[0]
    for h in range(heads):
        qs_scr[h] = (q[:, h * NSA_DH:(h + 1) * NSA_DH] * (NSA_DH ** -0.5)).astype(BF16)

    def span(start, deltas):
        start = start if isinstance(start, int) else pl.multiple_of(start, t)
        width = len(deltas) * t
        k = k_ref[0, pl.ds(start, width), :]
        v = v_ref[0, pl.ds(start, width), :]
        lane = lax.broadcasted_iota(jnp.int32, (width, LANES), 1)
        if not window:
            expand = _block_expand(start, width)
        for kv in range(2):
            k_bf = k[:, kv * NSA_DH:(kv + 1) * NSA_DH].astype(BF16)
            v_aug = jnp.where((lane < NSA_DH) if kv == 0 else (lane >= NSA_DH), v, 1.0).astype(BF16)
            if not window:
                sel_bias = _dot(selb_ref[0, kv].astype(BF16), expand)
            for g in range(NSA_GROUP):
                h = kv * NSA_GROUP + g
                s = _dot_nt(qs_scr[h], k_bf)
                ps = []
                for j, delta in enumerate(deltas):
                    for i in range(pieces):
                        cols = slice(j * t + i * LANES, j * t + (i + 1) * LANES)
                        pc = s[:, cols]
                        if delta is not None:
                            pc = pc + bias_ref[h, delta, :, i * LANES:(i + 1) * LANES]
                        if not window:
                            pc = pc + sel_bias[:, cols]
                        ps.append(pc)
                _flash_update(ps, v_aug, m_scr, acc_scr, h)

    if window:
        n_win = NSA_WINDOW // t
        for back in range(n_win + 1):
            @pl.when((qi >= back) if back == n_win else (qi == back))
            def _(back=back):
                span((qi - back) * t, list(range(back, -1, -1)))
    else:
        def far_body(ki, carry):
            span(ki * t, [None])
            return carry
        lax.fori_loop(0, jnp.maximum(qi - 1, 0), far_body, 0)

        @pl.when(qi >= 1)
        def _():
            span((qi - 1) * t, [1])
        span(qi * t, [0])

    gate = _sigmoid(g_ref[0])
    for h in range(heads):
        col = gate_col + h
        g0 = jnp.where(u == 0, gate[:, col:col + 1], gate[:, col + heads:col + heads + 1])
        acc = acc_scr[h]
        lo, hi = acc[:, 0:NSA_DH], acc[:, NSA_DH:2 * NSA_DH]
        o, l = (lo, hi) if h < NSA_GROUP else (hi, lo)
        o_ref[0, :, h * NSA_DH:(h + 1) * NSA_DH] = o / jnp.maximum(l, 1e-30) * g0


def nsa_flash_prompt(q, kv_arr, k_blk, v_blk, gates, rel_bias, selb, window):
    n, seq, _ = q.shape
    t = min(ATT_TILE, seq)
    nq = seq // t
    heads = 2 * NSA_GROUP
    assert t % LANES == 0 and _far_distance_ok()
    if window:
        assert NSA_WINDOW % t == 0
        n_delta = NSA_WINDOW // t + 1
        tiles = _toeplitz_tiles(rel_bias, t, n_delta, NSA_WINDOW)
    else:
        assert t + 1 >= FAR_DISTANCE
        n_delta = 2
        tiles = _toeplitz_tiles(rel_bias, t, n_delta, rel_to_far=True)
    in_specs = [pl.BlockSpec((1, t, heads * NSA_DH), lambda b, u, qi: (b, qi, u)),
                pl.BlockSpec((1, seq, LANES), lambda b, u, qi: (b, 0, k_blk + u)),
                pl.BlockSpec((1, seq, LANES), lambda b, u, qi: (b, 0, v_blk + u)),
                pl.BlockSpec((heads, n_delta, t, t), lambda b, u, qi: (u, 0, 0, 0)),
                pl.BlockSpec((1, t, LANES), lambda b, u, qi: (b, qi, 0))]
    args = [q, kv_arr, kv_arr, tiles, gates]
    if not window:
        in_specs.append(pl.BlockSpec((1, 2, t, LANES), lambda b, u, qi: (b, u, qi, 0)))
        args.append(selb)
    return pl.pallas_call(
        functools.partial(_nsa_flash_prompt_kernel, t=t, window=window, gate_col=2 * NSA_HEADS if window else NSA_HEADS),
        grid=(n, 2, nq),
        in_specs=in_specs,
        out_specs=pl.BlockSpec((1, t, heads * NSA_DH), lambda b, u, qi: (b, qi, u)),
        out_shape=jax.ShapeDtypeStruct((n, seq, D_MODEL), F32),
        scratch_shapes=[pltpu.VMEM((heads, t, NSA_DH), BF16), pltpu.VMEM((heads, t, LANES), F32),
                        pltpu.VMEM((heads, t, LANES), F32)],
        compiler_params=_cparams("parallel", "parallel", "arbitrary"),
        name="nsa_window_prompt" if window else "nsa_selected_prompt",
    )(*args)


def _diff_lambda(lam_ref, lam_init):
    lp = lam_ref[...]
    s1 = jnp.sum(lp[0:1] * lp[1:2], -1, keepdims=True)
    s2 = jnp.sum(lp[2:3] * lp[3:4], -1, keepdims=True)
    return jnp.exp(s1) - jnp.exp(s2) + lam_init


def _diff_flash_prompt_kernel(q_ref, k_ref, v_ref, bias_ref, lam_ref, hg_ref, o_ref, qs_scr, *, t, lam_init):
    qi = pl.program_id(2)
    pieces = t // LANES
    maps = 2 * DF_HEADS_PER_STEP
    q = q_ref[0]
    for c in range(maps):
        qs_scr[c] = (q[:, c * DF_DK:(c + 1) * DF_DK] * (DF_DK ** -0.5)).astype(BF16)
    def span(start, deltas, state):
        start = start if isinstance(start, int) else pl.multiple_of(start, t)
        width = len(deltas) * t
        k = k_ref[0, pl.ds(start, width), :]
        v = v_ref[0, pl.ds(start, width), :]
        ones = jnp.ones((width, LANES), BF16)
        new = []
        for c in range(maps):
            v_aug = jnp.concatenate([v[:, (c // 2) * DF_DV:(c // 2 + 1) * DF_DV].astype(BF16), ones], axis=1)
            s = _dot_nt(qs_scr[c], k[:, c * DF_DK:(c + 1) * DF_DK].astype(BF16))
            ps = []
            for j, delta in enumerate(deltas):
                for i in range(pieces):
                    pc = s[:, j * t + i * LANES:j * t + (i + 1) * LANES]
                    if delta is not None:
                        pc = pc + bias_ref[c, delta, :, i * LANES:(i + 1) * LANES]
                    ps.append(pc)
            new += _flash_step(ps, v_aug, *state[3 * c:3 * c + 3])
        return tuple(new)

    init = (jnp.full((t, LANES), M_FLOOR, F32), jnp.zeros((t, LANES), F32), jnp.zeros((t, DF_DV), F32)) * maps
    n_far = jnp.maximum(qi - 1, 0)
    quads = n_far // 4
    state = lax.fori_loop(0, quads, lambda ki, st: span(ki * (4 * t), [None] * 4, st), init)
    state = lax.cond(n_far % 4 >= 2, lambda st: span(quads * (4 * t), [None, None], st), lambda st: st, state)
    state = lax.cond(n_far % 2 == 1, lambda st: span((n_far - 1) * t, [None], st), lambda st: st, state)
    state = lax.cond(qi >= 1, lambda st: span(n_far * t, [1, 0], st), lambda st: span(0, [0], st), state)

    lam = _diff_lambda(lam_ref, lam_init)
    for hh in range(DF_HEADS_PER_STEP):
        (_, l0, a0), (_, l1, a1) = state[6 * hh:6 * hh + 3], state[6 * hh + 3:6 * hh + 6]
        o = a0 / jnp.maximum(l0, 1e-30) - lam * (a1 / jnp.maximum(l1, 1e-30))
        o = o * lax.rsqrt(jnp.mean(o * o, -1, keepdims=True) + EPS) * hg_ref[hh]
        o_ref[0, :, hh * DF_DV:(hh + 1) * DF_DV] = o * (1.0 - lam_init)


def diff_flash_prompt(q, kv, lam_p, head_g, rel_bias, lam_init):
    n, seq, _ = q.shape
    t = min(DF_TILE, seq)
    nq = seq // t
    hps = DF_HEADS_PER_STEP
    width = hps * DF_DV
    assert t % LANES == 0 and _far_distance_ok() and t + 1 >= FAR_DISTANCE and DF_HEADS % hps == 0
    tiles = _toeplitz_tiles(rel_bias, t, 2, rel_to_far=True)
    return pl.pallas_call(
        functools.partial(_diff_flash_prompt_kernel, t=t, lam_init=lam_init),
        grid=(n, DF_HEADS // hps, nq),
        in_specs=[pl.BlockSpec((1, t, width), lambda b, h, qi: (b, qi, h)),
                  pl.BlockSpec((1, seq, width), lambda b, h, qi: (b, 0, h)),
                  pl.BlockSpec((1, seq, width), lambda b, h, qi: (b, 0, DF_HEADS // hps + h)),
                  pl.BlockSpec((2 * hps, 2, t, t), lambda b, h, qi: (h, 0, 0, 0)),
                  pl.BlockSpec((4, DF_DK), lambda b, h, qi: (0, 0)),
                  pl.BlockSpec((hps, 1, DF_DV), lambda b, h, qi: (h, 0, 0))],
        out_specs=pl.BlockSpec((1, t, width), lambda b, h, qi: (b, qi, h)),
        out_shape=jax.ShapeDtypeStruct((n, seq, D_MODEL), F32),
        scratch_shapes=[pltpu.VMEM((2 * hps, t, DF_DK), BF16)],
        compiler_params=_cparams("parallel", "parallel", "arbitrary"),
        name="diff_flash_prompt",
    )(q, kv, kv, tiles, lam_p, head_g.reshape(DF_HEADS, 1, DF_DV))


def _mlstm_kernel(*refs, c, has_init):
    if has_init:
        (q_ref, k_ref, v_ref, gc_ref, gr_ref, og_ref, bc_ref, br_ref, hg_ref, c0_ref, n0_ref, m0_ref,
         a_ref, cout_ref, nout_ref, mout_ref, c_scr, n_scr, m_scr) = refs
    else:
        (q_ref, k_ref, v_ref, gc_ref, gr_ref, og_ref, bc_ref, br_ref, hg_ref,
         a_ref, cout_ref, nout_ref, mout_ref, c_scr, n_scr, m_scr) = refs
    j = pl.program_id(1)

    @pl.when(j == 0)
    def _():
        if has_init:
            c_scr[...] = c0_ref[0]
            n_scr[...] = n0_ref[0]
            m_scr[...] = m0_ref[0]
        else:
            c_scr[...] = jnp.zeros(c_scr.shape, F32)
            n_scr[...] = jnp.zeros(n_scr.shape, F32)
            m_scr[...] = jnp.zeros(m_scr.shape, F32)

    row = lax.broadcasted_iota(jnp.int32, (c, c), 0)
    col = lax.broadcasted_iota(jnp.int32, (c, c), 1)
    tri = row >= col
    gc = gc_ref[0] + bc_ref[...]
    gr = gr_ref[0] + br_ref[...]
    f_cols = _dot_exact(jnp.where(tri, 1.0, 0.0), _log_sigmoid(gc))
    f_rows = _dot_exact(_log_sigmoid(gr), jnp.where(row <= col, 1.0, 0.0))
    og = og_ref[0]
    for h in range(ML_HEADS):
        qc = q_ref[0, :, h * ML_DK:(h + 1) * ML_DK]
        kc = k_ref[0, :, h * ML_DK:(h + 1) * ML_DK] * (ML_DK ** -0.5)
        vc = v_ref[0, :, h * ML_DV:(h + 1) * ML_DV]
        q_bf, k_bf = qc.astype(BF16), kc.astype(BF16)
        f_col = f_cols[:, ML_HEADS + h:ML_HEADS + h + 1]
        f_row = f_rows[ML_HEADS + h:ML_HEADS + h + 1, :]
        i_col = gc[:, h:h + 1]
        i_row = gr[h:h + 1, :]
        m_prev = m_scr[h:h + 1, 0:1]
        dm = jnp.where(tri, f_col - f_row + i_row, -1e30)
        inter = f_col + m_prev
        mt = jnp.maximum(jnp.max(dm, -1, keepdims=True), inter)
        w = jnp.where(tri, jnp.exp(dm - mt), 0.0)
        s = _dot_nt(q_bf, k_bf) * w
        a = jnp.exp(inter - mt)
        c_old = c_scr[h]
        n_old = n_scr[h:h + 1, :]
        num = _dot(s.astype(BF16), vc.astype(BF16)) + a * _dot_nt(q_bf, c_old.astype(BF16))
        den = jnp.sum(s, -1, keepdims=True) + a * jnp.sum(qc * n_old, -1, keepdims=True)
        hh = num / jnp.maximum(jnp.abs(den), jnp.exp(-mt))
        m_new = mt[c - 1:c, :]
        wend = jnp.exp(f_col[c - 1:c, :] - f_col + i_col - m_new)
        decay = jnp.exp(inter[c - 1:c, :] - m_new)
        c_scr[h] = decay * c_old + _dot_tn((wend * vc).astype(BF16), k_bf)
        n_scr[h:h + 1, :] = decay * n_old + jnp.sum(wend * kc, 0, keepdims=True)
        m_scr[h:h + 1, :] = jnp.broadcast_to(m_new, (1, LANES))
        hn = hh * lax.rsqrt(jnp.mean(hh * hh, -1, keepdims=True) + EPS) * hg_ref[h:h + 1, :]
        a_ref[0, :, h * ML_DV:(h + 1) * ML_DV] = _sigmoid(og[:, h * ML_DV:(h + 1) * ML_DV]) * hn

    @pl.when(j == pl.num_programs(1) - 1)
    def _():
        cout_ref[0] = c_scr[...]
        nout_ref[0] = n_scr[...]
        mout_ref[0] = m_scr[...]


def mlstm_cell(q, k, v, gates, og, b_if, head_g, state):
    n, seq, _ = q.shape
    c = min(ML_CHUNK, seq)
    nc = seq // c
    has_init = state is not None
    gates_t = jnp.swapaxes(gates[:, :, :2 * ML_HEADS], 1, 2)
    b_flat = b_if.reshape(2 * ML_HEADS)
    b_col = jnp.pad(b_flat, (0, LANES - 2 * ML_HEADS)).reshape(1, LANES)
    b_row = b_flat.reshape(2 * ML_HEADS, 1)
    tok = lambda w: pl.BlockSpec((1, c, w), lambda b, j: (b, j, 0))
    fix2 = lambda s: pl.BlockSpec(s, lambda b, j: (0, 0))
    in_specs = [tok(ML_QK), tok(ML_QK), tok(D_MODEL), tok(LANES),
                pl.BlockSpec((1, 2 * ML_HEADS, c), lambda b, j: (b, 0, j)), tok(D_MODEL),
                fix2((1, LANES)), fix2((2 * ML_HEADS, 1)), fix2((ML_HEADS, ML_DV))]
    args = [q, k, v, gates, gates_t, og, b_col, b_row, head_g]
    st4 = pl.BlockSpec((1, ML_HEADS, ML_DV, ML_DK), lambda b, j: (b, 0, 0, 0))
    st3 = pl.BlockSpec((1, ML_HEADS, LANES), lambda b, j: (b, 0, 0))
    if has_init:
        c0, n0, m0 = state
        in_specs += [st4, st3, st3]
        args += [c0, n0, jnp.broadcast_to(m0[:, :, None], (n, ML_HEADS, LANES))]
    a, c_new, n_new, m_new = pl.pallas_call(
        functools.partial(_mlstm_kernel, c=c, has_init=has_init),
        grid=(n, nc),
        in_specs=in_specs,
        out_specs=[tok(D_MODEL), st4, st3, st3],
        out_shape=[jax.ShapeDtypeStruct((n, seq, D_MODEL), F32),
                   jax.ShapeDtypeStruct((n, ML_HEADS, ML_DV, ML_DK), F32),
                   jax.ShapeDtypeStruct((n, ML_HEADS, ML_DK), F32),
                   jax.ShapeDtypeStruct((n, ML_HEADS, LANES), F32)],
        scratch_shapes=[pltpu.VMEM((ML_HEADS, ML_DV, ML_DK), F32), pltpu.VMEM((ML_HEADS, ML_DK), F32),
                        pltpu.VMEM((ML_HEADS, LANES), F32)],
        compiler_params=_cparams("parallel", "arbitrary"),
        name="mlstm",
    )(*args)
    return a, c_new, n_new, m_new[:, :, 0]


def _conv_kernel(a_ref, b_ref, buf_ref, w_ref, dwb_ref, lng_ref, lnb_ref, o_ref, nbuf_ref, ctx_scr, *, tt):
    j = pl.program_id(1)

    @pl.when(j == 0)
    def _():
        ctx_scr[0:CONV_HALO, :] = buf_ref[0]

    @pl.when(j > 0)
    def _():
        ctx_scr[0:CONV_HALO, :] = ctx_scr[tt:tt + CONV_HALO, :]

    ctx_scr[CONV_HALO:CONV_HALO + tt, :] = a_ref[0] * _sigmoid(b_ref[0])
    off = CONV_HALO - (CONV_WIDTH - 1)
    acc = jnp.zeros((tt, D_MODEL), F32) + dwb_ref[...]
    sub = 8
    for r in range(sub):
        lead = off + r
        base, shift = (lead // sub) * sub, lead % sub
        rows = tt if shift == 0 else tt + sub
        part = None
        for tap in range(r, CONV_WIDTH, sub):
            lo = base + tap - r
            term = ctx_scr[lo:lo + rows, :] * w_ref[tap:tap + 1, :]
            part = term if part is None else part + term
        acc = acc + part[shift:shift + tt, :]
    mu = jnp.mean(acc, -1, keepdims=True)
    xc = acc - mu
    var = jnp.mean(xc * xc, -1, keepdims=True)
    o_ref[0] = _silu(xc * lax.rsqrt(var + EPS) * lng_ref[...] + lnb_ref[...])
    nbuf_ref[0] = ctx_scr[tt:tt + CONV_HALO, :]


def conv_block(a, b, buf, dw_w, dw_b, ln_g, ln_b):
    n, seq, d = a.shape
    tt = min(ATT_TILE, seq)
    assert seq == tt or tt >= CONV_HALO
    hist = CONV_WIDTH - 1
    if buf is None:
        buf_pad = jnp.zeros((n, CONV_HALO, d), F32)
    else:
        buf_pad = jnp.pad(buf, ((0, 0), (CONV_HALO - hist, 0), (0, 0)))
    w_pad = jnp.pad(dw_w, ((0, CONV_HALO - CONV_WIDTH), (0, 0)))
    tok = pl.BlockSpec((1, tt, d), lambda bb, j: (bb, j, 0))
    vec = pl.BlockSpec((1, d), lambda bb, j: (0, 0))
    halo = pl.BlockSpec((1, CONV_HALO, d), lambda bb, j: (bb, 0, 0))
    out, nbuf = pl.pallas_call(
        functools.partial(_conv_kernel, tt=tt),
        grid=(n, seq // tt),
        in_specs=[tok, tok, halo, pl.BlockSpec((CONV_HALO, d), lambda bb, j: (0, 0)), vec, vec, vec],
        out_specs=[tok, halo],
        out_shape=[jax.ShapeDtypeStruct((n, seq, d), F32), jax.ShapeDtypeStruct((n, CONV_HALO, d), F32)],
        scratch_shapes=[pltpu.VMEM((CONV_HALO + tt, d), F32)],
        compiler_params=_cparams("parallel", "arbitrary"),
        name="conv_block",
    )(a, b, buf_pad, w_pad, dw_b.reshape(1, d), ln_g.reshape(1, d), ln_b.reshape(1, d))
    return out, nbuf[:, CONV_HALO - hist:, :]


def _page_spec(block, col_block, j):
    return pl.BlockSpec(block, lambda b, pt: (pt[b, j], 0, col_block))


def _page_means_kernel(pt_ref, *refs):
    page_refs, o_ref = refs[:-1], refs[-1]
    for j, page_ref in enumerate(page_refs):
        x = page_ref[0]
        blocks = x.shape[0] // NSA_BLOCK
        o_ref[0, j * blocks:(j + 1) * blocks, :] = (
            jnp.sum(x.reshape(blocks, NSA_BLOCK, x.shape[1]), axis=1) * (1.0 / NSA_BLOCK))


def nsa_page_means(pool, page_table):
    n, n_pages = page_table.shape
    page = pool.shape[1]
    nbp = n_pages * page // NSA_BLOCK
    return pl.pallas_call(
        _page_means_kernel,
        grid_spec=pltpu.PrefetchScalarGridSpec(
            num_scalar_prefetch=1, grid=(n,),
            in_specs=[_page_spec((1, page, 2 * NSA_KVW), 0, j) for j in range(n_pages)],
            out_specs=pl.BlockSpec((1, nbp, 2 * NSA_KVW), lambda b, pt: (b, 0, 0))),
        out_shape=jax.ShapeDtypeStruct((n, nbp, 2 * NSA_KVW), F32),
        compiler_params=_cparams("parallel"),
        name="nsa_page_means",
    )(page_table, *([pool] * n_pages))


def _stack_group_queries(q, kh):
    parts = [q[:, (kh * NSA_GROUP + g) * NSA_DH:(kh * NSA_GROUP + g + 1) * NSA_DH] for g in range(NSA_GROUP)]
    return (jnp.concatenate(parts, axis=0) * (NSA_DH ** -0.5)).astype(BF16)


def _nsa_sample_local_kernel(q_ref, means_ref, rows_ref, win_ref, neww_ref, g_ref, bc_ref, bw_ref,
                             oc_ref, ow_ref, selb_ref, wout_ref, mk_scr, *, past, seq, k_sel):
    nbp = means_ref.shape[1]
    wb = win_ref.shape[1]
    mk_scr[...] = jnp.zeros(mk_scr.shape, F32)
    mk_scr[0:nbp, :] = means_ref[0]
    mk_scr[nbp:nbp + 1, :] = jnp.sum(rows_ref[0][:, 0:2 * NSA_KVW], 0, keepdims=True) * (1.0 / NSA_BLOCK)
    neww = neww_ref[0]
    wout_ref[0, 0:wb - seq, :] = win_ref[0, seq:wb, :]
    wout_ref[0, wb - seq:wb, :] = neww
    new_pad = jnp.concatenate([neww, jnp.zeros((LANES - seq, 2 * NSA_KVW), F32)], axis=0)
    q = q_ref[0]
    gate = _sigmoid(g_ref[0])
    imps = []
    for kh in range(NSA_KV_HEADS):
        qs = _stack_group_queries(q, kh)
        lo, hi = kh * NSA_DH, (kh + 1) * NSA_DH
        p = _softmax_rows(_dot_nt(qs, mk_scr[:, lo:hi].astype(BF16)) + bc_ref[kh])
        oc = _dot(p.astype(BF16), mk_scr[:, NSA_KVW + lo:NSA_KVW + hi].astype(BF16))
        imp = p[0:seq]
        for g in range(1, NSA_GROUP):
            imp = imp + p[g * seq:(g + 1) * seq]
        imps.append(imp)
        s_old = _dot_nt(qs, win_ref[0, :, lo:hi].astype(BF16)) + bw_ref[kh, :, 0:wb]
        s_new = _dot_nt(qs, new_pad[:, lo:hi].astype(BF16)) + bw_ref[kh, :, wb:wb + LANES]
        mw = jnp.maximum(jnp.maximum(jnp.max(s_old, -1, keepdims=True), jnp.max(s_new, -1, keepdims=True)), M_FLOOR)
        p_old, p_new = jnp.exp(s_old - mw), jnp.exp(s_new - mw)
        lw = jnp.sum(p_old, -1, keepdims=True) + jnp.sum(p_new, -1, keepdims=True)
        ow = (_dot(p_old.astype(BF16), win_ref[0, :, NSA_KVW + lo:NSA_KVW + hi].astype(BF16))
              + _dot(p_new.astype(BF16), new_pad[:, NSA_KVW + lo:NSA_KVW + hi].astype(BF16))) / jnp.maximum(lw, 1e-30)
        for g in range(NSA_GROUP):
            h = kh * NSA_GROUP + g
            oc_ref[0, :, h * NSA_DH:(h + 1) * NSA_DH] = oc[g * seq:(g + 1) * seq] * gate[:, h:h + 1]
            ow_ref[0, :, h * NSA_DH:(h + 1) * NSA_DH] = (
                ow[g * seq:(g + 1) * seq] * gate[:, 2 * NSA_HEADS + h:2 * NSA_HEADS + h + 1])
    rows_sel = NSA_KV_HEADS * seq
    imp_all = jnp.concatenate(imps + [jnp.zeros((LANES - rows_sel, LANES), F32)], axis=0)
    pos = past + lax.broadcasted_iota(jnp.int32, (1, LANES), 1) % seq
    sel = _select_blocks(imp_all, (pos // NSA_BLOCK).astype(F32), k_sel)
    for kh in range(NSA_KV_HEADS):
        selb_ref[0, kh] = sel[kh * seq:(kh + 1) * seq]


def _group_rows(tab):
    return tab.reshape(NSA_KV_HEADS, NSA_GROUP * tab.shape[1], tab.shape[2])


def nsa_sample_local(q, means, rows, win_buf, new_win, gates, rel_bias, past):
    n, seq, _ = q.shape
    nbp = means.shape[1]
    wb = win_buf.shape[1]
    nb = -(-(past + seq) // NSA_BLOCK)
    assert nbp + 1 == nb and nb <= LANES and seq % 8 == 0 and NSA_KV_HEADS * seq <= LANES
    lw = wb + LANES
    assert wb % LANES == 0 and seq <= LANES
    qpos = past + jnp.arange(seq)[:, None]
    bend = jnp.arange(LANES)[None, :] * NSA_BLOCK + NSA_BLOCK - 1
    bc = _group_rows(_bias_table(rel_bias, qpos - bend, bend <= qpos))
    cidx = jnp.arange(lw)[None, :]
    wpos = past - wb + cidx
    dw = qpos - wpos
    bw = _group_rows(_bias_table(rel_bias, dw, (dw >= 0) & (dw <= NSA_WINDOW) & (wpos >= 0) & (cidx < wb + seq)))
    rows_q = NSA_GROUP * seq
    per_b = lambda s: pl.BlockSpec((1,) + s, lambda b: (b,) + (0,) * len(s))
    fix = lambda s: pl.BlockSpec(s, lambda b: (0,) * len(s))
    return pl.pallas_call(
        functools.partial(_nsa_sample_local_kernel, past=past, seq=seq, k_sel=min(NSA_TOPK, nb)),
        grid=(n,),
        in_specs=[per_b((seq, D_MODEL)), per_b((nbp, 2 * NSA_KVW)), per_b((seq, D_MODEL)), per_b((wb, 2 * NSA_KVW)),
                  per_b((seq, 2 * NSA_KVW)), per_b((seq, LANES)), fix((NSA_KV_HEADS, rows_q, LANES)),
                  fix((NSA_KV_HEADS, rows_q, lw))],
        out_specs=[per_b((seq, D_MODEL)), per_b((seq, D_MODEL)), per_b((NSA_KV_HEADS, seq, LANES)),
                   per_b((wb, 2 * NSA_KVW))],
        out_shape=[jax.ShapeDtypeStruct((n, seq, D_MODEL), F32), jax.ShapeDtypeStruct((n, seq, D_MODEL), F32),
                   jax.ShapeDtypeStruct((n, NSA_KV_HEADS, seq, LANES), F32),
                   jax.ShapeDtypeStruct((n, wb, 2 * NSA_KVW), F32)],
        scratch_shapes=[pltpu.VMEM((LANES, 2 * NSA_KVW), F32)],
        compiler_params=_cparams("parallel"),
        name="nsa_sample_local",
    )(q, means, rows, win_buf, new_win, gates, bc, bw)


def _nsa_sample_selected_kernel(pt_ref, q_ref, rows_ref, selb_ref, bias_ref, g_ref, *refs, seq, page, past):
    n_pages = len(refs) - 2
    page_refs, o_ref, new_scr = refs[:n_pages], refs[n_pages], refs[n_pages + 1]
    new_scr[...] = jnp.zeros(new_scr.shape, F32)
    new_scr[0:seq, :] = rows_ref[0][:, 2 * NSA_KVW:4 * NSA_KVW]
    q = q_ref[0]
    gate = _sigmoid(g_ref[0])
    sources = [(page_refs[j], j * page) for j in range(n_pages)] + [(None, past)]
    expands = [_block_expand(first_pos, page) for _, first_pos in sources]
    for kh in range(NSA_KV_HEADS):
        lo, hi = kh * NSA_DH, (kh + 1) * NSA_DH
        qs = _stack_group_queries(q, kh)
        sel = selb_ref[0, kh].astype(BF16)
        scores = []
        for j, (ref, _) in enumerate(sources):
            k_bf = (new_scr[:, lo:hi] if ref is None else ref[0, :, lo:hi]).astype(BF16)
            sb = _dot(sel, expands[j])
            scores.append(_dot_nt(qs, k_bf) + bias_ref[j, kh] + jnp.concatenate([sb] * NSA_GROUP, axis=0))
        mx = scores[0]
        for s in scores[1:]:
            mx = jnp.maximum(mx, s)
        m = jnp.maximum(jnp.max(mx, -1, keepdims=True), M_FLOOR)
        total = None
        acc = None
        for j, (ref, _) in enumerate(sources):
            p = jnp.exp(scores[j] - m)
            v_bf = (new_scr[:, NSA_KVW + lo:NSA_KVW + hi] if ref is None
                    else ref[0, :, NSA_KVW + lo:NSA_KVW + hi]).astype(BF16)
            pv = _dot(p.astype(BF16), v_bf)
            total = p if total is None else total + p
            acc = pv if acc is None else acc + pv
        o = acc / jnp.maximum(jnp.sum(total, -1, keepdims=True), 1e-30)
        for g in range(NSA_GROUP):
            h = kh * NSA_GROUP + g
            o_ref[0, :, h * NSA_DH:(h + 1) * NSA_DH] = (
                o[g * seq:(g + 1) * seq] * gate[:, NSA_HEADS + h:NSA_HEADS + h + 1])


def nsa_sample_selected(q, pool, page_table, rows, selb, gates, rel_bias, past):
    n, seq, _ = q.shape
    n_pages = page_table.shape[1]
    page = pool.shape[1]
    assert page == LANES and seq <= page
    qpos = past + jnp.arange(seq)[:, None]
    j = jnp.arange(page)[None, :]
    d_past = qpos[None] - (jnp.arange(n_pages)[:, None, None] * page + j[None])
    d_new = (qpos - past) - j
    dist = jnp.concatenate([d_past, d_new[None]], 0)
    valid = jnp.concatenate([d_past >= 0, ((d_new >= 0) & (j < seq))[None]], 0)
    bias = _bias_table(rel_bias, dist, valid)
    bias = jnp.moveaxis(bias, 1, 0).reshape(n_pages + 1, NSA_KV_HEADS, NSA_GROUP * seq, page)
    rows_q = NSA_GROUP * seq
    per_b = lambda s: pl.BlockSpec((1,) + s, lambda b, pt: (b,) + (0,) * len(s))
    return pl.pallas_call(
        functools.partial(_nsa_sample_selected_kernel, seq=seq, page=page, past=past),
        grid_spec=pltpu.PrefetchScalarGridSpec(
            num_scalar_prefetch=1, grid=(n,),
            in_specs=[per_b((seq, D_MODEL)), per_b((seq, D_MODEL)), per_b((NSA_KV_HEADS, seq, LANES)),
                      pl.BlockSpec((n_pages + 1, NSA_KV_HEADS, rows_q, page), lambda b, pt: (0, 0, 0, 0)),
                      per_b((seq, LANES))]
            + [_page_spec((1, page, 2 * NSA_KVW), 1, jj) for jj in range(n_pages)],
            out_specs=per_b((seq, D_MODEL)),
            scratch_shapes=[pltpu.VMEM((page, 2 * NSA_KVW), F32)]),
        out_shape=jax.ShapeDtypeStruct((n, seq, D_MODEL), F32),
        compiler_params=_cparams("parallel"),
        name="nsa_sample_selected",
    )(page_table, q, rows, selb, bias, gates, *([pool] * n_pages))


def _diff_sample_kernel(pt_ref, q_ref, new_ref, tfar_ref, tnear_ref, tnew_ref, lam_ref, hg_ref, *refs,
                        seq, lam_init, pps, n_near):
    k_refs, v_refs = refs[:pps], refs[pps:2 * pps]
    o_ref, qall_scr, m_scr, l_scr, acc_scr = refs[2 * pps:]
    g = pl.program_id(1)
    last = pl.num_programs(1) - 1
    rows_k = k_refs[0].shape[0] * DF_HEADS
    pieces = rows_k // LANES

    @pl.when(g == 0)
    def _():
        m_scr[...] = jnp.full(m_scr.shape, M_FLOOR, F32)
        l_scr[...] = jnp.zeros(l_scr.shape, F32)
        acc_scr[...] = jnp.zeros(acc_scr.shape, F32)
        q = q_ref[0] * (DF_DK ** -0.5)
        half = lax.broadcasted_iota(jnp.int32, (seq, DF_DV), 1) // DF_DK
        parts = []
        for h in range(DF_HEADS):
            qh = q[:, h * DF_DV:(h + 1) * DF_DV]
            parts += [jnp.where(half == c, qh, 0.0) for c in range(2)]
        qall_scr[...] = jnp.concatenate(parts, axis=0).astype(BF16)

    qall = qall_scr[...]
    ppc = math.gcd(pps, DF_PAGES_PER_CHAIN)
    for j0 in range(0, pps, ppc):
        pages = range(j0, j0 + ppc)
        kf = jnp.concatenate([k_refs[jj][...].reshape(rows_k, DF_DV) for jj in pages], axis=0).astype(BF16)
        vf = jnp.concatenate([v_refs[jj][...].reshape(rows_k, DF_DV) for jj in pages], axis=0).astype(BF16)
        s = _dot_nt(qall, kf)
        ps = []
        for a, jj in enumerate(pages):
            near = jj - (pps - n_near)
            for i in range(pieces):
                tab = tfar_ref[:, i * LANES:(i + 1) * LANES]
                if near >= 0:
                    tab = jnp.where(g == last, tnear_ref[near, :, i * LANES:(i + 1) * LANES], tab)
                ps.append(s[:, a * rows_k + i * LANES:a * rows_k + (i + 1) * LANES] + tab)
        _flash_update(ps, vf, m_scr, acc_scr, slice(None), l_ref=l_scr)

    @pl.when(g == last)
    def _():
        new = new_ref[0]
        pad = jnp.zeros((LANES - DF_HEADS * seq, DF_DV), F32)
        kn = jnp.concatenate([new[:, h * DF_DV:(h + 1) * DF_DV] for h in range(DF_HEADS)] + [pad], axis=0)
        vn = jnp.concatenate([new[:, D_MODEL + h * DF_DV:D_MODEL + (h + 1) * DF_DV] for h in range(DF_HEADS)]
                             + [pad], axis=0)
        s = _dot_nt(qall, kn.astype(BF16)) + tnew_ref[...]
        _flash_update([s], vn.astype(BF16), m_scr, acc_scr, slice(None), l_ref=l_scr)
        lam = _diff_lambda(lam_ref, lam_init)
        acc = acc_scr[...] / jnp.maximum(l_scr[...], 1e-30)
        for h in range(DF_HEADS):
            r0 = 2 * h * seq
            o = acc[r0:r0 + seq] - lam * acc[r0 + seq:r0 + 2 * seq]
            o = o * lax.rsqrt(jnp.mean(o * o, -1, keepdims=True) + EPS) * hg_ref[:, h * DF_DV:(h + 1) * DF_DV]
            o_ref[0, :, h * DF_DV:(h + 1) * DF_DV] = o * (1.0 - lam_init)


def diff_sample(q, pool6, j_layer, page_table, kv_new, lam_p, head_g, rel_bias, past, lam_init):
    n, seq, _ = q.shape
    n_pages = page_table.shape[1]
    page = pool6.shape[2]
    pps = min(DF_PAGES_PER_STEP, n_pages)
    rows_q = 2 * DF_HEADS * seq
    rows_k = page * DF_HEADS
    assert n_pages % pps == 0 and rows_q == LANES and DF_HEADS * seq <= LANES and _far_distance_ok()
    n_near = sum(1 for p in range(n_pages) if past - (p * page + page - 1) < FAR_DISTANCE)
    assert 1 <= n_near <= pps
    head_of_row = jnp.arange(rows_q) // (2 * seq)
    same_head = head_of_row[:, None] == (jnp.arange(rows_k) % DF_HEADS)[None, :]
    far = jnp.repeat(rel_bias[N_BUCKETS - 1], seq)[:, None]
    t_far = jnp.where(same_head, far, MASKED).astype(F32)
    qpos = past + jnp.arange(seq)[:, None]
    near_pages = jnp.arange(n_pages - n_near, n_pages)
    d_near = qpos[None] - (near_pages[:, None, None] * page + jnp.arange(page)[None, None, :])
    b_near = _bias_table(rel_bias, d_near, d_near >= 0)
    b_near = jnp.moveaxis(b_near, 1, 0).reshape(n_near, rows_q, page)
    b_near = jnp.broadcast_to(b_near[..., None], (n_near, rows_q, page, DF_HEADS)).reshape(n_near, rows_q, rows_k)
    t_near = jnp.where(same_head[None], b_near, MASKED).astype(F32)
    d_new = jnp.arange(seq)[:, None] - jnp.arange(seq)[None, :]
    b_new = _bias_table(rel_bias, d_new, d_new >= 0).reshape(rows_q, 1, seq)
    b_new = jnp.broadcast_to(b_new, (rows_q, DF_HEADS, seq)).reshape(rows_q, DF_HEADS * seq)
    same_new = head_of_row[:, None] == (jnp.arange(DF_HEADS * seq) // seq)[None, :]
    t_new = jnp.pad(jnp.where(same_new, b_new, MASKED), ((0, 0), (0, LANES - DF_HEADS * seq)),
                    constant_values=MASKED).astype(F32)

    def page_spec(jj, c):
        return pl.BlockSpec((None, None, page, None, DF_HEADS, DF_DV),
                            lambda b, g, pt: (j_layer, pt[b, g * pps + jj], 0, c, 0, 0))

    fix = lambda s: pl.BlockSpec(s, lambda b, g, pt: (0,) * len(s))
    return pl.pallas_call(
        functools.partial(_diff_sample_kernel, seq=seq, lam_init=lam_init, pps=pps, n_near=n_near),
        grid_spec=pltpu.PrefetchScalarGridSpec(
            num_scalar_prefetch=1, grid=(n, n_pages // pps),
            in_specs=[pl.BlockSpec((1, seq, D_MODEL), lambda b, g, pt: (b, 0, 0)),
                      pl.BlockSpec((1, seq, 2 * D_MODEL), lambda b, g, pt: (b, 0, 0)),
                      fix((rows_q, rows_k)), fix((n_near, rows_q, rows_k)), fix((rows_q, LANES)),
                      fix((4, DF_DK)), fix((1, D_MODEL))]
            + [page_spec(jj, 0) for jj in range(pps)] + [page_spec(jj, 1) for jj in range(pps)],
            out_specs=pl.BlockSpec((1, seq, D_MODEL), lambda b, g, pt: (b, 0, 0)),
            scratch_shapes=[pltpu.VMEM((rows_q, DF_DV), BF16), pltpu.VMEM((rows_q, LANES), F32),
                            pltpu.VMEM((rows_q, LANES), F32), pltpu.VMEM((rows_q, DF_DV), F32)]),
        out_shape=jax.ShapeDtypeStruct((n, seq, D_MODEL), F32),
        compiler_params=_cparams("parallel", "arbitrary"),
        name="diff_sample",
    )(page_table, q, kv_new, t_far, t_near, t_new, lam_p, head_g.reshape(1, D_MODEL), *([pool6] * (2 * pps)))


def _trunk(x, p, past, page_table, nsa_pool, nsa_win, ml_c, ml_n, ml_m, conv_buf, diff_pool, weights):
    (norm_g, final_norm_g, rel_bias, ple_w, ple_gate_w, ple_norm_g,
     nsa_w_in, nsa_w_out, mlstm_w_in, mlstm_b_if, mlstm_head_g, mlstm_w_out,
     conv_w_in, conv_dw_w, conv_dw_b, conv_ln_g, conv_ln_b, conv_w_out,
     diff_w_in, diff_lam, diff_head_g, diff_w_out) = weights
    n, seq, d = x.shape
    m = n * seq
    sample = page_table is not None
    x2 = x.reshape(m, d)
    depth = norm_g.shape[0]
    out = {}
    for i in range(depth):
        kind, j = i % 4, i // 4
        assert j == 0
        p_i = p[i].reshape(m, D_PLE)
        tail = dict(x=x2, p=p_i, ple_norm_g=ple_norm_g[i], ple_gate_w=ple_gate_w[i], ple_w=ple_w[i],
                    final_g=final_norm_g if i == depth - 1 else None)
        if kind == 0:
            w = nsa_w_in[j]
            c0, c1, c2, c3 = D_MODEL, 2 * D_MODEL, 2 * D_MODEL + 2 * NSA_KVW, 2 * D_MODEL + 2 * NSA_KVW + 3 * NSA_HEADS
            q, rows, win, gates, z = norm_linear(x2, norm_g[i], [w[:, :c0], w[:, c0:c1], w[:, c1:c2], w[:, c2:c3], w[:, c3:]])
            q3, rows3, win3, gates3 = (t.reshape(n, seq, -1) for t in (q, rows, win, gates))
            if sample:
                pool = nsa_pool[j].reshape(nsa_pool.shape[1], nsa_pool.shape[2], D_MODEL)
                wb = nsa_win.shape[2]
                means = nsa_page_means(pool, page_table)
                oc, ow, selb, win_out = nsa_sample_local(q3, means, rows3, nsa_win[j].reshape(n, wb, 2 * NSA_KVW),
                                                         win3, gates3, rel_bias, past)
                os_ = nsa_sample_selected(q3, pool, page_table, rows3, selb, gates3, rel_bias, past)
            else:
                oc, selb = nsa_cmp_prompt(q3, rows3, gates3, rel_bias)
                os_ = nsa_flash_prompt(q3, rows3, 4, 6, gates3, rel_bias, selb, window=False)
                ow = nsa_flash_prompt(q3, win3, 0, 2, gates3, rel_bias, None, window=True)
                keep = min(NSA_WINDOW, seq)
                win_out = win3[:, seq - keep:]
                if keep < NSA_WINDOW:
                    win_out = jnp.concatenate([jnp.zeros((n, NSA_WINDOW - keep, 2 * NSA_KVW), F32), win_out], 1)
            out['kv'] = rows3.reshape(1, n, seq, 4, NSA_KV_HEADS, NSA_DH)
            out['win'] = win_out.reshape(1, n, win_out.shape[1], 2, NSA_KV_HEADS, NSA_DH)
            x2 = post_layer([t.reshape(m, d) for t in (oc, os_, ow)], z, w_out=nsa_w_out[j], **tail)
        elif kind == 1:
            w = mlstm_w_in[j]
            c0, c1, c2 = ML_QK, 2 * ML_QK, 2 * ML_QK + D_MODEL
            c3, c4 = c2 + 2 * ML_HEADS, c2 + 2 * ML_HEADS + D_MODEL
            q, k, v, gates, og, z = norm_linear(
                x2, norm_g[i], [w[:, :c0], w[:, c0:c1], w[:, c1:c2], w[:, c2:c3], w[:, c3:c4], w[:, c4:]])
            state = (ml_c[j], ml_n[j], ml_m[j]) if sample else None
            a, c_new, n_new, m_new = mlstm_cell(*(t.reshape(n, seq, -1) for t in (q, k, v, gates, og)),
                                                mlstm_b_if[j], mlstm_head_g[j], state)
            out['C'], out['n'], out['m'] = c_new[None], n_new[None], m_new[None]
            x2 = post_layer([a.reshape(m, d)], z, w_out=mlstm_w_out[j], **tail)
        elif kind == 2:
            w = conv_w_in[j]
            a, b, z = norm_linear(x2, norm_g[i], [w[:, :d], w[:, d:2 * d], w[:, 2 * d:]])
            cact, nbuf = conv_block(a.reshape(n, seq, d), b.reshape(n, seq, d), conv_buf[j] if sample else None,
                                    conv_dw_w[j], conv_dw_b[j], conv_ln_g[j], conv_ln_b[j])
            out['conv'] = nbuf[None]
            x2 = post_layer([cact.reshape(m, d)], z, w_out=conv_w_out[j], **tail)
        else:
            w = diff_w_in[j]
            q, kv, z = norm_linear(x2, norm_g[i], [w[:, :d], w[:, d:3 * d], w[:, 3 * d:]])
            lam_init = 0.8 - 0.6 * math.exp(-0.3 * i)
            q3, kv3 = q.reshape(n, seq, d), kv.reshape(n, seq, 2 * d)
            if sample:
                a = diff_sample(q3, diff_pool, j, page_table, kv3, diff_lam[j], diff_head_g[j], rel_bias, past,
                                lam_init)
            else:
                a = diff_flash_prompt(q3, kv3, diff_lam[j], diff_head_g[j], rel_bias, lam_init)
            out['diff'] = kv3.reshape(1, n, seq, 2, DF_HEADS, DF_DV)
            x2 = post_layer([a.reshape(m, d)], z, w_out=diff_w_out[j], **tail)
    return x2.reshape(n, seq, d), out


def kernel(x_prompt, x_sample, cache_nsa_kv, state_nsa_win, state_mlstm_C, state_mlstm_n, state_mlstm_m,
           state_conv, cache_diff_kv, page_table, p_prompt, p_sample, norm_g, final_norm_g, rel_bias,
           ple_w, ple_gate_w, ple_norm_g, nsa_w_in, nsa_w_out, mlstm_w_in, mlstm_b_if, mlstm_head_g,
           mlstm_w_out, conv_w_in, conv_dw_w, conv_dw_b, conv_ln_g, conv_ln_b, conv_w_out, diff_w_in,
           diff_lam, diff_head_g, diff_w_out):
    weights = (norm_g, final_norm_g, rel_bias, ple_w, ple_gate_w, ple_norm_g,
               nsa_w_in, nsa_w_out, mlstm_w_in, mlstm_b_if, mlstm_head_g, mlstm_w_out,
               conv_w_in, conv_dw_w, conv_dw_b, conv_ln_g, conv_ln_b, conv_w_out,
               diff_w_in, diff_lam, diff_head_g, diff_w_out)
    past = page_table.shape[1] * cache_nsa_kv.shape[2]
    y_p, o_p = _trunk(x_prompt, p_prompt, 0, None, None, None, None, None, None, None, None, weights)
    y_s, o_s = _trunk(x_sample, p_sample, past, page_table, cache_nsa_kv, state_nsa_win, state_mlstm_C,
                      state_mlstm_n, state_mlstm_m, state_conv, cache_diff_kv, weights)
    return (y_p, y_s, o_p['kv'], o_s['kv'], o_p['win'], o_s['win'], o_p['C'], o_s['C'], o_p['n'], o_s['n'],
            o_p['m'], o_s['m'], o_p['conv'], o_s['conv'], o_p['diff'], o_s['diff'])
```
